```python
import jax, jax.numpy as jnp
from jax import lax
import numpy as np

D_MODEL = 1024
BATCH = 2
SEQ = 8192
DEPTH = 4
DEC_BATCH = 128
DEC_SEQ = 8
PAST_LEN = 8192
PAGE_SIZE = 128

HEAD_DIM = 64
N_HEADS = D_MODEL // HEAD_DIM
N_KV_HEADS = 4
GQA_GROUP = N_HEADS // N_KV_HEADS
ATTN_W = N_HEADS * HEAD_DIM
KV_W = N_KV_HEADS * HEAD_DIM
WINDOW = 128
BLOCK = 128
CHUNK = 128
GMLP_CH = 128
GMLP_GROUPS = 6
GMLP_W = GMLP_GROUPS * GMLP_CH
D_FF = 2816
CONV_W = 3
EPS = 1e-5
NEG = -1e30
IN_SIZES = (ATTN_W, KV_W, KV_W, GMLP_W, GMLP_W, D_MODEL, D_MODEL)
IN_COLS = ATTN_W + 2 * KV_W + 2 * GMLP_W + 2 * D_MODEL

kernel_name = 'hybrid_swa_sink_gmlp_convffn_step'


def rmsnorm(x, g):
    xf = x.astype(jnp.float32)
    y = xf * lax.rsqrt(jnp.mean(xf * xf, axis=-1, keepdims=True) + EPS)
    return (y * g.astype(jnp.float32)).astype(x.dtype)


def layernorm(x, g, b):
    xf = x.astype(jnp.float32)
    mu = jnp.mean(xf, axis=-1, keepdims=True)
    var = jnp.mean(jnp.square(xf - mu), axis=-1, keepdims=True)
    y = (xf - mu) * lax.rsqrt(var + EPS)
    return (y * g.astype(jnp.float32) + b.astype(jnp.float32)).astype(x.dtype)


def split_in(z):
    idx = [int(i) for i in np.cumsum(IN_SIZES)[:-1]]
    return jnp.split(z, idx, axis=-1)


def sink_attention(q, k, v, mask, sinks):
    s = jnp.einsum('...qhgd,...khd->...hgqk', q.astype(jnp.float32), k.astype(jnp.float32))
    s = jnp.where(mask, s * (HEAD_DIM ** -0.5), NEG)
    sink = jnp.broadcast_to(sinks.astype(jnp.float32).reshape(N_KV_HEADS, GQA_GROUP, 1, 1),
                            s.shape[:-1] + (1,))
    m = jnp.maximum(jnp.max(s, axis=-1, keepdims=True), sink)
    p = jnp.exp(s - m)
    denom = jnp.sum(p, axis=-1, keepdims=True) + jnp.exp(sink - m)
    o = jnp.einsum('...hgqk,...khd->...qhgd', p / denom, v.astype(jnp.float32))
    return o.astype(v.dtype)


def attn_prompt(q, k, v, sinks):
    B, T = q.shape[:2]
    nb = T // BLOCK
    qb = q.reshape(B, nb, BLOCK, N_KV_HEADS, GQA_GROUP, HEAD_DIM)
    kb = k.reshape(B, nb, BLOCK, N_KV_HEADS, HEAD_DIM)
    vb = v.reshape(B, nb, BLOCK, N_KV_HEADS, HEAD_DIM)
    pad = ((0, 0), (1, 0), (0, 0), (0, 0), (0, 0))
    kk = jnp.concatenate([jnp.pad(kb, pad)[:, :-1], kb], axis=2)
    vv = jnp.concatenate([jnp.pad(vb, pad)[:, :-1], vb], axis=2)
    diff = jnp.arange(BLOCK)[:, None] + BLOCK - jnp.arange(2 * BLOCK)[None, :]
    band = (diff >= 0) & (diff < WINDOW)
    valid = (jnp.arange(nb)[:, None] > 0) | (jnp.arange(2 * BLOCK)[None, :] >= BLOCK)
    mask = band[None] & valid[:, None, :]
    o = sink_attention(qb, kk, vv, mask[:, None, None], sinks)
    return o.reshape(B, T, ATTN_W), k[:, T - WINDOW:], v[:, T - WINDOW:]


def attn_sample(q, k, v, buf_k, buf_v, sinks):
    B, T = q.shape[:2]
    L = buf_k.shape[1]
    kk = jnp.concatenate([buf_k, k], axis=1)
    vv = jnp.concatenate([buf_v, v], axis=1)
    diff = jnp.arange(T)[:, None] + L - jnp.arange(L + T)[None, :]
    mask = (diff >= 0) & (diff < WINDOW)
    o = sink_attention(q, kk, vv, mask, sinks)
    return o.reshape(B, T, ATTN_W), kk[:, T:], vv[:, T:]


def spatial_gating(u, vn, ws, bs):
    B, T = u.shape[:2]
    C = min(T, CHUNK)
    nc = T // C
    w = jnp.where(jnp.tril(jnp.ones((C, C), dtype=bool)), ws[:, :C, :C], 0.0)
    vb = vn.reshape(B, nc, C, GMLP_GROUPS, GMLP_CH)
    mix = jnp.einsum('gts,bnsgc->bntgc', w, vb) + bs[:, :C].T[:, :, None]
    return u * mix.reshape(B, T, GMLP_W)


def conv_ffn(h, w_up, conv_w, conv_b, w_down, prev):
    z = h @ w_up
    T = z.shape[1]
    zp = jnp.concatenate([prev, z], axis=1)
    c = conv_b + sum(zp[:, j:j + T] * conv_w[j] for j in range(CONV_W))
    a, b = jnp.split(c, 2, axis=-1)
    return (jax.nn.gelu(a) * b) @ w_down, zp[:, T:]


def layer(x, norm_mix, w_in, sinks, ln_g, ln_b, ws, bs, w_pa, w_pb, w_out,
          norm_ffn, w_up, conv_w, conv_b, w_down, win_k, win_v, conv_prev):
    B, T, _ = x.shape
    h = rmsnorm(x, norm_mix)
    q, k, v, gu, gv, ga, gb = split_in(h @ w_in)
    q = q.reshape(B, T, N_KV_HEADS, GQA_GROUP, HEAD_DIM)
    k = k.reshape(B, T, N_KV_HEADS, HEAD_DIM)
    v = v.reshape(B, T, N_KV_HEADS, HEAD_DIM)
    if win_k is None:
        o, nk, nv = attn_prompt(q, k, v, sinks)
    else:
        o, nk, nv = attn_sample(q, k, v, win_k, win_v, sinks)
    vn = layernorm(jax.nn.gelu(gv), ln_g, ln_b)
    s = spatial_gating(jax.nn.gelu(gu), vn, ws, bs)
    merged = jax.nn.sigmoid(ga) * (o @ w_pa) + jax.nn.sigmoid(gb) * (s @ w_pb)
    x = x + merged @ w_out
    f, nconv = conv_ffn(rmsnorm(x, norm_ffn), w_up, conv_w, conv_b, w_down, conv_prev)
    return x + f, nk, nv, nconv, vn


def setup_inputs(seed: int = 0) -> dict:
    key = jax.random.key(seed)
    ks = jax.random.split(key, 24)
    f32 = jnp.float32

    def nrm(k, shape, scale):
        return jax.random.normal(k, shape, f32) * scale

    win_buf = min(WINDOW, PAST_LEN)
    return {
        'x_prompt': nrm(ks[0], (BATCH, SEQ, D_MODEL), 1.0),
        'x_sample': nrm(ks[1], (DEC_BATCH, DEC_SEQ, D_MODEL), 1.0),
        'cache_win_k': nrm(ks[2], (DEPTH, DEC_BATCH, win_buf, N_KV_HEADS, HEAD_DIM), 1.0),
        'cache_win_v': nrm(ks[3], (DEPTH, DEC_BATCH, win_buf, N_KV_HEADS, HEAD_DIM), 1.0),
        'state_conv': nrm(ks[4], (DEPTH, DEC_BATCH, CONV_W - 1, 2 * D_FF), 1.0),
        'norm_mix': 1.0 + nrm(ks[5], (DEPTH, D_MODEL), 0.02),
        'w_in': nrm(ks[6], (DEPTH, D_MODEL, IN_COLS), D_MODEL ** -0.5),
        'sinks': nrm(ks[7], (DEPTH, N_HEADS), 1.0),
        'gmlp_ln_g': 1.0 + nrm(ks[8], (DEPTH, GMLP_W), 0.02),
        'gmlp_ln_b': nrm(ks[9], (DEPTH, GMLP_W), 0.02),
        'gmlp_ws': nrm(ks[10], (DEPTH, GMLP_GROUPS, CHUNK, CHUNK), CHUNK ** -0.5),
        'gmlp_bs': 1.0 + nrm(ks[11], (DEPTH, GMLP_GROUPS, CHUNK), 0.05),
        'w_branch_attn': nrm(ks[12], (DEPTH, ATTN_W, D_MODEL), ATTN_W ** -0.5),
        'w_branch_gmlp': nrm(ks[13], (DEPTH, GMLP_W, D_MODEL), GMLP_W ** -0.5),
        'w_out': nrm(ks[14], (DEPTH, D_MODEL, D_MODEL), D_MODEL ** -0.5),
        'norm_ffn': 1.0 + nrm(ks[15], (DEPTH, D_MODEL), 0.02),
        'w_up': nrm(ks[16], (DEPTH, D_MODEL, 2 * D_FF), D_MODEL ** -0.5),
        'conv_w': nrm(ks[17], (DEPTH, CONV_W, 2 * D_FF), CONV_W ** -0.5),
        'conv_b': nrm(ks[18], (DEPTH, 2 * D_FF), 0.02),
        'w_down': nrm(ks[19], (DEPTH, D_FF, D_MODEL), D_FF ** -0.5),
        'norm_final': 1.0 + nrm(ks[20], (D_MODEL,), 0.02),
    }


def reference(x_prompt, x_sample, cache_win_k, cache_win_v, state_conv,
              norm_mix, w_in, sinks, gmlp_ln_g, gmlp_ln_b, gmlp_ws, gmlp_bs,
              w_branch_attn, w_branch_gmlp, w_out, norm_ffn, w_up, conv_w, conv_b,
              w_down, norm_final):
    xp, xs = x_prompt, x_sample
    conv0 = jnp.zeros((x_prompt.shape[0], CONV_W - 1, 2 * D_FF), x_prompt.dtype)
    kp_l, vp_l, cp_l, ks_l, vs_l, cs_l, gv_l = [], [], [], [], [], [], []
    for l in range(DEPTH):
        params = (norm_mix[l], w_in[l], sinks[l], gmlp_ln_g[l], gmlp_ln_b[l], gmlp_ws[l],
                  gmlp_bs[l], w_branch_attn[l], w_branch_gmlp[l], w_out[l], norm_ffn[l],
                  w_up[l], conv_w[l], conv_b[l], w_down[l])
        xp, kp, vp, cp, _ = layer(xp, *params, None, None, conv0)
        xs, kss, vss, css, gvs = layer(xs, *params, cache_win_k[l], cache_win_v[l], state_conv[l])
        kp_l.append(kp); vp_l.append(vp); cp_l.append(cp)
        ks_l.append(kss); vs_l.append(vss); cs_l.append(css); gv_l.append(gvs)
    y_prompt = rmsnorm(xp, norm_final)
    y_sample = rmsnorm(xs, norm_final)
    return (y_prompt, y_sample, jnp.stack(kp_l), jnp.stack(vp_l), jnp.stack(cp_l),
            jnp.stack(ks_l), jnp.stack(vs_l), jnp.stack(cs_l), jnp.stack(gv_l))
```

```python
import functools

import jax
import jax.numpy as jnp
import numpy as np
from jax import lax
from jax.experimental import pallas as pl
from jax.experimental.pallas import tpu as pltpu

D_MODEL = 1024
DEPTH = 4
HEAD_DIM = 64
N_HEADS = 16
N_KV_HEADS = 4
GQA_GROUP = N_HEADS // N_KV_HEADS
ATTN_W = N_HEADS * HEAD_DIM
KV_W = N_KV_HEADS * HEAD_DIM
WINDOW = 128
BLOCK = 128
CHUNK = 128
GMLP_CH = 128
GMLP_GROUPS = 6
GMLP_W = GMLP_GROUPS * GMLP_CH
D_FF = 2816
CONV_W = 3
EPS = 1e-5
NEG = -1e30

Q0, K0, V0, GU0, GV0, GA0, GB0, IN_COLS = 0, 1024, 1280, 1536, 2304, 3072, 4096, 5120

V7X_VMEM_BYTES = 64 * 1024 * 1024
VMEM_LIMIT_BYTES = V7X_VMEM_BYTES - 8 * 1024 * 1024

TM_PROMPT = 256
SEQ_PER_STEP_MIXER = 16
SEQ_PER_STEP_FFN = 32

F32 = jnp.float32
BF16 = jnp.bfloat16


def _rmsnorm(x, g):
    return x * lax.rsqrt(jnp.mean(x * x, axis=-1, keepdims=True) + EPS) * g


def _layernorm(x, g, b):
    mu = jnp.mean(x, axis=-1, keepdims=True)
    xc = x - mu
    var = jnp.mean(xc * xc, axis=-1, keepdims=True)
    return xc * lax.rsqrt(var + EPS) * g + b


def _gelu(x):
    c = np.sqrt(2.0 / np.pi).astype(np.float32)
    return x * (0.5 * (1.0 + jnp.tanh(c * (x + 0.044715 * (x * x * x)))))


def _sigmoid(x):
    return 1.0 / (1.0 + jnp.exp(-x))


def _dot(a, b):
    return jnp.dot(a, b, preferred_element_type=F32)


def _dot_nt(a, b):
    return lax.dot_general(a, b, (((1,), (1,)), ((), ())), preferred_element_type=F32)


def _kv_lane_masks(rows):
    lane = lax.broadcasted_iota(jnp.int32, (rows, KV_W), 1)
    return [(lane >= h * HEAD_DIM) & (lane < (h + 1) * HEAD_DIM) for h in range(N_KV_HEADS)]


def _select_kv_lanes(masks, parts):
    out = parts[N_KV_HEADS - 1]
    for h in range(N_KV_HEADS - 2, -1, -1):
        out = jnp.where(masks[h], parts[h], out)
    return out


def _softmax_pv(s, sink, vv):
    m = jnp.maximum(jnp.max(s, axis=-1, keepdims=True), sink)
    p = jnp.exp(s - m)
    denom = jnp.sum(p, axis=-1, keepdims=True) + jnp.exp(sink - m)
    return _dot(p.astype(BF16), vv) / denom


def _gmlp(gu, gv, lng, lnb, ws_ref, bsb_ref, period):
    m_rows = gu.shape[0]
    u = _gelu(gu)
    vn = _layernorm(_gelu(gv), lng, lnb)
    vnb = vn.astype(BF16)
    row = lax.broadcasted_iota(jnp.int32, (CHUNK, CHUNK), 0)
    col = lax.broadcasted_iota(jnp.int32, (CHUNK, CHUNK), 1)
    keep = (col <= row) & (col >= row - (row & (period - 1)))
    w = [jnp.where(keep, ws_ref[g], 0.0).astype(BF16) for g in range(GMLP_GROUPS)]
    rows = []
    for c in range(m_rows // CHUNK):
        cols = []
        for g in range(GMLP_GROUPS):
            blk = vnb[c * CHUNK:(c + 1) * CHUNK, g * GMLP_CH:(g + 1) * GMLP_CH]
            cols.append(_dot(w[g], blk) + bsb_ref[g])
        rows.append(jnp.concatenate(cols, axis=1))
    mix = rows[0] if len(rows) == 1 else jnp.concatenate(rows, axis=0)
    return u * mix, vn


def _merge(x, o, s, ga, gb, wpa_ref, wpb_ref, wout_ref):
    merged = _sigmoid(ga) * _dot(o.astype(BF16), wpa_ref[...]) + _sigmoid(gb) * _dot(s.astype(BF16), wpb_ref[...])
    return x + _dot(merged.astype(BF16), wout_ref[...])


def _mixer_prompt_kernel(x_ref, nm_ref, win_ref, sink_ref, lng_ref, lnb_ref, ws_ref, bsb_ref,
                         wpa_ref, wpb_ref, wout_ref,
                         xo_ref, kt_ref, vt_ref, kk_ref, vv_ref):
    i = pl.program_id(1)
    tm = x_ref.shape[1]
    x = x_ref[0]
    hb = _rmsnorm(x, nm_ref[...]).astype(BF16)

    q = (_dot(hb, win_ref[:, Q0:K0]) * (HEAD_DIM ** -0.5))
    k = _dot(hb, win_ref[:, K0:V0])
    v = _dot(hb, win_ref[:, V0:GU0])

    @pl.when(i == 0)
    def _():
        kk_ref[0:BLOCK, :] = jnp.zeros((BLOCK, KV_W), BF16)
        vv_ref[0:BLOCK, :] = jnp.zeros((BLOCK, KV_W), BF16)

    kk_ref[BLOCK:BLOCK + tm, :] = k.astype(BF16)
    vv_ref[BLOCK:BLOCK + tm, :] = v.astype(BF16)

    rows = N_KV_HEADS * BLOCK
    t = lax.broadcasted_iota(jnp.int32, (rows, 2 * BLOCK), 0) & (BLOCK - 1)
    c = lax.broadcasted_iota(jnp.int32, (rows, 2 * BLOCK), 1)
    band = (c > t) & (c <= t + WINDOW)
    bias = jnp.where(band, 0.0, NEG).astype(F32)
    first_key = jnp.where(i > 0, 0, BLOCK)
    bias_first = jnp.where(band & (c >= first_key), 0.0, NEG).astype(F32)
    head_of_row = lax.broadcasted_iota(jnp.int32, (rows, 1), 0) >> (BLOCK.bit_length() - 1)
    masks = _kv_lane_masks(BLOCK)

    o_blocks = []
    for j in range(tm // BLOCK):
        r0 = j * BLOCK
        kk = kk_ref[r0:r0 + 2 * BLOCK, :]
        vv = vv_ref[r0:r0 + 2 * BLOCK, :]
        o_groups = []
        for g in range(GQA_GROUP):
            qg = q[r0:r0 + BLOCK, g * KV_W:(g + 1) * KV_W]
            lhs = jnp.concatenate([jnp.where(masks[h], qg, 0.0) for h in range(N_KV_HEADS)], axis=0).astype(BF16)
            s = _dot_nt(lhs, kk) + (bias_first if j == 0 else bias)
            sink = jnp.full((rows, 1), sink_ref[(N_KV_HEADS - 1) * GQA_GROUP + g], F32)
            for h in range(N_KV_HEADS - 2, -1, -1):
                sink = jnp.where(head_of_row == h, sink_ref[h * GQA_GROUP + g], sink)
            r = _softmax_pv(s, sink, vv)
            o_groups.append(_select_kv_lanes(masks, [r[h * BLOCK:(h + 1) * BLOCK] for h in range(N_KV_HEADS)]))
        o_blocks.append(jnp.concatenate(o_groups, axis=1))
    o = o_blocks[0] if len(o_blocks) == 1 else jnp.concatenate(o_blocks, axis=0)

    kk_ref[0:BLOCK, :] = kk_ref[tm:tm + BLOCK, :]
    vv_ref[0:BLOCK, :] = vv_ref[tm:tm + BLOCK, :]
    kt_ref[0] = k[tm - WINDOW:tm]
    vt_ref[0] = v[tm - WINDOW:tm]

    s_gate, _ = _gmlp(_dot(hb, win_ref[:, GU0:GV0]), _dot(hb, win_ref[:, GV0:GA0]),
                      lng_ref[...], lnb_ref[...], ws_ref, bsb_ref, CHUNK)
    xo_ref[0] = _merge(x, o, s_gate, _dot(hb, win_ref[:, GA0:GB0]), _dot(hb, win_ref[:, GB0:IN_COLS]),
                       wpa_ref, wpb_ref, wout_ref)


def _mixer_sample_kernel(x_ref, nm_ref, win_ref, sink_ref, lng_ref, lnb_ref, ws_ref, bsb_ref,
                         wpa_ref, wpb_ref, wout_ref, ck_ref, cv_ref,
                         xo_ref, wk_ref, wv_ref, vn_ref,
                         q_scr, k_scr, v_scr, o_scr, kk_scr, vv_scr):
    n_seq, past, _ = ck_ref.shape
    t_new = x_ref.shape[0] // n_seq
    x = x_ref[...]
    hb = _rmsnorm(x, nm_ref[...]).astype(BF16)
    q_scr[...] = _dot(hb, win_ref[:, Q0:K0]) * (HEAD_DIM ** -0.5)
    k_scr[...] = _dot(hb, win_ref[:, K0:V0])
    v_scr[...] = _dot(hb, win_ref[:, V0:GU0])

    n_keys = 2 * BLOCK
    kk_scr[past + t_new:n_keys, :] = jnp.zeros((n_keys - past - t_new, KV_W), F32)
    vv_scr[past + t_new:n_keys, :] = jnp.zeros((n_keys - past - t_new, KV_W), F32)

    rows = N_HEADS * t_new
    t = lax.broadcasted_iota(jnp.int32, (rows, n_keys), 0) & (t_new - 1)
    c = lax.broadcasted_iota(jnp.int32, (rows, n_keys), 1)
    diff = t + past - c
    bias = jnp.where((diff >= 0) & (diff < WINDOW) & (c < past + t_new), 0.0, NEG).astype(F32)
    head_slot = lax.broadcasted_iota(jnp.int32, (rows, 1), 0) >> (t_new.bit_length() - 1)
    sink = jnp.zeros((rows, 1), F32)
    for g in range(GQA_GROUP):
        for h in range(N_KV_HEADS):
            sink = jnp.where(head_slot == g * N_KV_HEADS + h, sink_ref[h * GQA_GROUP + g], sink)
    masks = _kv_lane_masks(t_new)

    def body(b, carry):
        r0 = pl.multiple_of(b * t_new, t_new)
        qb = q_scr[pl.ds(r0, t_new), :]
        k_new = k_scr[pl.ds(r0, t_new), :]
        v_new = v_scr[pl.ds(r0, t_new), :]
        k_old = ck_ref[b]
        v_old = cv_ref[b]
        kk_scr[0:past, :] = k_old
        kk_scr[past:past + t_new, :] = k_new
        vv_scr[0:past, :] = v_old
        vv_scr[past:past + t_new, :] = v_new
        wk_ref[b, 0:past - t_new, :] = k_old[t_new:past]
        wk_ref[b, past - t_new:past, :] = k_new
        wv_ref[b, 0:past - t_new, :] = v_old[t_new:past]
        wv_ref[b, past - t_new:past, :] = v_new

        lhs = jnp.concatenate(
            [jnp.where(masks[h], qb[:, g * KV_W:(g + 1) * KV_W], 0.0)
             for g in range(GQA_GROUP) for h in range(N_KV_HEADS)], axis=0).astype(BF16)
        s = _dot_nt(lhs, kk_scr[...].astype(BF16)) + bias
        r = _softmax_pv(s, sink, vv_scr[...].astype(BF16))
        o_groups = []
        for g in range(GQA_GROUP):
            base = g * N_KV_HEADS * t_new
            o_groups.append(_select_kv_lanes(
                masks, [r[base + h * t_new:base + (h + 1) * t_new] for h in range(N_KV_HEADS)]))
        o_scr[pl.ds(r0, t_new), :] = jnp.concatenate(o_groups, axis=1)
        return carry

    lax.fori_loop(0, n_seq, body, 0)

    s_gate, vn = _gmlp(_dot(hb, win_ref[:, GU0:GV0]), _dot(hb, win_ref[:, GV0:GA0]),
                       lng_ref[...], lnb_ref[...], ws_ref, bsb_ref, t_new)
    vn_ref[...] = vn
    xo_ref[...] = _merge(x, o_scr[...], s_gate, _dot(hb, win_ref[:, GA0:GB0]), _dot(hb, win_ref[:, GB0:IN_COLS]),
                         wpa_ref, wpb_ref, wout_ref)


def _conv_gate_down(x, z, z1, z2, cw_ref, cb_ref, wdn_ref):
    conv = cb_ref[...] + ((z2 * cw_ref[0:1, :] + z1 * cw_ref[1:2, :]) + z * cw_ref[2:3, :])
    gated = _gelu(conv[:, :D_FF]) * conv[:, D_FF:]
    return x + _dot(gated.astype(BF16), wdn_ref[...])


def _ffn_prompt_kernel(x_ref, nf_ref, wup_ref, cw_ref, cb_ref, wdn_ref, nfin_ref,
                       xo_ref, ct_ref, zbuf_ref, *, final):
    i = pl.program_id(1)
    tm = x_ref.shape[1]
    pad = 8
    x = x_ref[0]
    hb = _rmsnorm(x, nf_ref[...]).astype(BF16)
    z = _dot(hb, wup_ref[...])

    @pl.when(i == 0)
    def _():
        zbuf_ref[0:pad, :] = jnp.zeros((pad, 2 * D_FF), F32)

    zbuf_ref[pad:pad + tm, :] = z
    z1 = zbuf_ref[pad - 1:pad - 1 + tm, :]
    z2 = zbuf_ref[pad - 2:pad - 2 + tm, :]
    y = _conv_gate_down(x, z, z1, z2, cw_ref, cb_ref, wdn_ref)
    if final:
        y = _rmsnorm(y, nfin_ref[...])
    xo_ref[0] = y
    tail = zbuf_ref[tm:tm + pad, :]
    zbuf_ref[0:pad, :] = tail
    ct_ref[0] = tail[pad - (CONV_W - 1):pad]


def _ffn_sample_kernel(x_ref, nf_ref, wup_ref, cw_ref, cb_ref, wdn_ref, nfin_ref, st_ref,
                       xo_ref, ct_ref, *, final):
    n_seq = st_ref.shape[0]
    m_rows = x_ref.shape[0]
    t_new = m_rows // n_seq
    width = 2 * D_FF
    x = x_ref[...]
    hb = _rmsnorm(x, nf_ref[...]).astype(BF16)
    z = _dot(hb, wup_ref[...])
    z3 = z.reshape(n_seq, t_new, width)
    st = st_ref[...]
    p0 = jnp.broadcast_to(st[:, 0:1, :], (n_seq, t_new, width))
    p1 = jnp.broadcast_to(st[:, 1:2, :], (n_seq, t_new, width))
    t = lax.broadcasted_iota(jnp.int32, (n_seq, t_new, width), 1)
    r1 = pltpu.roll(z3, 1, 1)
    r2 = pltpu.roll(z3, 2, 1)
    z1 = jnp.where(t == 0, p1, r1).reshape(m_rows, width)
    z2 = jnp.where(t == 0, p0, jnp.where(t == 1, p1, r2)).reshape(m_rows, width)
    y = _conv_gate_down(x, z, z1, z2, cw_ref, cb_ref, wdn_ref)
    if final:
        y = _rmsnorm(y, nfin_ref[...])
    xo_ref[...] = y
    ct_ref[...] = r2[:, 0:CONV_W - 1, :]


def _const_spec(shape):
    nd = len(shape)
    return pl.BlockSpec(shape, lambda *_: (0,) * nd, pipeline_mode=pl.Buffered(1))


_SMEM_SPEC = pl.BlockSpec(memory_space=pltpu.SMEM)


def _compiler_params(n_axes):
    return pltpu.CompilerParams(dimension_semantics=("arbitrary",) * n_axes,
                                vmem_limit_bytes=VMEM_LIMIT_BYTES)


def _mixer_weight_specs():
    return [
        _const_spec((1, D_MODEL)),
        _const_spec((D_MODEL, IN_COLS)),
        _SMEM_SPEC,
        _const_spec((1, GMLP_W)),
        _const_spec((1, GMLP_W)),
        _const_spec((GMLP_GROUPS, CHUNK, CHUNK)),
        _const_spec((GMLP_GROUPS, CHUNK, GMLP_CH)),
        _const_spec((ATTN_W, D_MODEL)),
        _const_spec((GMLP_W, D_MODEL)),
        _const_spec((D_MODEL, D_MODEL)),
    ]


def _ffn_weight_specs():
    return [
        _const_spec((1, D_MODEL)),
        _const_spec((D_MODEL, 2 * D_FF)),
        _const_spec((CONV_W, 2 * D_FF)),
        _const_spec((1, 2 * D_FF)),
        _const_spec((D_FF, D_MODEL)),
        _const_spec((1, D_MODEL)),
    ]


def _mixer_prompt(x, mixer_w):
    batch, seq, _ = x.shape
    tm = TM_PROMPT
    tile = pl.BlockSpec((1, tm, D_MODEL), lambda b, i: (b, i, 0))
    tail = pl.BlockSpec((1, WINDOW, KV_W), lambda b, i: (b, 0, 0))
    return pl.pallas_call(
        _mixer_prompt_kernel,
        grid=(batch, seq // tm),
        in_specs=[tile] + _mixer_weight_specs(),
        out_specs=[tile, tail, tail],
        out_shape=[jax.ShapeDtypeStruct(x.shape, F32),
                   jax.ShapeDtypeStruct((batch, WINDOW, KV_W), F32),
                   jax.ShapeDtypeStruct((batch, WINDOW, KV_W), F32)],
        scratch_shapes=[pltpu.VMEM((BLOCK + tm, KV_W), BF16), pltpu.VMEM((BLOCK + tm, KV_W), BF16)],
        compiler_params=_compiler_params(2),
        name="mixer_prompt",
    )(x, *mixer_w)


def _mixer_sample(x, mixer_w, cache_k, cache_v):
    n_rows = x.shape[0]
    n_seq_total, past, _ = cache_k.shape
    t_new = n_rows // n_seq_total
    ns = SEQ_PER_STEP_MIXER
    m = ns * t_new
    tile = pl.BlockSpec((m, D_MODEL), lambda i: (i, 0))
    win =pl.BlockSpec((ns, past, KV_W), lambda i: (i, 0, 0))
    return pl.pallas_call(
        _mixer_sample_kernel,
        grid=(n_seq_total // ns,),
        in_specs=[tile] + _mixer_weight_specs() + [win, win],
        out_specs=[tile, win, win, pl.BlockSpec((m, GMLP_W), lambda i: (i, 0))],
        out_shape=[jax.ShapeDtypeStruct(x.shape, F32),
                   jax.ShapeDtypeStruct(cache_k.shape, F32),
                   jax.ShapeDtypeStruct(cache_v.shape, F32),
                   jax.ShapeDtypeStruct((n_rows, GMLP_W), F32)],
        scratch_shapes=[pltpu.VMEM((m, ATTN_W), F32), pltpu.VMEM((m, KV_W), F32), pltpu.VMEM((m, KV_W), F32),
                        pltpu.VMEM((m, ATTN_W), F32),
                        pltpu.VMEM((2 * BLOCK, KV_W), F32), pltpu.VMEM((2 * BLOCK, KV_W), F32)],
        compiler_params=_compiler_params(1),
        name="mixer_sample",
    )(x, *mixer_w, cache_k, cache_v)


def _ffn_prompt(x, ffn_w, final):
    batch, seq, _ = x.shape
    tm = TM_PROMPT
    tile = pl.BlockSpec((1, tm, D_MODEL), lambda b, i: (b, i, 0))
    return pl.pallas_call(
        functools.partial(_ffn_prompt_kernel, final=final),
        grid=(batch, seq // tm),
        in_specs=[tile] + _ffn_weight_specs(),
        out_specs=[tile, pl.BlockSpec((1, CONV_W - 1, 2 * D_FF), lambda b, i: (b, 0, 0))],
        out_shape=[jax.ShapeDtypeStruct(x.shape, F32),
                   jax.ShapeDtypeStruct((batch, CONV_W - 1, 2 * D_FF), F32)],
        scratch_shapes=[pltpu.VMEM((tm + 8, 2 * D_FF), F32)],
        compiler_params=_compiler_params(2),
        name="ffn_prompt",
    )(x, *ffn_w)


def _ffn_sample(x, ffn_w, state, final):
    n_rows = x.shape[0]
    n_seq_total = state.shape[0]
    t_new = n_rows // n_seq_total
    ns = SEQ_PER_STEP_FFN
    m = ns * t_new
    tile = pl.BlockSpec((m, D_MODEL), lambda i: (i, 0))
    st =pl.BlockSpec((ns, CONV_W - 1, 2 * D_FF), lambda i: (i, 0, 0))
    return pl.pallas_call(
        functools.partial(_ffn_sample_kernel, final=final),
        grid=(n_seq_total // ns,),
        in_specs=[tile] + _ffn_weight_specs() + [st],
        out_specs=[tile, st],
        out_shape=[jax.ShapeDtypeStruct(x.shape, F32), jax.ShapeDtypeStruct(state.shape, F32)],
        compiler_params=_compiler_params(1),
        name="ffn_sample",
    )(x, *ffn_w, state)


def _heads_group_major(w, axis):
    shape = w.shape
    split = shape[:axis] + (N_KV_HEADS, GQA_GROUP, HEAD_DIM) + shape[axis + 1:]
    return jnp.swapaxes(w.reshape(split), axis, axis + 1).reshape(shape)


def kernel(x_prompt, x_sample, cache_win_k, cache_win_v, state_conv, norm_mix, w_in, sinks, gmlp_ln_g, gmlp_ln_b, gmlp_ws, gmlp_bs, w_branch_attn, w_branch_gmlp, w_out, norm_ffn, w_up, conv_w, conv_b, w_down, norm_final):
    dec_batch, dec_seq, _ = x_sample.shape
    w_in_b = jnp.concatenate([_heads_group_major(w_in[:, :, :ATTN_W], 2), w_in[:, :, ATTN_W:]], axis=2).astype(BF16)
    w_pa_b = _heads_group_major(w_branch_attn, 1).astype(BF16)
    w_pb_b = w_branch_gmlp.astype(BF16)
    w_out_b = w_out.astype(BF16)
    w_up_b = w_up.astype(BF16)
    w_down_b = w_down.astype(BF16)
    bs_prompt = jnp.broadcast_to(gmlp_bs[:, :, :, None], (DEPTH, GMLP_GROUPS, CHUNK, GMLP_CH))
    reps = CHUNK // dec_seq
    ws_sample = jnp.tile(gmlp_ws[:, :, :dec_seq, :dec_seq], (1, 1, reps, reps))
    bs_sample = jnp.broadcast_to(jnp.tile(gmlp_bs[:, :, :dec_seq], (1, 1, reps))[:, :, :, None],
                                 (DEPTH, GMLP_GROUPS, CHUNK, GMLP_CH))
    cache_k = cache_win_k.reshape(DEPTH, dec_batch, WINDOW, KV_W)
    cache_v = cache_win_v.reshape(DEPTH, dec_batch, WINDOW, KV_W)
    nfin = norm_final.reshape(1, D_MODEL)

    xp = x_prompt
    xs = x_sample.reshape(dec_batch * dec_seq, D_MODEL)
    kp_l, vp_l, cp_l, ks_l, vs_l, cs_l, gv_l = [], [], [], [], [], [], []
    for l in range(DEPTH):
        final = l == DEPTH - 1
        shared = (norm_mix[l].reshape(1, D_MODEL), w_in_b[l], sinks[l],
                  gmlp_ln_g[l].reshape(1, GMLP_W), gmlp_ln_b[l].reshape(1, GMLP_W))
        tail = (w_pa_b[l], w_pb_b[l], w_out_b[l])
        ffn_w = (norm_ffn[l].reshape(1, D_MODEL), w_up_b[l], conv_w[l], conv_b[l].reshape(1, 2 * D_FF),
                 w_down_b[l], nfin)
        xp, kp, vp = _mixer_prompt(xp, shared + (gmlp_ws[l], bs_prompt[l]) + tail)
        xp, cp = _ffn_prompt(xp, ffn_w, final)
        xs, kss, vss, gvs = _mixer_sample(xs, shared + (ws_sample[l], bs_sample[l]) + tail, cache_k[l], cache_v[l])
        xs, css = _ffn_sample(xs, ffn_w, state_conv[l], final)
        kp_l.append(kp); vp_l.append(vp); cp_l.append(cp)
        ks_l.append(kss); vs_l.append(vss); cs_l.append(css); gv_l.append(gvs)

    batch = x_prompt.shape[0]
    kv_prompt = (DEPTH, batch, WINDOW, N_KV_HEADS, HEAD_DIM)
    kv_sample = (DEPTH, dec_batch, WINDOW, N_KV_HEADS, HEAD_DIM)
    return (xp, xs.reshape(x_sample.shape),
            jnp.stack(kp_l).reshape(kv_prompt), jnp.stack(vp_l).reshape(kv_prompt), jnp.stack(cp_l),
            jnp.stack(ks_l).reshape(kv_sample), jnp.stack(vs_l).reshape(kv_sample), jnp.stack(cs_l),
            jnp.stack(gv_l).reshape(DEPTH, dec_batch, dec_seq, GMLP_W))
```

```python
import functools

import jax
import jax.numpy as jnp
import numpy as np
from jax import lax
from jax.experimental import pallas as pl
from jax.experimental.pallas import tpu as pltpu

D_MODEL = 1024
DEPTH = 4
HEAD_DIM = 64
N_HEADS = 16
N_KV_HEADS = 4
GQA_GROUP = N_HEADS // N_KV_HEADS
ATTN_W = N_HEADS * HEAD_DIM
KV_W = N_KV_HEADS * HEAD_DIM
WINDOW = 128
BLOCK = 128
CHUNK = 128
GMLP_CH = 128
GMLP_GROUPS = 6
GMLP_W = GMLP_GROUPS * GMLP_CH
D_FF = 2816
CONV_W = 3
EPS = 1e-5
NEG = -1e30

Q0, K0, V0, GU0, GV0, GA0, GB0, IN_COLS = 0, 1024, 1280, 1536, 2304, 3072, 4096, 5120

V7X_VMEM_BYTES = 64 * 1024 * 1024
VMEM_LIMIT_BYTES = V7X_VMEM_BYTES - 8 * 1024 * 1024
F32_SUBLANES = 8

TM_PROMPT = 256
SEQ_PER_STEP_MIXER = 16
SEQ_PER_STEP_FFN = 32
FFN_CHUNK = 256
PROJ_TILE = 256

F32 = jnp.float32
BF16 = jnp.bfloat16


def _rmsnorm(x, g):
    return x * lax.rsqrt(jnp.mean(x * x, axis=-1, keepdims=True) + EPS) * g


def _layernorm(x, g, b):
    mu = jnp.mean(x, axis=-1, keepdims=True)
    xc = x - mu
    var = jnp.mean(xc * xc, axis=-1, keepdims=True)
    return xc * lax.rsqrt(var + EPS) * g + b


def _gelu(x):
    c = np.sqrt(2.0 / np.pi).astype(np.float32)
    return x * (0.5 * (1.0 + jnp.tanh(c * (x + 0.044715 * (x * x * x)))))


def _sigmoid(x):
    return 1.0 / (1.0 + jnp.exp(-x))


def _dot(a, b):
    return jnp.dot(a, b, preferred_element_type=F32)


def _dot_nt(a, b):
    return lax.dot_general(a, b, (((1,), (1,)), ((), ())), preferred_element_type=F32)


def _kv_lane_masks(rows):
    lane = lax.broadcasted_iota(jnp.int32, (rows, KV_W), 1)
    return [(lane >= h * HEAD_DIM) & (lane < (h + 1) * HEAD_DIM) for h in range(N_KV_HEADS)]


def _select_kv_lanes(masks, parts):
    out = parts[N_KV_HEADS - 1]
    for h in range(N_KV_HEADS - 2, -1, -1):
        out = jnp.where(masks[h], parts[h], out)
    return out


def _softmax_pv(s, sink, vv):
    m = jnp.maximum(jnp.max(s, axis=-1, keepdims=True), sink)
    p = jnp.exp(s - m)
    denom = jnp.sum(p, axis=-1, keepdims=True) + jnp.exp(sink - m)
    return _dot(p.astype(BF16), vv) / denom


def _gmlp_mix(vn, ws_ref, bs_ref, period):
    vnb = vn.astype(BF16)
    row = lax.broadcasted_iota(jnp.int32, (CHUNK, CHUNK), 0)
    col = lax.broadcasted_iota(jnp.int32, (CHUNK, CHUNK), 1)
    keep = (col <= row) & (col >= row - (row & (period - 1)))
    w = [jnp.where(keep, ws_ref[g], 0.0).astype(BF16) for g in range(GMLP_GROUPS)]
    rows = []
    for c in range(vn.shape[0] // CHUNK):
        cols = []
        for g in range(GMLP_GROUPS):
            blk = vnb[c * CHUNK:(c + 1) * CHUNK, g * GMLP_CH:(g + 1) * GMLP_CH]
            cols.append(_dot(w[g], blk) + bs_ref[g])
        rows.append(jnp.concatenate(cols, axis=1))
    return rows[0] if len(rows) == 1 else jnp.concatenate(rows, axis=0)


class _GateProjection:
    def __init__(self, hb, win_ref):
        self.hb, self.win_ref, self.tiles = hb, win_ref, []

    def issue(self, n_tiles):
        for _ in range(n_tiles):
            c0 = GU0 + len(self.tiles) * PROJ_TILE
            if c0 < IN_COLS:
                self.tiles.append(_dot(self.hb, self.win_ref[:, c0:c0 + PROJ_TILE]))

    def columns(self, c0, c1):
        self.issue((IN_COLS - GU0) // PROJ_TILE)
        return jnp.concatenate(self.tiles[(c0 - GU0) // PROJ_TILE:(c1 - GU0) // PROJ_TILE], axis=1)


def _gate_merge_out(x, o, proj, lng_ref, lnb_ref, ws_ref, bs_ref, wpa_ref, wpb_ref, wout_ref, period):
    u = _gelu(proj.columns(GU0, GV0))
    vn = _layernorm(_gelu(proj.columns(GV0, GA0)), lng_ref[...], lnb_ref[...])
    ga = proj.columns(GA0, GB0)
    gb = proj.columns(GB0, IN_COLS)
    oa = _dot(o.astype(BF16), wpa_ref[...])
    s_gate = u * _gmlp_mix(vn, ws_ref, bs_ref, period)
    merged = _sigmoid(ga) * oa + _sigmoid(gb) * _dot(s_gate.astype(BF16), wpb_ref[...])
    return x + _dot(merged.astype(BF16), wout_ref[...]), vn


def _mixer_prompt_kernel(x_ref, xprev_ref, nm_ref, win_ref, sink_ref, lng_ref, lnb_ref, ws_ref, bs_ref,
                         wpa_ref, wpb_ref, wout_ref,
                         xo_ref, kt_ref, vt_ref):
    i = pl.program_id(1)
    tm = x_ref.shape[0]
    x = x_ref[...]
    hb = _rmsnorm(x, nm_ref[...]).astype(BF16)

    hb_prev = _rmsnorm(xprev_ref[...], nm_ref[...]).astype(BF16)
    kv_prev = _dot(hb_prev, win_ref[:, K0:GU0])
    k = _dot(hb, win_ref[:, K0:V0])
    v = _dot(hb, win_ref[:, V0:GU0])
    q = (_dot(hb, win_ref[:, Q0:K0]) * (HEAD_DIM ** -0.5))
    proj = _GateProjection(hb, win_ref)

    kk_all = jnp.concatenate([kv_prev[:, 0:KV_W], k], axis=0).astype(BF16)
    vv_all = jnp.concatenate([kv_prev[:, KV_W:2 * KV_W], v], axis=0).astype(BF16)
    kt_ref[...] = k[tm - WINDOW:tm]
    vt_ref[...] = v[tm - WINDOW:tm]

    rows = N_KV_HEADS * BLOCK
    t = lax.broadcasted_iota(jnp.int32, (rows, 2 * BLOCK), 0) & (BLOCK - 1)
    c = lax.broadcasted_iota(jnp.int32, (rows, 2 * BLOCK), 1)
    band = (c > t) & (c <= t + WINDOW)
    bias = jnp.where(band, 0.0, NEG).astype(F32)
    first_key = jnp.where(i > 0, 0, BLOCK)
    bias_first = jnp.where(band & (c >= first_key), 0.0, NEG).astype(F32)
    head_of_row = lax.broadcasted_iota(jnp.int32, (rows, 1), 0) >> (BLOCK.bit_length() - 1)
    masks = _kv_lane_masks(BLOCK)

    items = [(j, g) for j in range(tm // BLOCK) for g in range(GQA_GROUP)]

    def scores(item):
        j, g = item
        qg = q[j * BLOCK:(j + 1) * BLOCK, g * KV_W:(g + 1) * KV_W]
        lhs = jnp.concatenate([jnp.where(masks[h], qg, 0.0) for h in range(N_KV_HEADS)], axis=0).astype(BF16)
        return _dot_nt(lhs, kk_all[j * BLOCK:(j + 2) * BLOCK]) + (bias_first if j == 0 else bias)

    fill = -(-((IN_COLS - GU0) // PROJ_TILE) // len(items))
    o_groups = {}
    s_next = scores(items[0])
    for n, (j, g) in enumerate(items):
        s = s_next
        if n + 1 < len(items):
            s_next = scores(items[n + 1])
        proj.issue(fill)
        sink = jnp.full((rows, 1), sink_ref[(N_KV_HEADS - 1) * GQA_GROUP + g], F32)
        for h in range(N_KV_HEADS - 2, -1, -1):
            sink = jnp.where(head_of_row == h, sink_ref[h * GQA_GROUP + g], sink)
        r = _softmax_pv(s, sink, vv_all[j * BLOCK:(j + 2) * BLOCK])
        o_groups[(j, g)] = _select_kv_lanes(masks, [r[h * BLOCK:(h + 1) * BLOCK] for h in range(N_KV_HEADS)])
    o_blocks = [jnp.concatenate([o_groups[(j, g)] for g in range(GQA_GROUP)], axis=1) for j in range(tm // BLOCK)]
    o = o_blocks[0] if len(o_blocks) == 1 else jnp.concatenate(o_blocks, axis=0)

    xo_ref[...], _ = _gate_merge_out(x, o, proj, lng_ref, lnb_ref, ws_ref, bs_ref, wpa_ref, wpb_ref, wout_ref, CHUNK)


def _mixer_sample_kernel(x_ref, nm_ref, win_ref, sink_ref, lng_ref, lnb_ref, ws_ref, bs_ref,
                         wpa_ref, wpb_ref, wout_ref, ck_ref, cv_ref,
                         xo_ref, wk_ref, wv_ref, vn_ref):
    n_seq, past, _ = ck_ref.shape
    t_new = x_ref.shape[0] // n_seq
    x = x_ref[...]
    hb = _rmsnorm(x, nm_ref[...]).astype(BF16)
    q = _dot(hb, win_ref[:, Q0:K0]) * (HEAD_DIM ** -0.5)
    k = _dot(hb, win_ref[:, K0:V0])
    v = _dot(hb, win_ref[:, V0:GU0])
    proj = _GateProjection(hb, win_ref)

    n_keys = 2 * BLOCK
    rows = N_HEADS * t_new
    t = lax.broadcasted_iota(jnp.int32, (rows, n_keys), 0) & (t_new - 1)
    c = lax.broadcasted_iota(jnp.int32, (rows, n_keys), 1)
    diff = t + past - c
    bias = jnp.where((diff >= 0) & (diff < WINDOW) & (c < past + t_new), 0.0, NEG).astype(F32)
    head_slot = lax.broadcasted_iota(jnp.int32, (rows, 1), 0) >> (t_new.bit_length() - 1)
    sink = jnp.zeros((rows, 1), F32)
    for g in range(GQA_GROUP):
        for h in range(N_KV_HEADS):
            sink = jnp.where(head_slot == g * N_KV_HEADS + h, sink_ref[h * GQA_GROUP + g], sink)
    masks = _kv_lane_masks(t_new)
    zero_keys = jnp.zeros((n_keys - past - t_new, KV_W), F32)

    def keys_values(b):
        k_old = ck_ref[b]
        v_old = cv_ref[b]
        k_new = k[b * t_new:(b + 1) * t_new]
        v_new = v[b * t_new:(b + 1) * t_new]
        wk_ref[b, 0:past - t_new, :] = k_old[t_new:past]
        wk_ref[b, past - t_new:past, :] = k_new
        wv_ref[b, 0:past - t_new, :] = v_old[t_new:past]
        wv_ref[b, past - t_new:past, :] = v_new
        kk = jnp.concatenate([k_old, k_new, zero_keys], axis=0).astype(BF16)
        vv = jnp.concatenate([v_old, v_new, zero_keys], axis=0).astype(BF16)
        return kk, vv

    s_list, vv_list = [], []
    for b in range(n_seq):
        kk, vv = keys_values(b)
        qb = q[b * t_new:(b + 1) * t_new]
        lhs = jnp.concatenate(
            [jnp.where(masks[h], qb[:, g * KV_W:(g + 1) * KV_W], 0.0)
             for g in range(GQA_GROUP) for h in range(N_KV_HEADS)], axis=0).astype(BF16)
        s_list.append(_dot_nt(lhs, kk) + bias)
        vv_list.append(vv)
    proj.issue((IN_COLS - GU0) // PROJ_TILE)
    o_rows = []
    for b in range(n_seq):
        r = _softmax_pv(s_list[b], sink, vv_list[b])
        o_groups = []
        for g in range(GQA_GROUP):
            base = g * N_KV_HEADS * t_new
            o_groups.append(_select_kv_lanes(
                masks, [r[base + h * t_new:base + (h + 1) * t_new] for h in range(N_KV_HEADS)]))
        o_rows.append(jnp.concatenate(o_groups, axis=1))
    o = jnp.concatenate(o_rows, axis=0)

    xo_ref[...], vn_ref[...] = _gate_merge_out(x, o, proj, lng_ref, lnb_ref, ws_ref, bs_ref,
                                               wpa_ref, wpb_ref, wout_ref, t_new)


def _conv_gate(za, zb, sa, sb, cw_ref, cb_ref, a0, b0, width):
    def conv(z, shifted, c0):
        z1, z2 = shifted
        cols = slice(c0, c0 + width)
        return cb_ref[:, cols] + ((z2 * cw_ref[0:1, cols] + z1 * cw_ref[1:2, cols]) + z * cw_ref[2:3, cols])

    a = conv(za, sa, a0)
    b = conv(zb, sb, b0)
    c0 = np.sqrt(2.0 / np.pi).astype(np.float32)
    c1 = np.float32(c0 * np.float32(0.044715))
    th = jnp.tanh(a * (c1 * (a * a) + c0))
    half_ab = (0.5 * a) * b
    return half_ab + half_ab * th


def _ffn_body(x, nf_ref, wup_ref, cw_ref, cb_ref, wdn_ref, nfin_ref, shift_fn, final):
    hb = _rmsnorm(x, nf_ref[...]).astype(BF16)

    def up(c):
        a0 = c * FFN_CHUNK
        return _dot(hb, wup_ref[:, a0:a0 + FFN_CHUNK]), _dot(hb, wup_ref[:, D_FF + a0:D_FF + a0 + FFN_CHUNK])

    n_chunks = D_FF // FFN_CHUNK
    acc = x
    nxt = up(0)
    for c in range(n_chunks):
        za, zb = nxt
        if c + 1 < n_chunks:
            nxt = up(c + 1)
        a0 = c * FFN_CHUNK
        b0 = D_FF + a0
        gated = _conv_gate(za, zb, shift_fn(za, a0), shift_fn(zb, b0), cw_ref, cb_ref, a0, b0, FFN_CHUNK)
        acc = acc + _dot(gated.astype(BF16), wdn_ref[a0:a0 + FFN_CHUNK, :])
    if final:
        acc = _rmsnorm(acc, nfin_ref[...])
    return acc


def _ffn_prompt_kernel(x_ref, nf_ref, wup_ref, cw_ref, cb_ref, wdn_ref, nfin_ref,
                       xo_ref, ct_ref, carry_ref, *, final):
    tm = x_ref.shape[0]
    sub = carry_ref.shape[0]

    @pl.when(pl.program_id(1) == 0)
    def _():
        carry_ref[...] = jnp.zeros_like(carry_ref)

    def shift_fn(z, c0):
        cols = slice(c0, c0 + z.shape[1])
        ext = jnp.concatenate([carry_ref[:, cols], z], axis=0)
        last = z[tm - sub:tm]
        carry_ref[:, cols] = last
        ct_ref[:, cols] = pltpu.roll(last, CONV_W - 1, 0)[0:CONV_W - 1]
        return pltpu.roll(ext, 1, 0)[sub:], pltpu.roll(ext, 2, 0)[sub:]

    xo_ref[...] = _ffn_body(x_ref[...], nf_ref, wup_ref, cw_ref, cb_ref, wdn_ref, nfin_ref, shift_fn, final)


def _ffn_sample_kernel(x_ref, nf_ref, wup_ref, cw_ref, cb_ref, wdn_ref, nfin_ref, st_ref,
                       xo_ref, ct_ref, *, final):
    n_seq = st_ref.shape[0]
    m_rows = x_ref.shape[0]
    t_new = m_rows // n_seq

    def shift_fn(z, c0):
        width = z.shape[1]
        cols = slice(c0, c0 + width)
        z3 = z.reshape(n_seq, t_new, width)
        st = st_ref[:, :, cols]
        p0 = jnp.broadcast_to(st[:, 0:1, :], z3.shape)
        p1 = jnp.broadcast_to(st[:, 1:2, :], z3.shape)
        t = lax.broadcasted_iota(jnp.int32, z3.shape, 1)
        r1 = pltpu.roll(z3, 1, 1)
        r2 = pltpu.roll(z3, 2, 1)
        ct_ref[:, :, cols] = r2[:, 0:CONV_W - 1, :]
        z1 = jnp.where(t == 0, p1, r1)
        z2 = jnp.where(t == 0, p0, jnp.where(t == 1, p1, r2))
        return z1.reshape(m_rows, width), z2.reshape(m_rows, width)

    xo_ref[...] = _ffn_body(x_ref[...], nf_ref, wup_ref, cw_ref, cb_ref, wdn_ref, nfin_ref, shift_fn, final)


def _drop_aliased(kernel_fn, n_in, n_aliased):
    def wrapped(*refs):
        return kernel_fn(*refs[:n_in], *refs[n_in + n_aliased:])
    return wrapped


def _layer_spec(layer, shape):
    nd = len(shape)
    return pl.BlockSpec((None,) + tuple(shape), lambda *_: (layer,) + (0,) * nd, pipeline_mode=pl.Buffered(1))


_SMEM_SPEC = pl.BlockSpec(memory_space=pltpu.SMEM)
_ANY_SPEC = pl.BlockSpec(memory_space=pl.ANY)


def _compiler_params(n_axes):
    return pltpu.CompilerParams(dimension_semantics=("arbitrary",) * n_axes,
                                vmem_limit_bytes=VMEM_LIMIT_BYTES)


def _mixer_weight_specs(layer):
    return [
        _layer_spec(layer, (1, D_MODEL)),
        _layer_spec(layer, (D_MODEL, IN_COLS)),
        _SMEM_SPEC,
        _layer_spec(layer, (1, GMLP_W)),
        _layer_spec(layer, (1, GMLP_W)),
        _layer_spec(layer, (GMLP_GROUPS, CHUNK, CHUNK)),
        _layer_spec(layer, (GMLP_GROUPS, CHUNK, 1)),
        _layer_spec(layer, (ATTN_W, D_MODEL)),
        _layer_spec(layer, (GMLP_W, D_MODEL)),
        _layer_spec(layer, (D_MODEL, D_MODEL)),
    ]


def _ffn_weight_specs(layer):
    return [
        _layer_spec(layer, (1, D_MODEL)),
        _layer_spec(layer, (D_MODEL, 2 * D_FF)),
        _layer_spec(layer, (CONV_W, 2 * D_FF)),
        _layer_spec(layer, (1, 2 * D_FF)),
        _layer_spec(layer, (D_FF, D_MODEL)),
        pl.BlockSpec((1, D_MODEL), lambda *_: (0, 0), pipeline_mode=pl.Buffered(1)),
    ]


def _call_layer(kernel_fn, layer, grid, inputs, in_specs, out_specs, out_shapes, stacked, scratch, name):
    n_in = len(inputs)
    aliases = {}
    if stacked is not None:
        inputs = list(inputs) + list(stacked)
        in_specs = list(in_specs) + [_ANY_SPEC] * len(stacked)
        aliases = {n_in + n: 1 + n for n in range(len(stacked))}
        kernel_fn = _drop_aliased(kernel_fn, n_in, len(stacked))
    return pl.pallas_call(
        kernel_fn, grid=grid, in_specs=in_specs, out_specs=out_specs, out_shape=out_shapes,
        scratch_shapes=scratch, input_output_aliases=aliases,
        compiler_params=_compiler_params(len(grid)), name=name,
    )(*inputs)


def _mixer_prompt(layer, x, mixer_w, stacked):
    batch, seq, _ = x.shape
    tm = TM_PROMPT
    tile = pl.BlockSpec((None, tm, D_MODEL), lambda b, i: (b, i, 0))
    prev_block = pl.BlockSpec((None, BLOCK, D_MODEL), lambda b, i: (b, jnp.maximum(i * (tm // BLOCK) - 1, 0), 0))
    tail = pl.BlockSpec((None, None, WINDOW, KV_W), lambda b, i: (layer, b, 0, 0))
    tail_shape = jax.ShapeDtypeStruct((DEPTH, batch, WINDOW, KV_W), F32)
    return _call_layer(
        _mixer_prompt_kernel, layer, (batch, seq // tm), (x, x) + mixer_w,
        [tile, prev_block] + _mixer_weight_specs(layer), [tile, tail, tail],
        [jax.ShapeDtypeStruct(x.shape, F32), tail_shape, tail_shape], stacked, [], "mixer_prompt")


def _mixer_sample(layer, x, mixer_w, cache_k, cache_v, stacked):
    n_rows = x.shape[0]
    _, n_seq_total, past, _ = cache_k.shape
    t_new = n_rows // n_seq_total
    ns = SEQ_PER_STEP_MIXER
    m = ns * t_new
    tile = pl.BlockSpec((m, D_MODEL), lambda i: (i, 0))
    win = pl.BlockSpec((None, ns, past, KV_W), lambda i: (layer, i, 0, 0))
    vn_spec = pl.BlockSpec((None, m, GMLP_W), lambda i: (layer, i, 0))
    return _call_layer(
        _mixer_sample_kernel, layer, (n_seq_total // ns,), (x,) + mixer_w + (cache_k, cache_v),
        [tile] + _mixer_weight_specs(layer) + [win, win], [tile, win, win, vn_spec],
        [jax.ShapeDtypeStruct(x.shape, F32), jax.ShapeDtypeStruct(cache_k.shape, F32),
         jax.ShapeDtypeStruct(cache_v.shape, F32), jax.ShapeDtypeStruct((DEPTH, n_rows, GMLP_W), F32)],
        stacked, [], "mixer_sample")


def _ffn_prompt(layer, x, ffn_w, stacked):
    batch, seq, _ = x.shape
    tm = TM_PROMPT
    tile = pl.BlockSpec((None, tm, D_MODEL), lambda b, i: (b, i, 0))
    tail = pl.BlockSpec((None, None, CONV_W - 1, 2 * D_FF), lambda b, i: (layer, b, 0, 0))
    return _call_layer(
        functools.partial(_ffn_prompt_kernel, final=layer == DEPTH - 1), layer, (batch, seq // tm), (x,) + ffn_w,
        [tile] + _ffn_weight_specs(layer), [tile, tail],
        [jax.ShapeDtypeStruct(x.shape, F32), jax.ShapeDtypeStruct((DEPTH, batch, CONV_W - 1, 2 * D_FF), F32)],
        stacked, [pltpu.VMEM((F32_SUBLANES, 2 * D_FF), F32)], "ffn_prompt")


def _ffn_sample(layer, x, ffn_w, state, stacked):
    n_rows = x.shape[0]
    n_seq_total = state.shape[1]
    t_new = n_rows // n_seq_total
    ns = SEQ_PER_STEP_FFN
    m = ns * t_new
    tile = pl.BlockSpec((m, D_MODEL), lambda i: (i, 0))
    st = pl.BlockSpec((None, ns, CONV_W - 1, 2 * D_FF), lambda i: (layer, i, 0, 0))
    return _call_layer(
        functools.partial(_ffn_sample_kernel, final=layer == DEPTH - 1), layer, (n_seq_total // ns,),
        (x,) + ffn_w + (state,), [tile] + _ffn_weight_specs(layer) + [st], [tile, st],
        [jax.ShapeDtypeStruct(x.shape, F32), jax.ShapeDtypeStruct(state.shape, F32)],
        stacked, [], "ffn_sample")


def _heads_group_major(w, axis):
    shape = w.shape
    split = shape[:axis] + (N_KV_HEADS, GQA_GROUP, HEAD_DIM) + shape[axis + 1:]
    return jnp.swapaxes(w.reshape(split), axis, axis + 1).reshape(shape)


def kernel(x_prompt, x_sample, cache_win_k, cache_win_v, state_conv, norm_mix, w_in, sinks, gmlp_ln_g, gmlp_ln_b, gmlp_ws, gmlp_bs, w_branch_attn, w_branch_gmlp, w_out, norm_ffn, w_up, conv_w, conv_b, w_down, norm_final):
    batch = x_prompt.shape[0]
    dec_batch, dec_seq, _ = x_sample.shape
    w_in_b = jnp.concatenate([_heads_group_major(w_in[:, :, :ATTN_W], 2), w_in[:, :, ATTN_W:]], axis=2).astype(BF16)
    w_pa_b = _heads_group_major(w_branch_attn, 1).astype(BF16)
    reps = CHUNK // dec_seq
    ws_sample = jnp.broadcast_to(gmlp_ws[:, :, None, :dec_seq, None, :dec_seq],
                                 (DEPTH, GMLP_GROUPS, reps, dec_seq, reps, dec_seq)
                                 ).reshape(DEPTH, GMLP_GROUPS, CHUNK, CHUNK)
    bs_sample = jnp.broadcast_to(gmlp_bs[:, :, None, :dec_seq], (DEPTH, GMLP_GROUPS, reps, dec_seq)
                                 ).reshape(DEPTH, GMLP_GROUPS, CHUNK, 1)
    mixer_shared = (norm_mix.reshape(DEPTH, 1, D_MODEL), w_in_b)
    mixer_ln = (gmlp_ln_g.reshape(DEPTH, 1, GMLP_W), gmlp_ln_b.reshape(DEPTH, 1, GMLP_W))
    mixer_tail = (w_pa_b, w_branch_gmlp.astype(BF16), w_out.astype(BF16))
    gmlp_prompt = (gmlp_ws, gmlp_bs.reshape(DEPTH, GMLP_GROUPS, CHUNK, 1))
    gmlp_sample = (ws_sample, bs_sample)
    ffn_w = (norm_ffn.reshape(DEPTH, 1, D_MODEL), w_up.astype(BF16), conv_w, conv_b.reshape(DEPTH, 1, 2 * D_FF),
             w_down.astype(BF16), norm_final.reshape(1, D_MODEL))
    cache_k = cache_win_k.reshape(DEPTH, dec_batch, WINDOW, KV_W)
    cache_v = cache_win_v.reshape(DEPTH, dec_batch, WINDOW, KV_W)

    xp = x_prompt
    xs = x_sample.reshape(dec_batch * dec_seq, D_MODEL)
    mp_out = fp_out = ms_out = fs_out = None
    for l in range(DEPTH):
        sink_l = sinks[l]
        xp, *mp_out = _mixer_prompt(l, xp, mixer_shared + (sink_l,) + mixer_ln + gmlp_prompt + mixer_tail, mp_out)
        xp, *fp_out = _ffn_prompt(l, xp, ffn_w, fp_out)
        xs, *ms_out = _mixer_sample(l, xs, mixer_shared + (sink_l,) + mixer_ln + gmlp_sample + mixer_tail,
                                    cache_k, cache_v, ms_out)
        xs, *fs_out = _ffn_sample(l, xs, ffn_w, state_conv, fs_out)

    kp, vp = mp_out
    (cp,) = fp_out
    ks, vs, gv = ms_out
    (cs,) = fs_out
    kv_prompt = (DEPTH, batch, WINDOW, N_KV_HEADS, HEAD_DIM)
    kv_sample = (DEPTH, dec_batch, WINDOW, N_KV_HEADS, HEAD_DIM)
    return (xp, xs.reshape(x_sample.shape),
            kp.reshape(kv_prompt), vp.reshape(kv_prompt), cp,
            ks.reshape(kv_sample), vs.reshape(kv_sample), cs,
            gv.reshape(DEPTH, dec_batch, dec_seq, GMLP_W))
```

```python
import functools

import jax
import jax.numpy as jnp
import numpy as np
from jax import lax
from jax.experimental import pallas as pl
from jax.experimental.pallas import tpu as pltpu

D_MODEL = 1024
DEPTH = 4
HEAD_DIM = 64
N_HEADS = 16
N_KV_HEADS = 4
GQA_GROUP = N_HEADS // N_KV_HEADS
ATTN_W = N_HEADS * HEAD_DIM
KV_W = N_KV_HEADS * HEAD_DIM
WINDOW = 128
BLOCK = 128
CHUNK = 128
GMLP_CH = 128
GMLP_GROUPS = 6
GMLP_W = GMLP_GROUPS * GMLP_CH
D_FF = 2816
CONV_W = 3
EPS = 1e-5
NEG = -1e30

Q0, K0, V0, GU0, GV0, GA0, GB0, IN_COLS = 0, 1024, 1280, 1536, 2304, 3072, 4096, 5120

V7X_VMEM_BYTES = 64 * 1024 * 1024
VMEM_LIMIT_BYTES = V7X_VMEM_BYTES - 8 * 1024 * 1024
F32_SUBLANES = 8

TM_PROMPT = 256
TM_PROMPT_FFN = 256
SEQ_PER_STEP_MIXER = 16
SEQ_PER_STEP_FFN = 32
FFN_CHUNK = 256
FFN_LOOKAHEAD = 3
ATTN_LOOKAHEAD = 2
PROJ_TILE = 256

F32 = jnp.float32
BF16 = jnp.bfloat16


def _rmsnorm(x, g):
    return x * lax.rsqrt(jnp.mean(x * x, axis=-1, keepdims=True) + EPS) * g


def _layernorm(x, g, b):
    mu = jnp.mean(x, axis=-1, keepdims=True)
    xc = x - mu
    var = jnp.mean(xc * xc, axis=-1, keepdims=True)
    return xc * lax.rsqrt(var + EPS) * g + b


def _gelu(x):
    c = np.sqrt(2.0 / np.pi).astype(np.float32)
    return x * (0.5 * (1.0 + jnp.tanh(c * (x + 0.044715 * (x * x * x)))))


def _sigmoid(x):
    return 1.0 / (1.0 + jnp.exp(-x))


def _dot(a, b):
    return jnp.dot(a, b, preferred_element_type=F32)


def _dot_nt(a, b):
    return lax.dot_general(a, b, (((1,), (1,)), ((), ())), preferred_element_type=F32)


def _kv_lane_masks(rows):
    lane = lax.broadcasted_iota(jnp.int32, (rows, KV_W), 1)
    return [(lane >= h * HEAD_DIM) & (lane < (h + 1) * HEAD_DIM) for h in range(N_KV_HEADS)]


def _select_kv_lanes(masks, parts):
    out = parts[N_KV_HEADS - 1]
    for h in range(N_KV_HEADS - 2, -1, -1):
        out = jnp.where(masks[h], parts[h], out)
    return out


def _softmax_pv(s, sink, vv):
    m = jnp.maximum(jnp.max(s, axis=-1, keepdims=True), sink)
    p = jnp.exp(s - m)
    denom = jnp.sum(p, axis=-1, keepdims=True) + jnp.exp(sink - m)
    return _dot(p.astype(BF16), vv) / denom


def _gmlp_mix(vn, ws_ref, bs_ref, period):
    vnb = vn.astype(BF16)
    row = lax.broadcasted_iota(jnp.int32, (CHUNK, CHUNK), 0)
    col = lax.broadcasted_iota(jnp.int32, (CHUNK, CHUNK), 1)
    keep = (col <= row) & (col >= row - (row & (period - 1)))
    w = [jnp.where(keep, ws_ref[g], 0.0).astype(BF16) for g in range(GMLP_GROUPS)]
    rows = []
    for c in range(vn.shape[0] // CHUNK):
        cols = []
        for g in range(GMLP_GROUPS):
            blk = vnb[c * CHUNK:(c + 1) * CHUNK, g * GMLP_CH:(g + 1) * GMLP_CH]
            cols.append(_dot(w[g], blk) + bs_ref[g])
        rows.append(jnp.concatenate(cols, axis=1))
    return rows[0] if len(rows) == 1 else jnp.concatenate(rows, axis=0)


class _GateProjection:
    def __init__(self, hb, win_ref):
        self.hb, self.win_ref, self.tiles = hb, win_ref, []

    def issue(self, n_tiles):
        for _ in range(n_tiles):
            c0 = GU0 + len(self.tiles) * PROJ_TILE
            if c0 < IN_COLS:
                self.tiles.append(_dot(self.hb, self.win_ref[:, c0:c0 + PROJ_TILE]))

    def columns(self, c0, c1):
        self.issue((IN_COLS - GU0) // PROJ_TILE)
        return jnp.concatenate(self.tiles[(c0 - GU0) // PROJ_TILE:(c1 - GU0) // PROJ_TILE], axis=1)


def _gate_merge_out(x, o, proj, lng_ref, lnb_ref, ws_ref, bs_ref, wpa_ref, wpb_ref, wout_ref, period):
    u = _gelu(proj.columns(GU0, GV0))
    vn = _layernorm(_gelu(proj.columns(GV0, GA0)), lng_ref[...], lnb_ref[...])
    ga = proj.columns(GA0, GB0)
    gb = proj.columns(GB0, IN_COLS)
    oa = _dot(o.astype(BF16), wpa_ref[...])
    s_gate = u * _gmlp_mix(vn, ws_ref, bs_ref, period)
    merged = _sigmoid(ga) * oa + _sigmoid(gb) * _dot(s_gate.astype(BF16), wpb_ref[...])
    return x + _dot(merged.astype(BF16), wout_ref[...]), vn


def _mixer_prompt_kernel(x_ref, xprev_ref, nm_ref, win_ref, sink_ref, lng_ref, lnb_ref, ws_ref, bs_ref,
                         wpa_ref, wpb_ref, wout_ref,
                         xo_ref, kt_ref, vt_ref):
    i = pl.program_id(1)
    tm = x_ref.shape[0]
    x = x_ref[...]
    hb = _rmsnorm(x, nm_ref[...]).astype(BF16)

    hb_prev = _rmsnorm(xprev_ref[...], nm_ref[...]).astype(BF16)
    kv_prev = _dot(hb_prev, win_ref[:, K0:GU0])
    k = _dot(hb, win_ref[:, K0:V0])
    v = _dot(hb, win_ref[:, V0:GU0])
    q = (_dot(hb, win_ref[:, Q0:K0]) * (HEAD_DIM ** -0.5))
    proj = _GateProjection(hb, win_ref)

    kk_all = jnp.concatenate([kv_prev[:, 0:KV_W], k], axis=0).astype(BF16)
    vv_all = jnp.concatenate([kv_prev[:, KV_W:2 * KV_W], v], axis=0).astype(BF16)
    kt_ref[...] = k[tm - WINDOW:tm]
    vt_ref[...] = v[tm - WINDOW:tm]

    rows = N_KV_HEADS * BLOCK
    t = lax.broadcasted_iota(jnp.int32, (rows, 2 * BLOCK), 0) & (BLOCK - 1)
    c = lax.broadcasted_iota(jnp.int32, (rows, 2 * BLOCK), 1)
    band = (c > t) & (c <= t + WINDOW)
    bias = jnp.where(band, 0.0, NEG).astype(F32)
    first_key = jnp.where(i > 0, 0, BLOCK)
    bias_first = jnp.where(band & (c >= first_key), 0.0, NEG).astype(F32)
    head_of_row = lax.broadcasted_iota(jnp.int32, (rows, 1), 0) >> (BLOCK.bit_length() - 1)
    masks = _kv_lane_masks(BLOCK)

    items = [(j, g) for j in range(tm // BLOCK) for g in range(GQA_GROUP)]

    def scores(item):
        j, g = item
        qg = q[j * BLOCK:(j + 1) * BLOCK, g * KV_W:(g + 1) * KV_W]
        lhs = jnp.concatenate([jnp.where(masks[h], qg, 0.0) for h in range(N_KV_HEADS)], axis=0).astype(BF16)
        return _dot_nt(lhs, kk_all[j * BLOCK:(j + 2) * BLOCK]) + (bias_first if j == 0 else bias)

    fill = -(-((IN_COLS - GU0) // PROJ_TILE) // len(items))
    o_groups = {}
    ahead = [scores(item) for item in items[:ATTN_LOOKAHEAD]]
    for n, (j, g) in enumerate(items):
        s = ahead.pop(0)
        if n + ATTN_LOOKAHEAD < len(items):
            ahead.append(scores(items[n + ATTN_LOOKAHEAD]))
        proj.issue(fill)
        sink = jnp.full((rows, 1), sink_ref[(N_KV_HEADS - 1) * GQA_GROUP + g], F32)
        for h in range(N_KV_HEADS - 2, -1, -1):
            sink = jnp.where(head_of_row == h, sink_ref[h * GQA_GROUP + g], sink)
        r = _softmax_pv(s, sink, vv_all[j * BLOCK:(j + 2) * BLOCK])
        o_groups[(j, g)] = _select_kv_lanes(masks, [r[h * BLOCK:(h + 1) * BLOCK] for h in range(N_KV_HEADS)])
    o_blocks = [jnp.concatenate([o_groups[(j, g)] for g in range(GQA_GROUP)], axis=1) for j in range(tm // BLOCK)]
    o = o_blocks[0] if len(o_blocks) == 1 else jnp.concatenate(o_blocks, axis=0)

    xo_ref[...], _ = _gate_merge_out(x, o, proj, lng_ref, lnb_ref, ws_ref, bs_ref, wpa_ref, wpb_ref, wout_ref, CHUNK)


def _mixer_sample_kernel(x_ref, nm_ref, win_ref, sink_ref, lng_ref, lnb_ref, ws_ref, bs_ref,
                         wpa_ref, wpb_ref, wout_ref, ck_ref, cv_ref,
                         xo_ref, wk_ref, wv_ref, vn_ref):
    n_seq, past, _ = ck_ref.shape
    t_new = x_ref.shape[0] // n_seq
    x = x_ref[...]
    hb = _rmsnorm(x, nm_ref[...]).astype(BF16)
    q = _dot(hb, win_ref[:, Q0:K0]) * (HEAD_DIM ** -0.5)
    k = _dot(hb, win_ref[:, K0:V0])
    v = _dot(hb, win_ref[:, V0:GU0])
    proj = _GateProjection(hb, win_ref)

    n_keys = 2 * BLOCK
    rows = N_HEADS * t_new
    t = lax.broadcasted_iota(jnp.int32, (rows, n_keys), 0) & (t_new - 1)
    c = lax.broadcasted_iota(jnp.int32, (rows, n_keys), 1)
    diff = t + past - c
    bias = jnp.where((diff >= 0) & (diff < WINDOW) & (c < past + t_new), 0.0, NEG).astype(F32)
    head_slot = lax.broadcasted_iota(jnp.int32, (rows, 1), 0) >> (t_new.bit_length() - 1)
    sink = jnp.zeros((rows, 1), F32)
    for g in range(GQA_GROUP):
        for h in range(N_KV_HEADS):
            sink = jnp.where(head_slot == g * N_KV_HEADS + h, sink_ref[h * GQA_GROUP + g], sink)
    masks = _kv_lane_masks(t_new)
    zero_keys = jnp.zeros((n_keys - past - t_new, KV_W), F32)

    def keys_values(b):
        k_old = ck_ref[b]
        v_old = cv_ref[b]
        k_new = k[b * t_new:(b + 1) * t_new]
        v_new = v[b * t_new:(b + 1) * t_new]
        wk_ref[b, 0:past - t_new, :] = k_old[t_new:past]
        wk_ref[b, past - t_new:past, :] = k_new
        wv_ref[b, 0:past - t_new, :] = v_old[t_new:past]
        wv_ref[b, past - t_new:past, :] = v_new
        kk = jnp.concatenate([k_old, k_new, zero_keys], axis=0).astype(BF16)
        vv = jnp.concatenate([v_old, v_new, zero_keys], axis=0).astype(BF16)
        return kk, vv

    s_list, vv_list = [], []
    for b in range(n_seq):
        kk, vv = keys_values(b)
        qb = q[b * t_new:(b + 1) * t_new]
        lhs = jnp.concatenate(
            [jnp.where(masks[h], qb[:, g * KV_W:(g + 1) * KV_W], 0.0)
             for g in range(GQA_GROUP) for h in range(N_KV_HEADS)], axis=0).astype(BF16)
        s_list.append(_dot_nt(lhs, kk) + bias)
        vv_list.append(vv)
    proj.issue((IN_COLS - GU0) // PROJ_TILE)
    o_rows = []
    for b in range(n_seq):
        r = _softmax_pv(s_list[b], sink, vv_list[b])
        o_groups = []
        for g in range(GQA_GROUP):
            base = g * N_KV_HEADS * t_new
            o_groups.append(_select_kv_lanes(
                masks, [r[base + h * t_new:base + (h + 1) * t_new] for h in range(N_KV_HEADS)]))
        o_rows.append(jnp.concatenate(o_groups, axis=1))
    o = jnp.concatenate(o_rows, axis=0)

    xo_ref[...], vn_ref[...] = _gate_merge_out(x, o, proj, lng_ref, lnb_ref, ws_ref, bs_ref,
                                               wpa_ref, wpb_ref, wout_ref, t_new)


def _conv_gate(za, zb, sa, sb, cw_ref, cb_ref, a0, b0, width):
    def conv(z, shifted, c0):
        z1, z2 = shifted
        cols = slice(c0, c0 + width)
        return cb_ref[:, cols] + ((z2 * cw_ref[0:1, cols] + z1 * cw_ref[1:2, cols]) + z * cw_ref[2:3, cols])

    a = conv(za, sa, a0)
    b = conv(zb, sb, b0)
    c0 = np.sqrt(2.0 / np.pi).astype(np.float32)
    c1 = np.float32(c0 * np.float32(0.044715))
    th = jnp.tanh(a * (c1 * (a * a) + c0))
    half_ab = (0.5 * a) * b
    return half_ab + half_ab * th


def _ffn_body(x, nf_ref, wup_ref, cw_ref, cb_ref, wdn_ref, nfin_ref, shift_fn, final):
    hb = _rmsnorm(x, nf_ref[...]).astype(BF16)

    def up(c):
        a0 = c * FFN_CHUNK
        return _dot(hb, wup_ref[:, a0:a0 + FFN_CHUNK]), _dot(hb, wup_ref[:, D_FF + a0:D_FF + a0 + FFN_CHUNK])

    n_chunks = D_FF // FFN_CHUNK
    acc = x
    ahead = [up(c) for c in range(FFN_LOOKAHEAD)]
    for c in range(n_chunks):
        za, zb = ahead.pop(0)
        if c + FFN_LOOKAHEAD < n_chunks:
            ahead.append(up(c + FFN_LOOKAHEAD))
        a0 = c * FFN_CHUNK
        b0 = D_FF + a0
        gated = _conv_gate(za, zb, shift_fn(za, a0), shift_fn(zb, b0), cw_ref, cb_ref, a0, b0, FFN_CHUNK)
        acc = acc + _dot(gated.astype(BF16), wdn_ref[a0:a0 + FFN_CHUNK, :])
    if final:
        acc = _rmsnorm(acc, nfin_ref[...])
    return acc


def _ffn_prompt_kernel(x_ref, nf_ref, wup_ref, cw_ref, cb_ref, wdn_ref, nfin_ref,
                       xo_ref, ct_ref, carry_ref, *, final):
    tm = x_ref.shape[0]
    sub = carry_ref.shape[0]

    @pl.when(pl.program_id(1) == 0)
    def _():
        carry_ref[...] = jnp.zeros_like(carry_ref)

    def shift_fn(z, c0):
        cols = slice(c0, c0 + z.shape[1])
        ext = jnp.concatenate([carry_ref[:, cols], z], axis=0)
        last = z[tm - sub:tm]
        carry_ref[:, cols] = last
        ct_ref[:, cols] = pltpu.roll(last, CONV_W - 1, 0)[0:CONV_W - 1]
        return pltpu.roll(ext, 1, 0)[sub:], pltpu.roll(ext, 2, 0)[sub:]

    xo_ref[...] = _ffn_body(x_ref[...], nf_ref, wup_ref, cw_ref, cb_ref, wdn_ref, nfin_ref, shift_fn, final)


def _ffn_sample_kernel(x_ref, nf_ref, wup_ref, cw_ref, cb_ref, wdn_ref, nfin_ref, st_ref,
                       xo_ref, ct_ref, *, final):
    n_seq = st_ref.shape[0]
    m_rows = x_ref.shape[0]
    t_new = m_rows // n_seq

    def shift_fn(z, c0):
        width = z.shape[1]
        cols = slice(c0, c0 + width)
        z3 = z.reshape(n_seq, t_new, width)
        st = st_ref[:, :, cols]
        p0 = jnp.broadcast_to(st[:, 0:1, :], z3.shape)
        p1 = jnp.broadcast_to(st[:, 1:2, :], z3.shape)
        t = lax.broadcasted_iota(jnp.int32, z3.shape, 1)
        r1 = pltpu.roll(z3, 1, 1)
        r2 = pltpu.roll(z3, 2, 1)
        ct_ref[:, :, cols] = r2[:, 0:CONV_W - 1, :]
        z1 = jnp.where(t == 0, p1, r1)
        z2 = jnp.where(t == 0, p0, jnp.where(t == 1, p1, r2))
        return z1.reshape(m_rows, width), z2.reshape(m_rows, width)

    xo_ref[...] = _ffn_body(x_ref[...], nf_ref, wup_ref, cw_ref, cb_ref, wdn_ref, nfin_ref, shift_fn, final)


def _drop_aliased(kernel_fn, n_in, n_aliased):
    def wrapped(*refs):
        return kernel_fn(*refs[:n_in], *refs[n_in + n_aliased:])
    return wrapped


def _layer_spec(layer, shape):
    nd = len(shape)
    return pl.BlockSpec((None,) + tuple(shape), lambda *_: (layer,) + (0,) * nd, pipeline_mode=pl.Buffered(1))


_SMEM_SPEC = pl.BlockSpec(memory_space=pltpu.SMEM)
_ANY_SPEC = pl.BlockSpec(memory_space=pl.ANY)


def _compiler_params(n_axes):
    return pltpu.CompilerParams(dimension_semantics=("arbitrary",) * n_axes,
                                vmem_limit_bytes=VMEM_LIMIT_BYTES)


def _mixer_weight_specs(layer):
    return [
        _layer_spec(layer, (1, D_MODEL)),
        _layer_spec(layer, (D_MODEL, IN_COLS)),
        _SMEM_SPEC,
        _layer_spec(layer, (1, GMLP_W)),
        _layer_spec(layer, (1, GMLP_W)),
        _layer_spec(layer, (GMLP_GROUPS, CHUNK, CHUNK)),
        _layer_spec(layer, (GMLP_GROUPS, CHUNK, 1)),
        _layer_spec(layer, (ATTN_W, D_MODEL)),
        _layer_spec(layer, (GMLP_W, D_MODEL)),
        _layer_spec(layer, (D_MODEL, D_MODEL)),
    ]


def _ffn_weight_specs(layer):
    return [
        _layer_spec(layer, (1, D_MODEL)),
        _layer_spec(layer, (D_MODEL, 2 * D_FF)),
        _layer_spec(layer, (CONV_W, 2 * D_FF)),
        _layer_spec(layer, (1, 2 * D_FF)),
        _layer_spec(layer, (D_FF, D_MODEL)),
        pl.BlockSpec((1, D_MODEL), lambda *_: (0, 0), pipeline_mode=pl.Buffered(1)),
    ]


def _call_layer(kernel_fn, grid, inputs, in_specs, out_specs, out_shapes, in_place, carried, scratch, name):
    n_in = len(inputs)
    first_carried = len(out_shapes) - len(carried)
    aliases = dict(in_place)
    aliases.update({n_in + n: first_carried + n for n in range(len(carried))})
    return pl.pallas_call(
        _drop_aliased(kernel_fn, n_in, len(carried)), grid=grid,
        in_specs=list(in_specs) + [_ANY_SPEC] * len(carried), out_specs=out_specs, out_shape=out_shapes,
        scratch_shapes=scratch, input_output_aliases=aliases,
        compiler_params=_compiler_params(len(grid)), name=name,
    )(*inputs, *carried)


def _mixer_prompt(layer, x, mixer_w, k_tails, v_tails):
    batch, seq, _ = x.shape
    tm = TM_PROMPT
    tile = pl.BlockSpec((None, tm, D_MODEL), lambda b, i: (b, i, 0))
    prev_block = pl.BlockSpec((None, BLOCK, D_MODEL), lambda b, i: (b, jnp.maximum(i * (tm // BLOCK) - 1, 0), 0))
    tail = pl.BlockSpec((None, None, WINDOW, KV_W), lambda b, i: (layer, b, 0, 0))
    tail_shape = jax.ShapeDtypeStruct(k_tails.shape, F32)
    return _call_layer(
        _mixer_prompt_kernel, (batch, seq // tm), (x, x) + mixer_w,
        [tile, prev_block] + _mixer_weight_specs(layer), [tile, tail, tail],
        [jax.ShapeDtypeStruct(x.shape, F32), tail_shape, tail_shape], {}, [k_tails, v_tails], [], "mixer_prompt")


def _mixer_sample(layer, x, mixer_w, win_k, win_v, vn_all):
    n_rows = x.shape[0]
    _, n_seq_total, past, _ = win_k.shape
    t_new = n_rows // n_seq_total
    ns = SEQ_PER_STEP_MIXER
    m = ns * t_new
    tile = pl.BlockSpec((m, D_MODEL), lambda i: (i, 0))
    win = pl.BlockSpec((None, ns, past, KV_W), lambda i: (layer, i, 0, 0))
    vn_spec = pl.BlockSpec((None, m, GMLP_W), lambda i: (layer, i, 0))
    inputs = (x,) + mixer_w + (win_k, win_v)
    return _call_layer(
        _mixer_sample_kernel, (n_seq_total // ns,), inputs,
        [tile] + _mixer_weight_specs(layer) + [win, win], [tile, win, win, vn_spec],
        [jax.ShapeDtypeStruct(x.shape, F32), jax.ShapeDtypeStruct(win_k.shape, F32),
         jax.ShapeDtypeStruct(win_v.shape, F32), jax.ShapeDtypeStruct(vn_all.shape, F32)],
        {len(inputs) - 2: 1, len(inputs) - 1: 2}, [vn_all], [], "mixer_sample")


def _ffn_prompt(layer, x, ffn_w, conv_tails):
    batch, seq, _ = x.shape
    tm = TM_PROMPT_FFN
    tile = pl.BlockSpec((None, tm, D_MODEL), lambda b, i: (b, i, 0))
    tail = pl.BlockSpec((None, None, CONV_W - 1, 2 * D_FF), lambda b, i: (layer, b, 0, 0))
    return _call_layer(
        functools.partial(_ffn_prompt_kernel, final=layer == DEPTH - 1), (batch, seq // tm), (x,) + ffn_w,
        [tile] + _ffn_weight_specs(layer), [tile, tail],
        [jax.ShapeDtypeStruct(x.shape, F32), jax.ShapeDtypeStruct(conv_tails.shape, F32)],
        {}, [conv_tails], [pltpu.VMEM((F32_SUBLANES, 2 * D_FF), F32)], "ffn_prompt")


def _ffn_sample(layer, x, ffn_w, state):
    n_rows = x.shape[0]
    n_seq_total = state.shape[1]
    t_new = n_rows // n_seq_total
    ns = SEQ_PER_STEP_FFN
    m = ns * t_new
    tile = pl.BlockSpec((m, D_MODEL), lambda i: (i, 0))
    st = pl.BlockSpec((None, ns, CONV_W - 1, 2 * D_FF), lambda i: (layer, i, 0, 0))
    inputs = (x,) + ffn_w + (state,)
    return _call_layer(
        functools.partial(_ffn_sample_kernel, final=layer == DEPTH - 1), (n_seq_total // ns,), inputs,
        [tile] + _ffn_weight_specs(layer) + [st], [tile, st],
        [jax.ShapeDtypeStruct(x.shape, F32), jax.ShapeDtypeStruct(state.shape, F32)],
        {len(inputs) - 1: 1}, [], [], "ffn_sample")


def _heads_group_major(w, axis):
    shape = w.shape
    split = shape[:axis] + (N_KV_HEADS, GQA_GROUP, HEAD_DIM) + shape[axis + 1:]
    return jnp.swapaxes(w.reshape(split), axis, axis + 1).reshape(shape)


def kernel(x_prompt, x_sample, cache_win_k, cache_win_v, state_conv, norm_mix, w_in, sinks, gmlp_ln_g, gmlp_ln_b, gmlp_ws, gmlp_bs, w_branch_attn, w_branch_gmlp, w_out, norm_ffn, w_up, conv_w, conv_b, w_down, norm_final):
    batch = x_prompt.shape[0]
    dec_batch, dec_seq, _ = x_sample.shape
    w_in_b = jnp.concatenate([_heads_group_major(w_in[:, :, :ATTN_W], 2), w_in[:, :, ATTN_W:]], axis=2).astype(BF16)
    w_pa_b = _heads_group_major(w_branch_attn, 1).astype(BF16)
    reps = CHUNK // dec_seq
    ws_sample = jnp.broadcast_to(gmlp_ws[:, :, None, :dec_seq, None, :dec_seq],
                                 (DEPTH, GMLP_GROUPS, reps, dec_seq, reps, dec_seq)
                                 ).reshape(DEPTH, GMLP_GROUPS, CHUNK, CHUNK)
    bs_sample = jnp.broadcast_to(gmlp_bs[:, :, None, :dec_seq], (DEPTH, GMLP_GROUPS, reps, dec_seq)
                                 ).reshape(DEPTH, GMLP_GROUPS, CHUNK, 1)
    mixer_shared = (norm_mix.reshape(DEPTH, 1, D_MODEL), w_in_b)
    mixer_ln = (gmlp_ln_g.reshape(DEPTH, 1, GMLP_W), gmlp_ln_b.reshape(DEPTH, 1, GMLP_W))
    mixer_tail = (w_pa_b, w_branch_gmlp.astype(BF16), w_out.astype(BF16))
    gmlp_prompt = (gmlp_ws, gmlp_bs.reshape(DEPTH, GMLP_GROUPS, CHUNK, 1))
    gmlp_sample = (ws_sample, bs_sample)
    ffn_w = (norm_ffn.reshape(DEPTH, 1, D_MODEL), w_up.astype(BF16), conv_w, conv_b.reshape(DEPTH, 1, 2 * D_FF),
             w_down.astype(BF16), norm_final.reshape(1, D_MODEL))
    cache_k = cache_win_k.reshape(DEPTH, dec_batch, WINDOW, KV_W)
    cache_v = cache_win_v.reshape(DEPTH, dec_batch, WINDOW, KV_W)

    xp = x_prompt
    xs = x_sample.reshape(dec_batch * dec_seq, D_MODEL)
    kp = jnp.zeros((DEPTH, batch, WINDOW, KV_W), F32)
    vp = jnp.zeros((DEPTH, batch, WINDOW, KV_W), F32)
    cp = jnp.zeros((DEPTH, batch, CONV_W - 1, 2 * D_FF), F32)
    gv = jnp.zeros((DEPTH, dec_batch * dec_seq, GMLP_W), F32)
    ks, vs, cs = cache_k, cache_v, state_conv
    for l in range(DEPTH):
        sink_l = sinks[l]
        xp, kp, vp = _mixer_prompt(l, xp, mixer_shared + (sink_l,) + mixer_ln + gmlp_prompt + mixer_tail, kp, vp)
        xp, cp = _ffn_prompt(l, xp, ffn_w, cp)
        xs, ks, vs, gv = _mixer_sample(l, xs, mixer_shared + (sink_l,) + mixer_ln + gmlp_sample + mixer_tail,
                                       ks, vs, gv)
        xs, cs = _ffn_sample(l, xs, ffn_w, cs)

    kv_prompt = (DEPTH, batch, WINDOW, N_KV_HEADS, HEAD_DIM)
    kv_sample = (DEPTH, dec_batch, WINDOW, N_KV_HEADS, HEAD_DIM)
    return (xp, xs.reshape(x_sample.shape),
            kp.reshape(kv_prompt), vp.reshape(kv_prompt), cp,
            ks.reshape(kv_sample), vs.reshape(kv_sample), cs,
            gv.reshape(DEPTH, dec_batch, dec_seq, GMLP_W))
```

```python
import functools

import jax
import jax.numpy as jnp
import numpy as np
from jax import lax
from jax.experimental import pallas as pl
from jax.experimental.pallas import tpu as pltpu

D_MODEL = 1024
DEPTH = 4
HEAD_DIM = 64
N_HEADS = 16
N_KV_HEADS = 4
GQA_GROUP = N_HEADS // N_KV_HEADS
ATTN_W = N_HEADS * HEAD_DIM
KV_W = N_KV_HEADS * HEAD_DIM
WINDOW = 128
BLOCK = 128
CHUNK = 128
GMLP_CH = 128
GMLP_GROUPS = 6
GMLP_W = GMLP_GROUPS * GMLP_CH
D_FF = 2816
CONV_W = 3
EPS = 1e-5
NEG = -1e30

K0, V0, GU0, GV0, GA0, GB0, REST_COLS = 0, 256, 512, 1280, 2048, 3072, 4096

V7X_VMEM_BYTES = 64 * 1024 * 1024
VMEM_LIMIT_BYTES = V7X_VMEM_BYTES - 8 * 1024 * 1024
F32_SUBLANES = 8

TM_PROMPT = 256
TM_PROMPT_FFN = 256
SEQ_PER_STEP_MIXER = 16
SEQ_PER_STEP_FFN = 32
FFN_CHUNK = 256
FFN_LOOKAHEAD = 3
ATTN_LOOKAHEAD = 2
PROJ_TILE = 256

F32 = jnp.float32
BF16 = jnp.bfloat16


def _rmsnorm(x, g):
    return x * lax.rsqrt(jnp.mean(x * x, axis=-1, keepdims=True) + EPS) * g


def _layernorm(x, g, b):
    mu = jnp.mean(x, axis=-1, keepdims=True)
    xc = x - mu
    var = jnp.mean(xc * xc, axis=-1, keepdims=True)
    return xc * lax.rsqrt(var + EPS) * g + b


def _gelu(x):
    c = np.sqrt(2.0 / np.pi).astype(np.float32)
    return x * (0.5 * (1.0 + jnp.tanh(c * (x + 0.044715 * (x * x * x)))))


def _sigmoid(x):
    return 1.0 / (1.0 + jnp.exp(-x))


def _dot(a, b):
    return jnp.dot(a, b, preferred_element_type=F32)


def _dot_nt(a, b):
    return lax.dot_general(a, b, (((1,), (1,)), ((), ())), preferred_element_type=F32)


def _kv_lane_masks(rows):
    lane = lax.broadcasted_iota(jnp.int32, (rows, KV_W), 1)
    return [(lane >= h * HEAD_DIM) & (lane < (h + 1) * HEAD_DIM) for h in range(N_KV_HEADS)]


def _select_kv_lanes(masks, parts):
    out = parts[N_KV_HEADS - 1]
    for h in range(N_KV_HEADS - 2, -1, -1):
        out = jnp.where(masks[h], parts[h], out)
    return out


def _softmax_pv(s, sink, vv):
    m = jnp.maximum(jnp.max(s, axis=-1, keepdims=True), sink)
    p = jnp.exp(s - m)
    denom = jnp.sum(p, axis=-1, keepdims=True) + jnp.exp(sink - m)
    return _dot(p.astype(BF16), vv) / denom


def _gmlp_mix(vn, ws_ref, bs_ref, period):
    vnb = vn.astype(BF16)
    row = lax.broadcasted_iota(jnp.int32, (CHUNK, CHUNK), 0)
    col = lax.broadcasted_iota(jnp.int32, (CHUNK, CHUNK), 1)
    keep = (col <= row) & (col >= row - (row & (period - 1)))
    w = [jnp.where(keep, ws_ref[g], 0.0).astype(BF16) for g in range(GMLP_GROUPS)]
    rows = []
    for c in range(vn.shape[0] // CHUNK):
        cols = []
        for g in range(GMLP_GROUPS):
            blk = vnb[c * CHUNK:(c + 1) * CHUNK, g * GMLP_CH:(g + 1) * GMLP_CH]
            cols.append(_dot(w[g], blk) + bs_ref[g])
        rows.append(jnp.concatenate(cols, axis=1))
    return rows[0] if len(rows) == 1 else jnp.concatenate(rows, axis=0)


class _GateProjection:
    def __init__(self, hb, wr_ref):
        self.hb, self.wr_ref, self.tiles = hb, wr_ref, []

    def issue(self, n_tiles):
        for _ in range(n_tiles):
            c0 = GU0 + len(self.tiles) * PROJ_TILE
            if c0 < REST_COLS:
                self.tiles.append(_dot(self.hb, self.wr_ref[:, c0:c0 + PROJ_TILE]))

    def columns(self, c0, c1):
        self.issue((REST_COLS - GU0) // PROJ_TILE)
        return jnp.concatenate(self.tiles[(c0 - GU0) // PROJ_TILE:(c1 - GU0) // PROJ_TILE], axis=1)


def _gate_merge_out(x, o, proj, lng_ref, lnb_ref, ws_ref, bs_ref, wpa_ref, wpb_ref, wout_ref, period):
    u = _gelu(proj.columns(GU0, GV0))
    vn = _layernorm(_gelu(proj.columns(GV0, GA0)), lng_ref[...], lnb_ref[...])
    ga = proj.columns(GA0, GB0)
    gb = proj.columns(GB0, REST_COLS)
    oa = _dot(o.astype(BF16), wpa_ref[...])
    s_gate = u * _gmlp_mix(vn, ws_ref, bs_ref, period)
    merged = _sigmoid(ga) * oa + _sigmoid(gb) * _dot(s_gate.astype(BF16), wpb_ref[...])
    return x + _dot(merged.astype(BF16), wout_ref[...]), vn


def _mixer_prompt_kernel(x_ref, xprev_ref, nm_ref, wq_ref, wr_ref, sink_ref, lng_ref, lnb_ref, ws_ref, bs_ref,
                         wpa_ref, wpb_ref, wout_ref,
                         xo_ref, kt_ref, vt_ref):
    i = pl.program_id(1)
    tm = x_ref.shape[0]
    x = x_ref[...]
    hb = _rmsnorm(x, nm_ref[...]).astype(BF16)

    hb_prev = _rmsnorm(xprev_ref[...], nm_ref[...]).astype(BF16)
    kv_prev = _dot(hb_prev, wr_ref[:, K0:GU0])
    k = _dot(hb, wr_ref[:, K0:V0])
    v = _dot(hb, wr_ref[:, V0:GU0])
    q = _dot(hb, wq_ref[...]) * (HEAD_DIM ** -0.5)
    proj = _GateProjection(hb, wr_ref)

    kk_all = jnp.concatenate([kv_prev[:, 0:KV_W], k], axis=0).astype(BF16)
    vv_all = jnp.concatenate([kv_prev[:, KV_W:2 * KV_W], v], axis=0).astype(BF16)
    kt_ref[...] = k[tm - WINDOW:tm]
    vt_ref[...] = v[tm - WINDOW:tm]

    rows = N_KV_HEADS * BLOCK
    t = lax.broadcasted_iota(jnp.int32, (rows, 2 * BLOCK), 0) & (BLOCK - 1)
    c = lax.broadcasted_iota(jnp.int32, (rows, 2 * BLOCK), 1)
    band = (c > t) & (c <= t + WINDOW)
    bias = jnp.where(band, 0.0, NEG).astype(F32)
    first_key = jnp.where(i > 0, 0, BLOCK)
    bias_first = jnp.where(band & (c >= first_key), 0.0, NEG).astype(F32)
    head_of_row = lax.broadcasted_iota(jnp.int32, (rows, 1), 0) >> (BLOCK.bit_length() - 1)
    masks = _kv_lane_masks(BLOCK)

    items = [(j, g) for j in range(tm // BLOCK) for g in range(GQA_GROUP)]

    def scores(item):
        j, g = item
        qg = q[j * BLOCK:(j + 1) * BLOCK, g * KV_W:(g + 1) * KV_W]
        lhs = jnp.concatenate([jnp.where(masks[h], qg, 0.0) for h in range(N_KV_HEADS)], axis=0).astype(BF16)
        return _dot_nt(lhs, kk_all[j * BLOCK:(j + 2) * BLOCK]) + (bias_first if j == 0 else bias)

    fill = -(-((REST_COLS - GU0) // PROJ_TILE) // len(items))
    o_groups = {}
    ahead = [scores(item) for item in items[:ATTN_LOOKAHEAD]]
    for n, (j, g) in enumerate(items):
        s = ahead.pop(0)
        if n + ATTN_LOOKAHEAD < len(items):
            ahead.append(scores(items[n + ATTN_LOOKAHEAD]))
        proj.issue(fill)
        sink = jnp.full((rows, 1), sink_ref[(N_KV_HEADS - 1) * GQA_GROUP + g], F32)
        for h in range(N_KV_HEADS - 2, -1, -1):
            sink = jnp.where(head_of_row == h, sink_ref[h * GQA_GROUP + g], sink)
        r = _softmax_pv(s, sink, vv_all[j * BLOCK:(j + 2) * BLOCK])
        o_groups[(j, g)] = _select_kv_lanes(masks, [r[h * BLOCK:(h + 1) * BLOCK] for h in range(N_KV_HEADS)])
    o_blocks = [jnp.concatenate([o_groups[(j, g)] for g in range(GQA_GROUP)], axis=1) for j in range(tm // BLOCK)]
    o = o_blocks[0] if len(o_blocks) == 1 else jnp.concatenate(o_blocks, axis=0)

    xo_ref[...], _ = _gate_merge_out(x, o, proj, lng_ref, lnb_ref, ws_ref, bs_ref, wpa_ref, wpb_ref, wout_ref, CHUNK)


def _mixer_sample_kernel(x_ref, nm_ref, wq_ref, wr_ref, sink_ref, lng_ref, lnb_ref, ws_ref, bs_ref,
                         wpa_ref, wpb_ref, wout_ref, wkvt_ref, ckt_ref, cvt_ref,
                         xo_ref, wkt_ref, wvt_ref, vn_ref):
    n_seq, _, past = ckt_ref.shape
    m_rows = x_ref.shape[0]
    t_new = m_rows // n_seq
    assert m_rows == past and past == BLOCK
    x = x_ref[...]
    hb = _rmsnorm(x, nm_ref[...]).astype(BF16)
    q = _dot(hb, wq_ref[...]) * (HEAD_DIM ** -0.5)
    k = _dot(hb, wr_ref[:, K0:V0])
    v = _dot(hb, wr_ref[:, V0:GU0])
    kvt = _dot_nt(wkvt_ref[...], hb)
    proj = _GateProjection(hb, wr_ref)

    n_keys = 2 * BLOCK
    rows = N_HEADS * t_new
    t = lax.broadcasted_iota(jnp.int32, (rows, n_keys), 0) & (t_new - 1)
    c = lax.broadcasted_iota(jnp.int32, (rows, n_keys), 1)
    diff = t + past - c
    bias = jnp.where((diff >= 0) & (diff < WINDOW) & (c < past + t_new), 0.0, NEG).astype(F32)
    head_slot = lax.broadcasted_iota(jnp.int32, (rows, 1), 0) >> (t_new.bit_length() - 1)
    sink = jnp.zeros((rows, 1), F32)
    for g in range(GQA_GROUP):
        for h in range(N_KV_HEADS):
            sink = jnp.where(head_slot == g * N_KV_HEADS + h, sink_ref[h * GQA_GROUP + g], sink)
    masks = _kv_lane_masks(t_new)
    zero_rows = jnp.zeros((BLOCK - t_new, KV_W), F32)
    keep_old = lax.broadcasted_iota(jnp.int32, (KV_W, past), 1) < past - t_new

    def new_window(old_t, new_t, b):
        shift = (past - t_new - b * t_new) % past
        placed = pltpu.roll(new_t, shift, 1) if shift else new_t
        return jnp.where(keep_old, pltpu.roll(old_t, past - t_new, 1), placed)

    s_list, v_list = [], []
    for b in range(n_seq):
        kct = ckt_ref[b]
        vct = cvt_ref[b]
        wkt_ref[b] = new_window(kct, kvt[0:KV_W], b)
        wvt_ref[b] = new_window(vct, kvt[KV_W:2 * KV_W], b)
        rows_b = slice(b * t_new, (b + 1) * t_new)
        k_pad = jnp.concatenate([k[rows_b], zero_rows], axis=0).astype(BF16)
        v_pad = jnp.concatenate([v[rows_b], zero_rows], axis=0).astype(BF16)
        qb = q[rows_b]
        lhs = jnp.concatenate(
            [jnp.where(masks[h], qb[:, g * KV_W:(g + 1) * KV_W], 0.0)
             for g in range(GQA_GROUP) for h in range(N_KV_HEADS)], axis=0).astype(BF16)
        s_list.append(jnp.concatenate([_dot(lhs, kct.astype(BF16)), _dot_nt(lhs, k_pad)], axis=1) + bias)
        v_list.append((vct.astype(BF16), v_pad))
    proj.issue((REST_COLS - GU0) // PROJ_TILE)
    o_rows = []
    for b in range(n_seq):
        s = s_list[b]
        vct_b, v_pad = v_list[b]
        m = jnp.maximum(jnp.max(s, axis=-1, keepdims=True), sink)
        p = jnp.exp(s - m)
        denom = jnp.sum(p, axis=-1, keepdims=True) + jnp.exp(sink - m)
        pb = p.astype(BF16)
        r = (_dot_nt(pb[:, 0:past], vct_b) + _dot(pb[:, past:n_keys], v_pad)) / denom
        o_groups = []
        for g in range(GQA_GROUP):
            base = g * N_KV_HEADS * t_new
            o_groups.append(_select_kv_lanes(
                masks, [r[base + h * t_new:base + (h + 1) * t_new] for h in range(N_KV_HEADS)]))
        o_rows.append(jnp.concatenate(o_groups, axis=1))
    o = jnp.concatenate(o_rows, axis=0)

    xo_ref[...], vn_ref[...] = _gate_merge_out(x, o, proj, lng_ref, lnb_ref, ws_ref, bs_ref,
                                               wpa_ref, wpb_ref, wout_ref, t_new)


def _conv_gate(za, zb, sa, sb, cw_ref, cb_ref, a0, b0, width):
    def conv(z, shifted, c0):
        z1, z2 = shifted
        cols = slice(c0, c0 + width)
        return cb_ref[:, cols] + ((z2 * cw_ref[0:1, cols] + z1 * cw_ref[1:2, cols]) + z * cw_ref[2:3, cols])

    a = conv(za, sa, a0)
    b = conv(zb, sb, b0)
    c0 = np.sqrt(2.0 / np.pi).astype(np.float32)
    c1 = np.float32(c0 * np.float32(0.044715))
    th = jnp.tanh(a * (c1 * (a * a) + c0))
    half_ab = (0.5 * a) * b
    return half_ab + half_ab * th


def _ffn_body(x, nf_ref, wup_ref, cw_ref, cb_ref, wdn_ref, nfin_ref, shift_fn, final):
    hb = _rmsnorm(x, nf_ref[...]).astype(BF16)

    def up(c):
        a0 = c * FFN_CHUNK
        return _dot(hb, wup_ref[:, a0:a0 + FFN_CHUNK]), _dot(hb, wup_ref[:, D_FF + a0:D_FF + a0 + FFN_CHUNK])

    n_chunks = D_FF // FFN_CHUNK
    acc = x
    ahead = [up(c) for c in range(FFN_LOOKAHEAD)]
    for c in range(n_chunks):
        za, zb = ahead.pop(0)
        if c + FFN_LOOKAHEAD < n_chunks:
            ahead.append(up(c + FFN_LOOKAHEAD))
        a0 = c * FFN_CHUNK
        b0 = D_FF + a0
        gated = _conv_gate(za, zb, shift_fn(za, a0), shift_fn(zb, b0), cw_ref, cb_ref, a0, b0, FFN_CHUNK)
        acc = acc + _dot(gated.astype(BF16), wdn_ref[a0:a0 + FFN_CHUNK, :])
    if final:
        acc = _rmsnorm(acc, nfin_ref[...])
    return acc


def _ffn_prompt_kernel(x_ref, nf_ref, wup_ref, cw_ref, cb_ref, wdn_ref, nfin_ref,
                       xo_ref, ct_ref, carry_ref, *, final):
    tm = x_ref.shape[0]
    sub = carry_ref.shape[0]

    @pl.when(pl.program_id(1) == 0)
    def _():
        carry_ref[...] = jnp.zeros_like(carry_ref)

    def shift_fn(z, c0):
        cols = slice(c0, c0 + z.shape[1])
        ext = jnp.concatenate([carry_ref[:, cols], z], axis=0)
        last = z[tm - sub:tm]
        carry_ref[:, cols] = last
        ct_ref[:, cols] = pltpu.roll(last, CONV_W - 1, 0)[0:CONV_W - 1]
        return pltpu.roll(ext, 1, 0)[sub:], pltpu.roll(ext, 2, 0)[sub:]

    xo_ref[...] = _ffn_body(x_ref[...], nf_ref, wup_ref, cw_ref, cb_ref, wdn_ref, nfin_ref, shift_fn, final)


def _ffn_sample_kernel(x_ref, nf_ref, wup_ref, cw_ref, cb_ref, wdn_ref, nfin_ref, st_ref,
                       xo_ref, ct_ref, *, final):
    n_seq = st_ref.shape[0]
    m_rows = x_ref.shape[0]
    t_new = m_rows // n_seq

    def shift_fn(z, c0):
        width = z.shape[1]
        cols = slice(c0, c0 + width)
        z3 = z.reshape(n_seq, t_new, width)
        st = st_ref[:, :, cols]
        p0 = jnp.broadcast_to(st[:, 0:1, :], z3.shape)
        p1 = jnp.broadcast_to(st[:, 1:2, :], z3.shape)
        t = lax.broadcasted_iota(jnp.int32, z3.shape, 1)
        r1 = pltpu.roll(z3, 1, 1)
        r2 = pltpu.roll(z3, 2, 1)
        ct_ref[:, :, cols] = r2[:, 0:CONV_W - 1, :]
        z1 = jnp.where(t == 0, p1, r1)
        z2 = jnp.where(t == 0, p0, jnp.where(t == 1, p1, r2))
        return z1.reshape(m_rows, width), z2.reshape(m_rows, width)

    xo_ref[...] = _ffn_body(x_ref[...], nf_ref, wup_ref, cw_ref, cb_ref, wdn_ref, nfin_ref, shift_fn, final)


def _drop_aliased(kernel_fn, n_in, n_aliased):
    def wrapped(*refs):
        return kernel_fn(*refs[:n_in], *refs[n_in + n_aliased:])
    return wrapped


def _layer_spec(layer, shape):
    nd = len(shape)
    return pl.BlockSpec((None,) + tuple(shape), lambda *_: (layer,) + (0,) * nd, pipeline_mode=pl.Buffered(1))


_SMEM_SPEC = pl.BlockSpec(memory_space=pltpu.SMEM)
_ANY_SPEC = pl.BlockSpec(memory_space=pl.ANY)


def _compiler_params(n_axes):
    return pltpu.CompilerParams(dimension_semantics=("arbitrary",) * n_axes,
                                vmem_limit_bytes=VMEM_LIMIT_BYTES)


def _mixer_weight_specs(layer):
    return [
        _layer_spec(layer, (1, D_MODEL)),
        _layer_spec(layer, (D_MODEL, ATTN_W)),
        _layer_spec(layer, (D_MODEL, REST_COLS)),
        _SMEM_SPEC,
        _layer_spec(layer, (1, GMLP_W)),
        _layer_spec(layer, (1, GMLP_W)),
        _layer_spec(layer, (GMLP_GROUPS, CHUNK, CHUNK)),
        _layer_spec(layer, (GMLP_GROUPS, CHUNK, 1)),
        _layer_spec(layer, (ATTN_W, D_MODEL)),
        _layer_spec(layer, (GMLP_W, D_MODEL)),
        _layer_spec(layer, (D_MODEL, D_MODEL)),
    ]


def _ffn_weight_specs(layer):
    return [
        _layer_spec(layer, (1, D_MODEL)),
        _layer_spec(layer, (D_MODEL, 2 * D_FF)),
        _layer_spec(layer, (CONV_W, 2 * D_FF)),
        _layer_spec(layer, (1, 2 * D_FF)),
        _layer_spec(layer, (D_FF, D_MODEL)),
        pl.BlockSpec((1, D_MODEL), lambda *_: (0, 0), pipeline_mode=pl.Buffered(1)),
    ]


def _call_layer(kernel_fn, grid, inputs, in_specs, out_specs, out_shapes, carried, scratch, name):
    n_in = len(inputs)
    first_carried = len(out_shapes) - len(carried)
    aliases = {n_in + n: first_carried + n for n in range(len(carried))}
    return pl.pallas_call(
        _drop_aliased(kernel_fn, n_in, len(carried)), grid=grid,
        in_specs=list(in_specs) + [_ANY_SPEC] * len(carried), out_specs=out_specs, out_shape=out_shapes,
        scratch_shapes=scratch, input_output_aliases=aliases,
        compiler_params=_compiler_params(len(grid)), name=name,
    )(*inputs, *carried)


def _mixer_prompt(layer, x, mixer_w, k_tails, v_tails):
    batch, seq, _ = x.shape
    tm = TM_PROMPT
    tile = pl.BlockSpec((None, tm, D_MODEL), lambda b, i: (b, i, 0))
    prev_block = pl.BlockSpec((None, BLOCK, D_MODEL), lambda b, i: (b, jnp.maximum(i * (tm // BLOCK) - 1, 0), 0))
    tail = pl.BlockSpec((None, None, WINDOW, KV_W), lambda b, i: (layer, b, 0, 0))
    tail_shape = jax.ShapeDtypeStruct(k_tails.shape, F32)
    return _call_layer(
        _mixer_prompt_kernel, (batch, seq // tm), (x, x) + mixer_w,
        [tile, prev_block] + _mixer_weight_specs(layer), [tile, tail, tail],
        [jax.ShapeDtypeStruct(x.shape, F32), tail_shape, tail_shape], [k_tails, v_tails], [], "mixer_prompt")


def _mixer_sample(layer, x, mixer_w, w_kvt, cache_k, cache_v, win_k, win_v, vn_all):
    n_rows = x.shape[0]
    _, n_seq_total, _, past = cache_k.shape
    t_new = n_rows // n_seq_total
    ns = SEQ_PER_STEP_MIXER
    m = ns * t_new
    tile = pl.BlockSpec((m, D_MODEL), lambda i: (i, 0))
    win = pl.BlockSpec((None, ns, KV_W, past), lambda i: (layer, i, 0, 0))
    vn_spec = pl.BlockSpec((None, m, GMLP_W), lambda i: (layer, i, 0))
    return _call_layer(
        _mixer_sample_kernel, (n_seq_total // ns,), (x,) + mixer_w + (w_kvt, cache_k, cache_v),
        [tile] + _mixer_weight_specs(layer) + [_layer_spec(layer, (2 * KV_W, D_MODEL)), win, win],
        [tile, win, win, vn_spec],
        [jax.ShapeDtypeStruct(x.shape, F32), jax.ShapeDtypeStruct(win_k.shape, F32),
         jax.ShapeDtypeStruct(win_v.shape, F32), jax.ShapeDtypeStruct(vn_all.shape, F32)],
        [win_k, win_v, vn_all], [], "mixer_sample")


def _ffn_prompt(layer, x, ffn_w, conv_tails):
    batch, seq, _ = x.shape
    tm = TM_PROMPT_FFN
    tile = pl.BlockSpec((None, tm, D_MODEL), lambda b, i: (b, i, 0))
    tail = pl.BlockSpec((None, None, CONV_W - 1, 2 * D_FF), lambda b, i: (layer, b, 0, 0))
    return _call_layer(
        functools.partial(_ffn_prompt_kernel, final=layer == DEPTH - 1), (batch, seq // tm), (x,) + ffn_w,
        [tile] + _ffn_weight_specs(layer), [tile, tail],
        [jax.ShapeDtypeStruct(x.shape, F32), jax.ShapeDtypeStruct(conv_tails.shape, F32)],
        [conv_tails], [pltpu.VMEM((F32_SUBLANES, 2 * D_FF), F32)], "ffn_prompt")


def _ffn_sample(layer, x, ffn_w, state, new_state):
    n_rows = x.shape[0]
    n_seq_total = state.shape[1]
    t_new = n_rows // n_seq_total
    ns = SEQ_PER_STEP_FFN
    m = ns * t_new
    tile = pl.BlockSpec((m, D_MODEL), lambda i: (i, 0))
    st = pl.BlockSpec((None, ns, CONV_W - 1, 2 * D_FF), lambda i: (layer, i, 0, 0))
    return _call_layer(
        functools.partial(_ffn_sample_kernel, final=layer == DEPTH - 1), (n_seq_total // ns,),
        (x,) + ffn_w + (state,), [tile] + _ffn_weight_specs(layer) + [st], [tile, st],
        [jax.ShapeDtypeStruct(x.shape, F32), jax.ShapeDtypeStruct(state.shape, F32)],
        [new_state], [], "ffn_sample")


def _heads_group_major(w, axis):
    shape = w.shape
    split = shape[:axis] + (N_KV_HEADS, GQA_GROUP, HEAD_DIM) + shape[axis + 1:]
    return jnp.swapaxes(w.reshape(split), axis, axis + 1).reshape(shape)


def kernel(x_prompt, x_sample, cache_win_k, cache_win_v, state_conv, norm_mix, w_in, sinks, gmlp_ln_g, gmlp_ln_b, gmlp_ws, gmlp_bs, w_branch_attn, w_branch_gmlp, w_out, norm_ffn, w_up, conv_w, conv_b, w_down, norm_final):
    batch = x_prompt.shape[0]
    dec_batch, dec_seq, _ = x_sample.shape
    w_q_b = _heads_group_major(w_in[:, :, :ATTN_W], 2).astype(BF16)
    w_rest_b = w_in[:, :, ATTN_W:].astype(BF16)
    w_kvt_b = jnp.swapaxes(w_in[:, :, ATTN_W + K0:ATTN_W + GU0], 1, 2).astype(BF16)
    w_pa_b = _heads_group_major(w_branch_attn, 1).astype(BF16)
    reps = CHUNK // dec_seq
    ws_sample = jnp.broadcast_to(gmlp_ws[:, :, None, :dec_seq, None, :dec_seq],
                                 (DEPTH, GMLP_GROUPS, reps, dec_seq, reps, dec_seq)
                                 ).reshape(DEPTH, GMLP_GROUPS, CHUNK, CHUNK)
    bs_sample = jnp.broadcast_to(gmlp_bs[:, :, None, :dec_seq], (DEPTH, GMLP_GROUPS, reps, dec_seq)
                                 ).reshape(DEPTH, GMLP_GROUPS, CHUNK, 1)
    mixer_shared = (norm_mix.reshape(DEPTH, 1, D_MODEL), w_q_b, w_rest_b)
    mixer_ln = (gmlp_ln_g.reshape(DEPTH, 1, GMLP_W), gmlp_ln_b.reshape(DEPTH, 1, GMLP_W))
    mixer_tail = (w_pa_b, w_branch_gmlp.astype(BF16), w_out.astype(BF16))
    gmlp_prompt = (gmlp_ws, gmlp_bs.reshape(DEPTH, GMLP_GROUPS, CHUNK, 1))
    gmlp_sample = (ws_sample, bs_sample)
    ffn_w = (norm_ffn.reshape(DEPTH, 1, D_MODEL), w_up.astype(BF16), conv_w, conv_b.reshape(DEPTH, 1, 2 * D_FF),
             w_down.astype(BF16), norm_final.reshape(1, D_MODEL))
    cache_k = jnp.transpose(cache_win_k, (0, 1, 3, 4, 2)).reshape(DEPTH, dec_batch, KV_W, WINDOW)
    cache_v = jnp.transpose(cache_win_v, (0, 1, 3, 4, 2)).reshape(DEPTH, dec_batch, KV_W, WINDOW)

    xp = x_prompt
    xs = x_sample.reshape(dec_batch * dec_seq, D_MODEL)
    kp = jnp.zeros((DEPTH, batch, WINDOW, KV_W), F32)
    vp = jnp.zeros((DEPTH, batch, WINDOW, KV_W), F32)
    cp = jnp.zeros((DEPTH, batch, CONV_W - 1, 2 * D_FF), F32)
    gv = jnp.zeros((DEPTH, dec_batch * dec_seq, GMLP_W), F32)
    ks = jnp.zeros(cache_k.shape, F32)
    vs = jnp.zeros(cache_v.shape, F32)
    cs = jnp.zeros(state_conv.shape, F32)
    for l in range(DEPTH):
        sink_l = sinks[l]
        xp, kp, vp = _mixer_prompt(l, xp, mixer_shared + (sink_l,) + mixer_ln + gmlp_prompt + mixer_tail, kp, vp)
        xp, cp = _ffn_prompt(l, xp, ffn_w, cp)
        xs, ks, vs, gv = _mixer_sample(l, xs, mixer_shared + (sink_l,) + mixer_ln + gmlp_sample + mixer_tail,
                                       w_kvt_b, cache_k, cache_v, ks, vs, gv)
        xs, cs = _ffn_sample(l, xs, ffn_w, state_conv, cs)

    kv_prompt = (DEPTH, batch, WINDOW, N_KV_HEADS, HEAD_DIM)
    kv_sample_t = (DEPTH, dec_batch, N_KV_HEADS, HEAD_DIM, WINDOW)
    return (xp, xs.reshape(x_sample.shape),
            kp.reshape(kv_prompt), vp.reshape(kv_prompt), cp,
            jnp.transpose(ks.reshape(kv_sample_t), (0, 1, 4, 2, 3)),
            jnp.transpose(vs.reshape(kv_sample_t), (0, 1, 4, 2, 3)), cs,
            gv.reshape(DEPTH, dec_batch, dec_seq, GMLP_W))
```

```python
import functools

import jax
import jax.numpy as jnp
import numpy as np
from jax import lax
from jax.experimental import pallas as pl
from jax.experimental.pallas import tpu as pltpu

D_MODEL = 1024
DEPTH = 4
HEAD_DIM = 64
N_HEADS = 16
N_KV_HEADS = 4
GQA_GROUP = N_HEADS // N_KV_HEADS
ATTN_W = N_HEADS * HEAD_DIM
KV_W = N_KV_HEADS * HEAD_DIM
WINDOW = 128
BLOCK = 128
CHUNK = 128
GMLP_CH = 128
GMLP_GROUPS = 6
GMLP_W = GMLP_GROUPS * GMLP_CH
D_FF = 2816
CONV_W = 3
EPS = 1e-5
NEG = -1e30

K0, V0, GU0, GV0, GA0, GB0, REST_COLS = 1024, 1280, 1536, 2304, 3072, 4096, 5120

V7X_VMEM_BYTES = 64 * 1024 * 1024
VMEM_LIMIT_BYTES = V7X_VMEM_BYTES - 8 * 1024 * 1024
F32_SUBLANES = 8

TM_PROMPT = 256
TM_PROMPT_FFN = 256
SEQ_PER_STEP_MIXER = 16
SEQ_PER_STEP_FFN = 32
FFN_CHUNK = 256
FFN_LOOKAHEAD = 3
ATTN_LOOKAHEAD = 2
PROJ_TILE = 256

F32 = jnp.float32
BF16 = jnp.bfloat16


def _rmsnorm(x, g):
    return x * lax.rsqrt(jnp.mean(x * x, axis=-1, keepdims=True) + EPS) * g


def _layernorm(x, g, b):
    mu = jnp.mean(x, axis=-1, keepdims=True)
    xc = x - mu
    var = jnp.mean(xc * xc, axis=-1, keepdims=True)
    return xc * lax.rsqrt(var + EPS) * g + b


def _gelu(x):
    c = np.sqrt(2.0 / np.pi).astype(np.float32)
    return x * (0.5 * (1.0 + jnp.tanh(c * (x + 0.044715 * (x * x * x)))))


def _sigmoid(x):
    return 1.0 / (1.0 + jnp.exp(-x))


def _dot(a, b):
    return jnp.dot(a, b, preferred_element_type=F32)


def _dot_nt(a, b):
    return lax.dot_general(a, b, (((1,), (1,)), ((), ())), preferred_element_type=F32)


def _kv_lane_masks(rows):
    lane = lax.broadcasted_iota(jnp.int32, (rows, KV_W), 1)
    return [(lane >= h * HEAD_DIM) & (lane < (h + 1) * HEAD_DIM) for h in range(N_KV_HEADS)]


def _select_kv_lanes(masks, parts):
    out = parts[N_KV_HEADS - 1]
    for h in range(N_KV_HEADS - 2, -1, -1):
        out = jnp.where(masks[h], parts[h], out)
    return out


def _softmax_pv(s, sink, vv):
    m = jnp.maximum(jnp.max(s, axis=-1, keepdims=True), sink)
    p = jnp.exp(s - m)
    denom = jnp.sum(p, axis=-1, keepdims=True) + jnp.exp(sink - m)
    return _dot(p.astype(BF16), vv) / denom


def _gmlp_mix(vn, ws_ref, bs_ref, period):
    vnb = vn.astype(BF16)
    row = lax.broadcasted_iota(jnp.int32, (CHUNK, CHUNK), 0)
    col = lax.broadcasted_iota(jnp.int32, (CHUNK, CHUNK), 1)
    keep = (col <= row) & (col >= row - (row & (period - 1)))
    reps = CHUNK // period
    w, bias = [], []
    for g in range(GMLP_GROUPS):
        if reps == 1:
            wg, bg = ws_ref[g], bs_ref[g]
        else:
            top = ws_ref[g, 0:period, :]
            wg = jnp.concatenate([pltpu.roll(top, i * period, 1) if i else top for i in range(reps)], axis=0)
            bg = jnp.concatenate([bs_ref[g, 0:period, :]] * reps, axis=0)
        w.append(jnp.where(keep, wg, 0.0).astype(BF16))
        bias.append(bg)
    rows = []
    for c in range(vn.shape[0] // CHUNK):
        cols = []
        for g in range(GMLP_GROUPS):
            blk = vnb[c * CHUNK:(c + 1) * CHUNK, g * GMLP_CH:(g + 1) * GMLP_CH]
            cols.append(_dot(w[g], blk) + bias[g])
        rows.append(jnp.concatenate(cols, axis=1))
    return rows[0] if len(rows) == 1 else jnp.concatenate(rows, axis=0)


class _GateProjection:
    def __init__(self, hb, wr_ref):
        self.hb, self.wr_ref, self.tiles = hb, wr_ref, []

    def issue(self, n_tiles):
        for _ in range(n_tiles):
            c0 = GU0 + len(self.tiles) * PROJ_TILE
            if c0 < REST_COLS:
                self.tiles.append(_dot(self.hb, self.wr_ref[:, c0:c0 + PROJ_TILE]))

    def columns(self, c0, c1):
        self.issue((REST_COLS - GU0) // PROJ_TILE)
        return jnp.concatenate(self.tiles[(c0 - GU0) // PROJ_TILE:(c1 - GU0) // PROJ_TILE], axis=1)


def _gate_merge_out(x, o, proj, lng_ref, lnb_ref, ws_ref, bs_ref, wpa_ref, wpb_ref, wout_ref, period):
    u = _gelu(proj.columns(GU0, GV0))
    vn = _layernorm(_gelu(proj.columns(GV0, GA0)), lng_ref[...], lnb_ref[...])
    ga = proj.columns(GA0, GB0)
    gb = proj.columns(GB0, REST_COLS)
    oa = _dot(o.astype(BF16), wpa_ref[...])
    s_gate = u * _gmlp_mix(vn, ws_ref, bs_ref, period)
    merged = _sigmoid(ga) * oa + _sigmoid(gb) * _dot(s_gate.astype(BF16), wpb_ref[...])
    return x + _dot(merged.astype(BF16), wout_ref[...]), vn


def _mixer_prompt_kernel(x_ref, xprev_ref, nm_ref, wq_ref, wr_ref, sink_ref, lng_ref, lnb_ref, ws_ref, bs_ref,
                         wpa_ref, wpb_ref, wout_ref,
                         xo_ref, kt_ref, vt_ref):
    i = pl.program_id(1)
    tm = x_ref.shape[0]
    x = x_ref[...]
    hb = _rmsnorm(x, nm_ref[...]).astype(BF16)

    hb_prev = _rmsnorm(xprev_ref[...], nm_ref[...]).astype(BF16)
    kv_prev = _dot(hb_prev, wr_ref[:, K0:GU0])
    k = _dot(hb, wr_ref[:, K0:V0])
    v = _dot(hb, wr_ref[:, V0:GU0])
    q = _dot(hb, wq_ref[...]) * (HEAD_DIM ** -0.5)
    proj = _GateProjection(hb, wr_ref)

    kk_all = jnp.concatenate([kv_prev[:, 0:KV_W], k], axis=0).astype(BF16)
    vv_all = jnp.concatenate([kv_prev[:, KV_W:2 * KV_W], v], axis=0).astype(BF16)
    kt_ref[...] = k[tm - WINDOW:tm]
    vt_ref[...] = v[tm - WINDOW:tm]

    rows = N_KV_HEADS * BLOCK
    t = lax.broadcasted_iota(jnp.int32, (rows, 2 * BLOCK), 0) & (BLOCK - 1)
    c = lax.broadcasted_iota(jnp.int32, (rows, 2 * BLOCK), 1)
    band = (c > t) & (c <= t + WINDOW)
    bias = jnp.where(band, 0.0, NEG).astype(F32)
    first_key = jnp.where(i > 0, 0, BLOCK)
    bias_first = jnp.where(band & (c >= first_key), 0.0, NEG).astype(F32)
    head_of_row = lax.broadcasted_iota(jnp.int32, (rows, 1), 0) >> (BLOCK.bit_length() - 1)
    masks = _kv_lane_masks(BLOCK)

    items = [(j, g) for j in range(tm // BLOCK) for g in range(GQA_GROUP)]

    def scores(item):
        j, g = item
        qg = q[j * BLOCK:(j + 1) * BLOCK, g * KV_W:(g + 1) * KV_W]
        lhs = jnp.concatenate([jnp.where(masks[h], qg, 0.0) for h in range(N_KV_HEADS)], axis=0).astype(BF16)
        return _dot_nt(lhs, kk_all[j * BLOCK:(j + 2) * BLOCK]) + (bias_first if j == 0 else bias)

    fill = -(-((REST_COLS - GU0) // PROJ_TILE) // len(items))
    o_groups = {}
    ahead = [scores(item) for item in items[:ATTN_LOOKAHEAD]]
    for n, (j, g) in enumerate(items):
        s = ahead.pop(0)
        if n + ATTN_LOOKAHEAD < len(items):
            ahead.append(scores(items[n + ATTN_LOOKAHEAD]))
        proj.issue(fill)
        sink = jnp.full((rows, 1), sink_ref[(N_KV_HEADS - 1) * GQA_GROUP + g], F32)
        for h in range(N_KV_HEADS - 2, -1, -1):
            sink = jnp.where(head_of_row == h, sink_ref[h * GQA_GROUP + g], sink)
        r = _softmax_pv(s, sink, vv_all[j * BLOCK:(j + 2) * BLOCK])
        o_groups[(j, g)] = _select_kv_lanes(masks, [r[h * BLOCK:(h + 1) * BLOCK] for h in range(N_KV_HEADS)])
    o_blocks = [jnp.concatenate([o_groups[(j, g)] for g in range(GQA_GROUP)], axis=1) for j in range(tm // BLOCK)]
    o = o_blocks[0] if len(o_blocks) == 1 else jnp.concatenate(o_blocks, axis=0)

    xo_ref[...], _ = _gate_merge_out(x, o, proj, lng_ref, lnb_ref, ws_ref, bs_ref, wpa_ref, wpb_ref, wout_ref, CHUNK)


def _mixer_sample_kernel(x_ref, nm_ref, wq_ref, wr_ref, sink_ref, lng_ref, lnb_ref, ws_ref, bs_ref,
                         wpa_ref, wpb_ref, wout_ref, ckt_ref, cvt_ref,
                         xo_ref, wkt_ref, wvt_ref, vn_ref):
    n_seq, _, past = ckt_ref.shape
    m_rows = x_ref.shape[0]
    t_new = m_rows // n_seq
    assert m_rows == past and past == BLOCK
    x = x_ref[...]
    hb = _rmsnorm(x, nm_ref[...]).astype(BF16)
    q = _dot(hb, wq_ref[...]) * (HEAD_DIM ** -0.5)
    k = _dot(hb, wr_ref[:, K0:V0])
    v = _dot(hb, wr_ref[:, V0:GU0])
    kt = jnp.transpose(k)
    vt = jnp.transpose(v)
    proj = _GateProjection(hb, wr_ref)

    n_keys = 2 * BLOCK
    rows = N_HEADS * t_new
    t = lax.broadcasted_iota(jnp.int32, (rows, n_keys), 0) & (t_new - 1)
    c = lax.broadcasted_iota(jnp.int32, (rows, n_keys), 1)
    diff = t + past - c
    bias = jnp.where((diff >= 0) & (diff < WINDOW) & (c < past + t_new), 0.0, NEG).astype(F32)
    head_slot = lax.broadcasted_iota(jnp.int32, (rows, 1), 0) >> (t_new.bit_length() - 1)
    sink = jnp.zeros((rows, 1), F32)
    for g in range(GQA_GROUP):
        for h in range(N_KV_HEADS):
            sink = jnp.where(head_slot == g * N_KV_HEADS + h, sink_ref[h * GQA_GROUP + g], sink)
    masks = _kv_lane_masks(t_new)
    zero_rows = jnp.zeros((BLOCK - t_new, KV_W), F32)
    keep_old = lax.broadcasted_iota(jnp.int32, (KV_W, past), 1) < past - t_new

    def new_window(old_t, new_t, b):
        shift = (past - t_new - b * t_new) % past
        placed = pltpu.roll(new_t, shift, 1) if shift else new_t
        return jnp.where(keep_old, pltpu.roll(old_t, past - t_new, 1), placed)

    s_list, v_list = [], []
    for b in range(n_seq):
        kct = ckt_ref[b]
        vct = cvt_ref[b]
        wkt_ref[b] = new_window(kct, kt, b)
        wvt_ref[b] = new_window(vct, vt, b)
        rows_b = slice(b * t_new, (b + 1) * t_new)
        k_pad = jnp.concatenate([k[rows_b], zero_rows], axis=0).astype(BF16)
        v_pad = jnp.concatenate([v[rows_b], zero_rows], axis=0).astype(BF16)
        qb = q[rows_b]
        lhs = jnp.concatenate(
            [jnp.where(masks[h], qb[:, g * KV_W:(g + 1) * KV_W], 0.0)
             for g in range(GQA_GROUP) for h in range(N_KV_HEADS)], axis=0).astype(BF16)
        s_list.append(jnp.concatenate([_dot(lhs, kct.astype(BF16)), _dot_nt(lhs, k_pad)], axis=1) + bias)
        v_list.append((vct.astype(BF16), v_pad))
    proj.issue((REST_COLS - GU0) // PROJ_TILE)
    o_rows = []
    for b in range(n_seq):
        s = s_list[b]
        vct_b, v_pad = v_list[b]
        m = jnp.maximum(jnp.max(s, axis=-1, keepdims=True), sink)
        p = jnp.exp(s - m)
        denom = jnp.sum(p, axis=-1, keepdims=True) + jnp.exp(sink - m)
        pb = p.astype(BF16)
        r = (_dot_nt(pb[:, 0:past], vct_b) + _dot(pb[:, past:n_keys], v_pad)) / denom
        o_groups = []
        for g in range(GQA_GROUP):
            base = g * N_KV_HEADS * t_new
            o_groups.append(_select_kv_lanes(
                masks, [r[base + h * t_new:base + (h + 1) * t_new] for h in range(N_KV_HEADS)]))
        o_rows.append(jnp.concatenate(o_groups, axis=1))
    o = jnp.concatenate(o_rows, axis=0)

    xo_ref[...], vn_ref[...] = _gate_merge_out(x, o, proj, lng_ref, lnb_ref, ws_ref, bs_ref,
                                               wpa_ref, wpb_ref, wout_ref, t_new)


def _conv_gate(za, zb, sa, sb, cw_ref, cb_ref, a0, b0, width):
    def conv(z, shifted, c0):
        z1, z2 = shifted
        cols = slice(c0, c0 + width)
        return cb_ref[:, cols] + ((z2 * cw_ref[0:1, cols] + z1 * cw_ref[1:2, cols]) + z * cw_ref[2:3, cols])

    a = conv(za, sa, a0)
    b = conv(zb, sb, b0)
    c0 = np.sqrt(2.0 / np.pi).astype(np.float32)
    c1 = np.float32(c0 * np.float32(0.044715))
    th = jnp.tanh(a * (c1 * (a * a) + c0))
    half_ab = (0.5 * a) * b
    return half_ab + half_ab * th


def _ffn_body(x, nf_ref, wup_ref, cw_ref, cb_ref, wdn_ref, nfin_ref, shift_fn, final):
    hb = _rmsnorm(x, nf_ref[...]).astype(BF16)

    def up(c):
        a0 = c * FFN_CHUNK
        return _dot(hb, wup_ref[:, a0:a0 + FFN_CHUNK]), _dot(hb, wup_ref[:, D_FF + a0:D_FF + a0 + FFN_CHUNK])

    n_chunks = D_FF // FFN_CHUNK
    acc = x
    ahead = [up(c) for c in range(FFN_LOOKAHEAD)]
    for c in range(n_chunks):
        za, zb = ahead.pop(0)
        if c + FFN_LOOKAHEAD < n_chunks:
            ahead.append(up(c + FFN_LOOKAHEAD))
        a0 = c * FFN_CHUNK
        b0 = D_FF + a0
        gated = _conv_gate(za, zb, shift_fn(za, a0), shift_fn(zb, b0), cw_ref, cb_ref, a0, b0, FFN_CHUNK)
        acc = acc + _dot(gated.astype(BF16), wdn_ref[a0:a0 + FFN_CHUNK, :])
    if final:
        acc = _rmsnorm(acc, nfin_ref[...])
    return acc


def _ffn_prompt_kernel(x_ref, nf_ref, wup_ref, cw_ref, cb_ref, wdn_ref, nfin_ref,
                       xo_ref, ct_ref, carry_ref, *, final):
    tm = x_ref.shape[0]
    sub = carry_ref.shape[0]

    @pl.when(pl.program_id(1) == 0)
    def _():
        carry_ref[...] = jnp.zeros_like(carry_ref)

    def shift_fn(z, c0):
        cols = slice(c0, c0 + z.shape[1])
        ext = jnp.concatenate([carry_ref[:, cols], z], axis=0)
        last = z[tm - sub:tm]
        carry_ref[:, cols] = last
        ct_ref[:, cols] = pltpu.roll(last, CONV_W - 1, 0)[0:CONV_W - 1]
        return pltpu.roll(ext, 1, 0)[sub:], pltpu.roll(ext, 2, 0)[sub:]

    xo_ref[...] = _ffn_body(x_ref[...], nf_ref, wup_ref, cw_ref, cb_ref, wdn_ref, nfin_ref, shift_fn, final)


def _ffn_sample_kernel(x_ref, nf_ref, wup_ref, cw_ref, cb_ref, wdn_ref, nfin_ref, st_ref,
                       xo_ref, ct_ref, *, final):
    n_seq = st_ref.shape[0]
    m_rows = x_ref.shape[0]
    t_new = m_rows // n_seq

    def shift_fn(z, c0):
        width = z.shape[1]
        cols = slice(c0, c0 + width)
        z3 = z.reshape(n_seq, t_new, width)
        st = st_ref[:, :, cols]
        p0 = jnp.broadcast_to(st[:, 0:1, :], z3.shape)
        p1 = jnp.broadcast_to(st[:, 1:2, :], z3.shape)
        t = lax.broadcasted_iota(jnp.int32, z3.shape, 1)
        r1 = pltpu.roll(z3, 1, 1)
        r2 = pltpu.roll(z3, 2, 1)
        ct_ref[:, :, cols] = r2[:, 0:CONV_W - 1, :]
        z1 = jnp.where(t == 0, p1, r1)
        z2 = jnp.where(t == 0, p0, jnp.where(t == 1, p1, r2))
        return z1.reshape(m_rows, width), z2.reshape(m_rows, width)

    xo_ref[...] = _ffn_body(x_ref[...], nf_ref, wup_ref, cw_ref, cb_ref, wdn_ref, nfin_ref, shift_fn, final)


def _drop_aliased(kernel_fn, n_in, n_aliased):
    def wrapped(*refs):
        return kernel_fn(*refs[:n_in], *refs[n_in + n_aliased:])
    return wrapped


def _layer_spec(layer, shape):
    nd = len(shape)
    return pl.BlockSpec((None,) + tuple(shape), lambda *_: (layer,) + (0,) * nd, pipeline_mode=pl.Buffered(1))


_SMEM_SPEC = pl.BlockSpec(memory_space=pltpu.SMEM)
_ANY_SPEC = pl.BlockSpec(memory_space=pl.ANY)


def _compiler_params(n_axes):
    return pltpu.CompilerParams(dimension_semantics=("arbitrary",) * n_axes,
                                vmem_limit_bytes=VMEM_LIMIT_BYTES)


def _mixer_weight_specs(layer):
    return [
        _layer_spec(layer, (1, D_MODEL)),
        _layer_spec(layer, (D_MODEL, ATTN_W)),
        _layer_spec(layer, (D_MODEL, REST_COLS)),
        _SMEM_SPEC,
        _layer_spec(layer, (1, GMLP_W)),
        _layer_spec(layer, (1, GMLP_W)),
        _layer_spec(layer, (GMLP_GROUPS, CHUNK, CHUNK)),
        _layer_spec(layer, (GMLP_GROUPS, CHUNK, 1)),
        _layer_spec(layer, (ATTN_W, D_MODEL)),
        _layer_spec(layer, (GMLP_W, D_MODEL)),
        _layer_spec(layer, (D_MODEL, D_MODEL)),
    ]


def _ffn_weight_specs(layer):
    return [
        _layer_spec(layer, (1, D_MODEL)),
        _layer_spec(layer, (D_MODEL, 2 * D_FF)),
        _layer_spec(layer, (CONV_W, 2 * D_FF)),
        _layer_spec(layer, (1, 2 * D_FF)),
        _layer_spec(layer, (D_FF, D_MODEL)),
        pl.BlockSpec((1, D_MODEL), lambda *_: (0, 0), pipeline_mode=pl.Buffered(1)),
    ]


def _call_layer(kernel_fn, grid, inputs, in_specs, out_specs, out_shapes, carried, scratch, name):
    n_in = len(inputs)
    first_carried = len(out_shapes) - len(carried)
    aliases = {n_in + n: first_carried + n for n in range(len(carried))}
    return pl.pallas_call(
        _drop_aliased(kernel_fn, n_in, len(carried)), grid=grid,
        in_specs=list(in_specs) + [_ANY_SPEC] * len(carried), out_specs=out_specs, out_shape=out_shapes,
        scratch_shapes=scratch, input_output_aliases=aliases,
        compiler_params=_compiler_params(len(grid)), name=name,
    )(*inputs, *carried)


def _mixer_prompt(layer, x, mixer_w, k_tails, v_tails):
    batch, seq, _ = x.shape
    tm = TM_PROMPT
    tile = pl.BlockSpec((None, tm, D_MODEL), lambda b, i: (b, i, 0))
    prev_block = pl.BlockSpec((None, BLOCK, D_MODEL), lambda b, i: (b, jnp.maximum(i * (tm // BLOCK) - 1, 0), 0))
    tail = pl.BlockSpec((None, None, WINDOW, KV_W), lambda b, i: (layer, b, 0, 0))
    tail_shape = jax.ShapeDtypeStruct(k_tails.shape, F32)
    return _call_layer(
        _mixer_prompt_kernel, (batch, seq // tm), (x, x) + mixer_w,
        [tile, prev_block] + _mixer_weight_specs(layer), [tile, tail, tail],
        [jax.ShapeDtypeStruct(x.shape, F32), tail_shape, tail_shape], [k_tails, v_tails], [], "mixer_prompt")


def _mixer_sample(layer, x, mixer_w, cache_k, cache_v, win_k, win_v, vn_all):
    n_rows = x.shape[0]
    _, n_seq_total, _, past = cache_k.shape
    t_new = n_rows // n_seq_total
    ns = SEQ_PER_STEP_MIXER
    m = ns * t_new
    tile = pl.BlockSpec((m, D_MODEL), lambda i: (i, 0))
    win = pl.BlockSpec((None, ns, KV_W, past), lambda i: (layer, i, 0, 0))
    vn_spec = pl.BlockSpec((None, m, GMLP_W), lambda i: (layer, i, 0))
    return _call_layer(
        _mixer_sample_kernel, (n_seq_total // ns,), (x,) + mixer_w + (cache_k, cache_v),
        [tile] + _mixer_weight_specs(layer) + [win, win], [tile, win, win, vn_spec],
        [jax.ShapeDtypeStruct(x.shape, F32), jax.ShapeDtypeStruct(win_k.shape, F32),
         jax.ShapeDtypeStruct(win_v.shape, F32), jax.ShapeDtypeStruct(vn_all.shape, F32)],
        [win_k, win_v, vn_all], [], "mixer_sample")


def _ffn_prompt(layer, x, ffn_w, conv_tails):
    batch, seq, _ = x.shape
    tm = TM_PROMPT_FFN
    tile = pl.BlockSpec((None, tm, D_MODEL), lambda b, i: (b, i, 0))
    tail = pl.BlockSpec((None, None, CONV_W - 1, 2 * D_FF), lambda b, i: (layer, b, 0, 0))
    return _call_layer(
        functools.partial(_ffn_prompt_kernel, final=layer == DEPTH - 1), (batch, seq // tm), (x,) + ffn_w,
        [tile] + _ffn_weight_specs(layer), [tile, tail],
        [jax.ShapeDtypeStruct(x.shape, F32), jax.ShapeDtypeStruct(conv_tails.shape, F32)],
        [conv_tails], [pltpu.VMEM((F32_SUBLANES, 2 * D_FF), F32)], "ffn_prompt")


def _ffn_sample(layer, x, ffn_w, state, new_state):
    n_rows = x.shape[0]
    n_seq_total = state.shape[1]
    t_new = n_rows // n_seq_total
    ns = SEQ_PER_STEP_FFN
    m = ns * t_new
    tile = pl.BlockSpec((m, D_MODEL), lambda i: (i, 0))
    st = pl.BlockSpec((None, ns, CONV_W - 1, 2 * D_FF), lambda i: (layer, i, 0, 0))
    return _call_layer(
        functools.partial(_ffn_sample_kernel, final=layer == DEPTH - 1), (n_seq_total // ns,),
        (x,) + ffn_w + (state,), [tile] + _ffn_weight_specs(layer) + [st], [tile, st],
        [jax.ShapeDtypeStruct(x.shape, F32), jax.ShapeDtypeStruct(state.shape, F32)],
        [new_state], [], "ffn_sample")


def _heads_group_major(w, axis):
    shape = w.shape
    split = shape[:axis] + (N_KV_HEADS, GQA_GROUP, HEAD_DIM) + shape[axis + 1:]
    return jnp.swapaxes(w.reshape(split), axis, axis + 1).reshape(shape)


def kernel(x_prompt, x_sample, cache_win_k, cache_win_v, state_conv, norm_mix, w_in, sinks, gmlp_ln_g, gmlp_ln_b, gmlp_ws, gmlp_bs, w_branch_attn, w_branch_gmlp, w_out, norm_ffn, w_up, conv_w, conv_b, w_down, norm_final):
    batch = x_prompt.shape[0]
    dec_batch, dec_seq, _ = x_sample.shape
    w_in_b = w_in.astype(BF16)
    w_q_b = _heads_group_major(w_in_b[:, :, :ATTN_W], 2)
    w_pa_b = _heads_group_major(w_branch_attn, 1).astype(BF16)
    mixer_shared = (norm_mix.reshape(DEPTH, 1, D_MODEL), w_q_b, w_in_b)
    mixer_ln = (gmlp_ln_g.reshape(DEPTH, 1, GMLP_W), gmlp_ln_b.reshape(DEPTH, 1, GMLP_W))
    mixer_tail = (w_pa_b, w_branch_gmlp.astype(BF16), w_out.astype(BF16))
    gmlp_w = (gmlp_ws, gmlp_bs.reshape(DEPTH, GMLP_GROUPS, CHUNK, 1))
    ffn_w = (norm_ffn.reshape(DEPTH, 1, D_MODEL), w_up.astype(BF16), conv_w, conv_b.reshape(DEPTH, 1, 2 * D_FF),
             w_down.astype(BF16), norm_final.reshape(1, D_MODEL))
    cache_k = jnp.transpose(cache_win_k, (0, 1, 3, 4, 2)).reshape(DEPTH, dec_batch, KV_W, WINDOW)
    cache_v = jnp.transpose(cache_win_v, (0, 1, 3, 4, 2)).reshape(DEPTH, dec_batch, KV_W, WINDOW)

    xp = x_prompt
    xs = x_sample.reshape(dec_batch * dec_seq, D_MODEL)
    kp = jnp.zeros((DEPTH, batch, WINDOW, KV_W), F32)
    vp = jnp.zeros((DEPTH, batch, WINDOW, KV_W), F32)
    cp = jnp.zeros((DEPTH, batch, CONV_W - 1, 2 * D_FF), F32)
    gv = jnp.zeros((DEPTH, dec_batch * dec_seq, GMLP_W), F32)
    ks = jnp.zeros(cache_k.shape, F32)
    vs = jnp.zeros(cache_v.shape, F32)
    cs = jnp.zeros(state_conv.shape, F32)
    for l in range(DEPTH):
        sink_l = sinks[l]
        xp, kp, vp = _mixer_prompt(l, xp, mixer_shared + (sink_l,) + mixer_ln + gmlp_w + mixer_tail, kp, vp)
        xp, cp = _ffn_prompt(l, xp, ffn_w, cp)
        xs, ks, vs, gv = _mixer_sample(l, xs, mixer_shared + (sink_l,) + mixer_ln + gmlp_w + mixer_tail,
                                       cache_k, cache_v, ks, vs, gv)
        xs, cs = _ffn_sample(l, xs, ffn_w, state_conv, cs)

    kv_prompt = (DEPTH, batch, WINDOW, N_KV_HEADS, HEAD_DIM)
    kv_sample_t = (DEPTH, dec_batch, N_KV_HEADS, HEAD_DIM, WINDOW)
    return (xp, xs.reshape(x_sample.shape),
            kp.reshape(kv_prompt), vp.reshape(kv_prompt), cp,
            jnp.transpose(ks.reshape(kv_sample_t), (0, 1, 4, 2, 3)),
            jnp.transpose(vs.reshape(kv_sample_t), (0, 1, 4, 2, 3)), cs,
            gv.reshape(DEPTH, dec_batch, dec_seq, GMLP_W))
```

```python
import functools

import jax
import jax.numpy as jnp
import numpy as np
from jax import lax
from jax.experimental import pallas as pl
from jax.experimental.pallas import tpu as pltpu

D_MODEL = 1024
DEPTH = 4
HEAD_DIM = 64
N_HEADS = 16
N_KV_HEADS = 4
GQA_GROUP = N_HEADS // N_KV_HEADS
ATTN_W = N_HEADS * HEAD_DIM
KV_W = N_KV_HEADS * HEAD_DIM
WINDOW = 128
BLOCK = 128
CHUNK = 128
GMLP_CH = 128
GMLP_GROUPS = 6
GMLP_W = GMLP_GROUPS * GMLP_CH
D_FF = 2816
CONV_W = 3
EPS = 1e-5
NEG = -1e30

K0, V0, GU0, GV0, GA0, GB0, REST_COLS = 1024, 1280, 1536, 2304, 3072, 4096, 5120

V7X_VMEM_BYTES = 64 * 1024 * 1024
VMEM_LIMIT_BYTES = V7X_VMEM_BYTES - 8 * 1024 * 1024
F32_SUBLANES = 8

TM_PROMPT = 512
TM_PROMPT_FFN = 512
FFN_SUBTILE = 256
SEQ_PER_STEP_MIXER = 16
SEQ_PER_STEP_FFN = 32
FFN_CHUNK = 256
FFN_LOOKAHEAD = 3
ATTN_LOOKAHEAD = 2
PROJ_TILE = 256

F32 = jnp.float32
BF16 = jnp.bfloat16


def _rmsnorm(x, g):
    return x * lax.rsqrt(jnp.mean(x * x, axis=-1, keepdims=True) + EPS) * g


def _layernorm(x, g, b):
    mu = jnp.mean(x, axis=-1, keepdims=True)
    xc = x - mu
    var = jnp.mean(xc * xc, axis=-1, keepdims=True)
    return xc * lax.rsqrt(var + EPS) * g + b


def _gelu(x):
    c = np.sqrt(2.0 / np.pi).astype(np.float32)
    return x * (0.5 * (1.0 + jnp.tanh(c * (x + 0.044715 * (x * x * x)))))


def _sigmoid(x):
    return 1.0 / (1.0 + jnp.exp(-x))


def _dot(a, b):
    return jnp.dot(a, b, preferred_element_type=F32)


def _dot_nt(a, b):
    return lax.dot_general(a, b, (((1,), (1,)), ((), ())), preferred_element_type=F32)


def _kv_lane_masks(rows):
    lane = lax.broadcasted_iota(jnp.int32, (rows, KV_W), 1)
    return [(lane >= h * HEAD_DIM) & (lane < (h + 1) * HEAD_DIM) for h in range(N_KV_HEADS)]


def _select_kv_lanes(masks, parts):
    out = parts[N_KV_HEADS - 1]
    for h in range(N_KV_HEADS - 2, -1, -1):
        out = jnp.where(masks[h], parts[h], out)
    return out


def _softmax_pv(s, sink, vv):
    m = jnp.maximum(jnp.max(s, axis=-1, keepdims=True), sink)
    p = jnp.exp(s - m)
    denom = jnp.sum(p, axis=-1, keepdims=True) + jnp.exp(sink - m)
    return _dot(p.astype(BF16), vv) / denom


def _gmlp_mix(vn, ws_ref, bs_ref, period):
    vnb = vn.astype(BF16)
    row = lax.broadcasted_iota(jnp.int32, (CHUNK, CHUNK), 0)
    col = lax.broadcasted_iota(jnp.int32, (CHUNK, CHUNK), 1)
    keep = (col <= row) & (col >= row - (row & (period - 1)))
    reps = CHUNK // period
    w, bias = [], []
    for g in range(GMLP_GROUPS):
        if reps == 1:
            wg, bg = ws_ref[g], bs_ref[g]
        else:
            top = ws_ref[g, 0:period, :]
            wg = jnp.concatenate([pltpu.roll(top, i * period, 1) if i else top for i in range(reps)], axis=0)
            bg = jnp.concatenate([bs_ref[g, 0:period, :]] * reps, axis=0)
        w.append(jnp.where(keep, wg, 0.0).astype(BF16))
        bias.append(bg)
    rows = []
    for c in range(vn.shape[0] // CHUNK):
        cols = []
        for g in range(GMLP_GROUPS):
            blk = vnb[c * CHUNK:(c + 1) * CHUNK, g * GMLP_CH:(g + 1) * GMLP_CH]
            cols.append(_dot(w[g], blk) + bias[g])
        rows.append(jnp.concatenate(cols, axis=1))
    return rows[0] if len(rows) == 1 else jnp.concatenate(rows, axis=0)


class _GateProjection:
    def __init__(self, hb, wr_ref):
        self.hb, self.wr_ref, self.tiles = hb, wr_ref, []

    def issue(self, n_tiles):
        for _ in range(n_tiles):
            c0 = GU0 + len(self.tiles) * PROJ_TILE
            if c0 < REST_COLS:
                self.tiles.append(_dot(self.hb, self.wr_ref[:, c0:c0 + PROJ_TILE]))

    def columns(self, c0, c1):
        self.issue((REST_COLS - GU0) // PROJ_TILE)
        return jnp.concatenate(self.tiles[(c0 - GU0) // PROJ_TILE:(c1 - GU0) // PROJ_TILE], axis=1)


def _gate_merge_out(x, o, proj, lng_ref, lnb_ref, ws_ref, bs_ref, wpa_ref, wpb_ref, wout_ref, period):
    u = _gelu(proj.columns(GU0, GV0))
    vn = _layernorm(_gelu(proj.columns(GV0, GA0)), lng_ref[...], lnb_ref[...])
    ga = proj.columns(GA0, GB0)
    gb = proj.columns(GB0, REST_COLS)
    oa = _dot(o.astype(BF16), wpa_ref[...])
    s_gate = u * _gmlp_mix(vn, ws_ref, bs_ref, period)
    merged = _sigmoid(ga) * oa + _sigmoid(gb) * _dot(s_gate.astype(BF16), wpb_ref[...])
    return x + _dot(merged.astype(BF16), wout_ref[...]), vn


def _mixer_prompt_kernel(x_ref, xprev_ref, nm_ref, wq_ref, wr_ref, sink_ref, lng_ref, lnb_ref, ws_ref, bs_ref,
                         wpa_ref, wpb_ref, wout_ref,
                         xo_ref, kt_ref, vt_ref):
    i = pl.program_id(1)
    tm = x_ref.shape[0]
    x = x_ref[...]
    hb = _rmsnorm(x, nm_ref[...]).astype(BF16)

    hb_prev = _rmsnorm(xprev_ref[...], nm_ref[...]).astype(BF16)
    kv_prev = _dot(hb_prev, wr_ref[:, K0:GU0])
    k = _dot(hb, wr_ref[:, K0:V0])
    v = _dot(hb, wr_ref[:, V0:GU0])
    q = _dot(hb, wq_ref[...]) * (HEAD_DIM ** -0.5)
    proj = _GateProjection(hb, wr_ref)

    kk_all = jnp.concatenate([kv_prev[:, 0:KV_W], k], axis=0).astype(BF16)
    vv_all = jnp.concatenate([kv_prev[:, KV_W:2 * KV_W], v], axis=0).astype(BF16)
    kt_ref[...] = k[tm - WINDOW:tm]
    vt_ref[...] = v[tm - WINDOW:tm]

    rows = N_KV_HEADS * BLOCK
    t = lax.broadcasted_iota(jnp.int32, (rows, 2 * BLOCK), 0) & (BLOCK - 1)
    c = lax.broadcasted_iota(jnp.int32, (rows, 2 * BLOCK), 1)
    band = (c > t) & (c <= t + WINDOW)
    bias = jnp.where(band, 0.0, NEG).astype(F32)
    first_key = jnp.where(i > 0, 0, BLOCK)
    bias_first = jnp.where(band & (c >= first_key), 0.0, NEG).astype(F32)
    head_of_row = lax.broadcasted_iota(jnp.int32, (rows, 1), 0) >> (BLOCK.bit_length() - 1)
    masks = _kv_lane_masks(BLOCK)

    items = [(j, g) for j in range(tm // BLOCK) for g in range(GQA_GROUP)]

    def scores(item):
        j, g = item
        qg = q[j * BLOCK:(j + 1) * BLOCK, g * KV_W:(g + 1) * KV_W]
        lhs = jnp.concatenate([jnp.where(masks[h], qg, 0.0) for h in range(N_KV_HEADS)], axis=0).astype(BF16)
        return _dot_nt(lhs, kk_all[j * BLOCK:(j + 2) * BLOCK]) + (bias_first if j == 0 else bias)

    fill = -(-((REST_COLS - GU0) // PROJ_TILE) // len(items))
    o_groups = {}
    ahead = [scores(item) for item in items[:ATTN_LOOKAHEAD]]
    for n, (j, g) in enumerate(items):
        s = ahead.pop(0)
        if n + ATTN_LOOKAHEAD < len(items):
            ahead.append(scores(items[n + ATTN_LOOKAHEAD]))
        proj.issue(fill)
        sink = jnp.full((rows, 1), sink_ref[(N_KV_HEADS - 1) * GQA_GROUP + g], F32)
        for h in range(N_KV_HEADS - 2, -1, -1):
            sink = jnp.where(head_of_row == h, sink_ref[h * GQA_GROUP + g], sink)
        r = _softmax_pv(s, sink, vv_all[j * BLOCK:(j + 2) * BLOCK])
        o_groups[(j, g)] = _select_kv_lanes(masks, [r[h * BLOCK:(h + 1) * BLOCK] for h in range(N_KV_HEADS)])
    o_blocks = [jnp.concatenate([o_groups[(j, g)] for g in range(GQA_GROUP)], axis=1) for j in range(tm // BLOCK)]
    o = o_blocks[0] if len(o_blocks) == 1 else jnp.concatenate(o_blocks, axis=0)

    xo_ref[...], _ = _gate_merge_out(x, o, proj, lng_ref, lnb_ref, ws_ref, bs_ref, wpa_ref, wpb_ref, wout_ref, CHUNK)


def _mixer_sample_kernel(x_ref, nm_ref, wq_ref, wr_ref, sink_ref, lng_ref, lnb_ref, ws_ref, bs_ref,
                         wpa_ref, wpb_ref, wout_ref, ckt_ref, cvt_ref,
                         xo_ref, wkt_ref, wvt_ref, vn_ref):
    n_seq, _, past = ckt_ref.shape
    m_rows = x_ref.shape[0]
    t_new = m_rows // n_seq
    assert m_rows == past and past == BLOCK
    x = x_ref[...]
    hb = _rmsnorm(x, nm_ref[...]).astype(BF16)
    q = _dot(hb, wq_ref[...]) * (HEAD_DIM ** -0.5)
    k = _dot(hb, wr_ref[:, K0:V0])
    v = _dot(hb, wr_ref[:, V0:GU0])
    kt = jnp.transpose(k)
    vt = jnp.transpose(v)
    proj = _GateProjection(hb, wr_ref)

    n_keys = 2 * BLOCK
    rows = N_HEADS * t_new
    t = lax.broadcasted_iota(jnp.int32, (rows, n_keys), 0) & (t_new - 1)
    c = lax.broadcasted_iota(jnp.int32, (rows, n_keys), 1)
    diff = t + past - c
    bias = jnp.where((diff >= 0) & (diff < WINDOW) & (c < past + t_new), 0.0, NEG).astype(F32)
    head_slot = lax.broadcasted_iota(jnp.int32, (rows, 1), 0) >> (t_new.bit_length() - 1)
    sink = jnp.zeros((rows, 1), F32)
    for g in range(GQA_GROUP):
        for h in range(N_KV_HEADS):
            sink = jnp.where(head_slot == g * N_KV_HEADS + h, sink_ref[h * GQA_GROUP + g], sink)
    masks = _kv_lane_masks(t_new)
    zero_rows = jnp.zeros((BLOCK - t_new, KV_W), F32)
    keep_old = lax.broadcasted_iota(jnp.int32, (KV_W, past), 1) < past - t_new

    def new_window(old_t, new_t, b):
        shift = (past - t_new - b * t_new) % past
        placed = pltpu.roll(new_t, shift, 1) if shift else new_t
        return jnp.where(keep_old, pltpu.roll(old_t, past - t_new, 1), placed)

    s_list, v_list = [], []
    for b in range(n_seq):
        kct = ckt_ref[b]
        vct = cvt_ref[b]
        wkt_ref[b] = new_window(kct, kt, b)
        wvt_ref[b] = new_window(vct, vt, b)
        rows_b = slice(b * t_new, (b + 1) * t_new)
        k_pad = jnp.concatenate([k[rows_b], zero_rows], axis=0).astype(BF16)
        v_pad = jnp.concatenate([v[rows_b], zero_rows], axis=0).astype(BF16)
        qb = q[rows_b]
        lhs = jnp.concatenate(
            [jnp.where(masks[h], qb[:, g * KV_W:(g + 1) * KV_W], 0.0)
             for g in range(GQA_GROUP) for h in range(N_KV_HEADS)], axis=0).astype(BF16)
        s_list.append(jnp.concatenate([_dot(lhs, kct.astype(BF16)), _dot_nt(lhs, k_pad)], axis=1) + bias)
        v_list.append((vct.astype(BF16), v_pad))
    proj.issue((REST_COLS - GU0) // PROJ_TILE)
    o_rows = []
    for b in range(n_seq):
        s = s_list[b]
        vct_b, v_pad = v_list[b]
        m = jnp.maximum(jnp.max(s, axis=-1, keepdims=True), sink)
        p = jnp.exp(s - m)
        denom = jnp.sum(p, axis=-1, keepdims=True) + jnp.exp(sink - m)
        pb = p.astype(BF16)
        r = (_dot_nt(pb[:, 0:past], vct_b) + _dot(pb[:, past:n_keys], v_pad)) / denom
        o_groups = []
        for g in range(GQA_GROUP):
            base = g * N_KV_HEADS * t_new
            o_groups.append(_select_kv_lanes(
                masks, [r[base + h * t_new:base + (h + 1) * t_new] for h in range(N_KV_HEADS)]))
        o_rows.append(jnp.concatenate(o_groups, axis=1))
    o = jnp.concatenate(o_rows, axis=0)

    xo_ref[...], vn_ref[...] = _gate_merge_out(x, o, proj, lng_ref, lnb_ref, ws_ref, bs_ref,
                                               wpa_ref, wpb_ref, wout_ref, t_new)


def _conv_gate(za, zb, sa, sb, cw_ref, cb_ref, a0, b0, width):
    def conv(z, shifted, c0):
        z1, z2 = shifted
        cols = slice(c0, c0 + width)
        return cb_ref[:, cols] + ((z2 * cw_ref[0:1, cols] + z1 * cw_ref[1:2, cols]) + z * cw_ref[2:3, cols])

    a = conv(za, sa, a0)
    b = conv(zb, sb, b0)
    c0 = np.sqrt(2.0 / np.pi).astype(np.float32)
    c1 = np.float32(c0 * np.float32(0.044715))
    th = jnp.tanh(a * (c1 * (a * a) + c0))
    half_ab = (0.5 * a) * b
    return half_ab + half_ab * th


def _ffn_body(x, nf_ref, wup_ref, cw_ref, cb_ref, wdn_ref, nfin_ref, shift_fn, final):
    hb = _rmsnorm(x, nf_ref[...]).astype(BF16)

    def up(c):
        a0 = c * FFN_CHUNK
        return _dot(hb, wup_ref[:, a0:a0 + FFN_CHUNK]), _dot(hb, wup_ref[:, D_FF + a0:D_FF + a0 + FFN_CHUNK])

    n_chunks = D_FF // FFN_CHUNK
    acc = x
    ahead = [up(c) for c in range(FFN_LOOKAHEAD)]
    for c in range(n_chunks):
        za, zb = ahead.pop(0)
        if c + FFN_LOOKAHEAD < n_chunks:
            ahead.append(up(c + FFN_LOOKAHEAD))
        a0 = c * FFN_CHUNK
        b0 = D_FF + a0
        gated = _conv_gate(za, zb, shift_fn(za, a0), shift_fn(zb, b0), cw_ref, cb_ref, a0, b0, FFN_CHUNK)
        acc = acc + _dot(gated.astype(BF16), wdn_ref[a0:a0 + FFN_CHUNK, :])
    if final:
        acc = _rmsnorm(acc, nfin_ref[...])
    return acc


def _ffn_prompt_kernel(x_ref, nf_ref, wup_ref, cw_ref, cb_ref, wdn_ref, nfin_ref,
                       xo_ref, ct_ref, carry_ref, *, final):
    sub = carry_ref.shape[0]

    @pl.when(pl.program_id(1) == 0)
    def _():
        carry_ref[...] = jnp.zeros_like(carry_ref)

    def shift_fn(z, c0):
        rows = z.shape[0]
        cols = slice(c0, c0 + z.shape[1])
        ext = jnp.concatenate([carry_ref[:, cols], z], axis=0)
        last = z[rows - sub:rows]
        carry_ref[:, cols] = last
        ct_ref[:, cols] = pltpu.roll(last, CONV_W - 1, 0)[0:CONV_W - 1]
        return pltpu.roll(ext, 1, 0)[sub:], pltpu.roll(ext, 2, 0)[sub:]

    def sub_tile(n, carry):
        r0 = pl.multiple_of(n * FFN_SUBTILE, FFN_SUBTILE)
        xo_ref[pl.ds(r0, FFN_SUBTILE), :] = _ffn_body(
            x_ref[pl.ds(r0, FFN_SUBTILE), :], nf_ref, wup_ref, cw_ref, cb_ref, wdn_ref, nfin_ref, shift_fn, final)
        return carry

    lax.fori_loop(0, x_ref.shape[0] // FFN_SUBTILE, sub_tile, 0)


def _ffn_sample_kernel(x_ref, nf_ref, wup_ref, cw_ref, cb_ref, wdn_ref, nfin_ref, st_ref,
                       xo_ref, ct_ref, *, final):
    n_seq = st_ref.shape[0]
    m_rows = x_ref.shape[0]
    t_new = m_rows // n_seq

    def shift_fn(z, c0):
        width = z.shape[1]
        cols = slice(c0, c0 + width)
        z3 = z.reshape(n_seq, t_new, width)
        st = st_ref[:, :, cols]
        p0 = jnp.broadcast_to(st[:, 0:1, :], z3.shape)
        p1 = jnp.broadcast_to(st[:, 1:2, :], z3.shape)
        t = lax.broadcasted_iota(jnp.int32, z3.shape, 1)
        r1 = pltpu.roll(z3, 1, 1)
        r2 = pltpu.roll(z3, 2, 1)
        ct_ref[:, :, cols] = r2[:, 0:CONV_W - 1, :]
        z1 = jnp.where(t == 0, p1, r1)
        z2 = jnp.where(t == 0, p0, jnp.where(t == 1, p1, r2))
        return z1.reshape(m_rows, width), z2.reshape(m_rows, width)

    xo_ref[...] = _ffn_body(x_ref[...], nf_ref, wup_ref, cw_ref, cb_ref, wdn_ref, nfin_ref, shift_fn, final)


def _drop_aliased(kernel_fn, n_in, n_aliased):
    def wrapped(*refs):
        return kernel_fn(*refs[:n_in], *refs[n_in + n_aliased:])
    return wrapped


def _layer_spec(layer, shape):
    nd = len(shape)
    return pl.BlockSpec((None,) + tuple(shape), lambda *_: (layer,) + (0,) * nd, pipeline_mode=pl.Buffered(1))


_SMEM_SPEC = pl.BlockSpec(memory_space=pltpu.SMEM)
_ANY_SPEC = pl.BlockSpec(memory_space=pl.ANY)


def _compiler_params(n_axes):
    return pltpu.CompilerParams(dimension_semantics=("arbitrary",) * n_axes,
                                vmem_limit_bytes=VMEM_LIMIT_BYTES)


def _mixer_weight_specs(layer):
    return [
        _layer_spec(layer, (1, D_MODEL)),
        _layer_spec(layer, (D_MODEL, ATTN_W)),
        _layer_spec(layer, (D_MODEL, REST_COLS)),
        _SMEM_SPEC,
        _layer_spec(layer, (1, GMLP_W)),
        _layer_spec(layer, (1, GMLP_W)),
        _layer_spec(layer, (GMLP_GROUPS, CHUNK, CHUNK)),
        _layer_spec(layer, (GMLP_GROUPS, CHUNK, 1)),
        _layer_spec(layer, (ATTN_W, D_MODEL)),
        _layer_spec(layer, (GMLP_W, D_MODEL)),
        _layer_spec(layer, (D_MODEL, D_MODEL)),
    ]


def _ffn_weight_specs(layer):
    return [
        _layer_spec(layer, (1, D_MODEL)),
        _layer_spec(layer, (D_MODEL, 2 * D_FF)),
        _layer_spec(layer, (CONV_W, 2 * D_FF)),
        _layer_spec(layer, (1, 2 * D_FF)),
        _layer_spec(layer, (D_FF, D_MODEL)),
        pl.BlockSpec((1, D_MODEL), lambda *_: (0, 0), pipeline_mode=pl.Buffered(1)),
    ]


def _call_layer(kernel_fn, grid, inputs, in_specs, out_specs, out_shapes, carried, scratch, name):
    n_in = len(inputs)
    first_carried = len(out_shapes) - len(carried)
    aliases = {n_in + n: first_carried + n for n in range(len(carried))}
    return pl.pallas_call(
        _drop_aliased(kernel_fn, n_in, len(carried)), grid=grid,
        in_specs=list(in_specs) + [_ANY_SPEC] * len(carried), out_specs=out_specs, out_shape=out_shapes,
        scratch_shapes=scratch, input_output_aliases=aliases,
        compiler_params=_compiler_params(len(grid)), name=name,
    )(*inputs, *carried)


def _mixer_prompt(layer, x, mixer_w, k_tails, v_tails):
    batch, seq, _ = x.shape
    tm = TM_PROMPT
    tile = pl.BlockSpec((None, tm, D_MODEL), lambda b, i: (b, i, 0))
    prev_block = pl.BlockSpec((None, BLOCK, D_MODEL), lambda b, i: (b, jnp.maximum(i * (tm // BLOCK) - 1, 0), 0))
    tail = pl.BlockSpec((None, None, WINDOW, KV_W), lambda b, i: (layer, b, 0, 0))
    tail_shape = jax.ShapeDtypeStruct(k_tails.shape, F32)
    return _call_layer(
        _mixer_prompt_kernel, (batch, seq // tm), (x, x) + mixer_w,
        [tile, prev_block] + _mixer_weight_specs(layer), [tile, tail, tail],
        [jax.ShapeDtypeStruct(x.shape, F32), tail_shape, tail_shape], [k_tails, v_tails], [], "mixer_prompt")


def _mixer_sample(layer, x, mixer_w, cache_k, cache_v, win_k, win_v, vn_all):
    n_rows = x.shape[0]
    _, n_seq_total, _, past = cache_k.shape
    t_new = n_rows // n_seq_total
    ns = SEQ_PER_STEP_MIXER
    m = ns * t_new
    tile = pl.BlockSpec((m, D_MODEL), lambda i: (i, 0))
    win = pl.BlockSpec((None, ns, KV_W, past), lambda i: (layer, i, 0, 0))
    vn_spec = pl.BlockSpec((None, m, GMLP_W), lambda i: (layer, i, 0))
    return _call_layer(
        _mixer_sample_kernel, (n_seq_total // ns,), (x,) + mixer_w + (cache_k, cache_v),
        [tile] + _mixer_weight_specs(layer) + [win, win], [tile, win, win, vn_spec],
        [jax.ShapeDtypeStruct(x.shape, F32), jax.ShapeDtypeStruct(win_k.shape, F32),
         jax.ShapeDtypeStruct(win_v.shape, F32), jax.ShapeDtypeStruct(vn_all.shape, F32)],
        [win_k, win_v, vn_all], [], "mixer_sample")


def _ffn_prompt(layer, x, ffn_w, conv_tails):
    batch, seq, _ = x.shape
    tm = TM_PROMPT_FFN
    tile = pl.BlockSpec((None, tm, D_MODEL), lambda b, i: (b, i, 0))
    tail = pl.BlockSpec((None, None, CONV_W - 1, 2 * D_FF), lambda b, i: (layer, b, 0, 0))
    return _call_layer(
        functools.partial(_ffn_prompt_kernel, final=layer == DEPTH - 1), (batch, seq // tm), (x,) + ffn_w,
        [tile] + _ffn_weight_specs(layer), [tile, tail],
        [jax.ShapeDtypeStruct(x.shape, F32), jax.ShapeDtypeStruct(conv_tails.shape, F32)],
        [conv_tails], [pltpu.VMEM((F32_SUBLANES, 2 * D_FF), F32)], "ffn_prompt")


def _ffn_sample(layer, x, ffn_w, state, new_state):
    n_rows = x.shape[0]
    n_seq_total = state.shape[1]
    t_new = n_rows // n_seq_total
    ns = SEQ_PER_STEP_FFN
    m = ns * t_new
    tile = pl.BlockSpec((m, D_MODEL), lambda i: (i, 0))
    st = pl.BlockSpec((None, ns, CONV_W - 1, 2 * D_FF), lambda i: (layer, i, 0, 0))
    return _call_layer(
        functools.partial(_ffn_sample_kernel, final=layer == DEPTH - 1), (n_seq_total // ns,),
        (x,) + ffn_w + (state,), [tile] + _ffn_weight_specs(layer) + [st], [tile, st],
        [jax.ShapeDtypeStruct(x.shape, F32), jax.ShapeDtypeStruct(state.shape, F32)],
        [new_state], [], "ffn_sample")


def _heads_group_major(w, axis):
    shape = w.shape
    split = shape[:axis] + (N_KV_HEADS, GQA_GROUP, HEAD_DIM) + shape[axis + 1:]
    return jnp.swapaxes(w.reshape(split), axis, axis + 1).reshape(shape)


def kernel(x_prompt, x_sample, cache_win_k, cache_win_v, state_conv, norm_mix, w_in, sinks, gmlp_ln_g, gmlp_ln_b, gmlp_ws, gmlp_bs, w_branch_attn, w_branch_gmlp, w_out, norm_ffn, w_up, conv_w, conv_b, w_down, norm_final):
    batch = x_prompt.shape[0]
    dec_batch, dec_seq, _ = x_sample.shape
    w_in_b = w_in.astype(BF16)
    w_q_b = _heads_group_major(w_in_b[:, :, :ATTN_W], 2)
    w_pa_b = _heads_group_major(w_branch_attn, 1).astype(BF16)
    mixer_shared = (norm_mix.reshape(DEPTH, 1, D_MODEL), w_q_b, w_in_b)
    mixer_ln = (gmlp_ln_g.reshape(DEPTH, 1, GMLP_W), gmlp_ln_b.reshape(DEPTH, 1, GMLP_W))
    mixer_tail = (w_pa_b, w_branch_gmlp.astype(BF16), w_out.astype(BF16))
    gmlp_w = (gmlp_ws, gmlp_bs.reshape(DEPTH, GMLP_GROUPS, CHUNK, 1))
    ffn_w = (norm_ffn.reshape(DEPTH, 1, D_MODEL), w_up.astype(BF16), conv_w, conv_b.reshape(DEPTH, 1, 2 * D_FF),
             w_down.astype(BF16), norm_final.reshape(1, D_MODEL))
    cache_k = jnp.transpose(cache_win_k, (0, 1, 3, 4, 2)).reshape(DEPTH, dec_batch, KV_W, WINDOW)
    cache_v = jnp.transpose(cache_win_v, (0, 1, 3, 4, 2)).reshape(DEPTH, dec_batch, KV_W, WINDOW)

    xp = x_prompt
    xs = x_sample.reshape(dec_batch * dec_seq, D_MODEL)
    kp = jnp.zeros((DEPTH, batch, WINDOW, KV_W), F32)
    vp = jnp.zeros((DEPTH, batch, WINDOW, KV_W), F32)
    cp = jnp.zeros((DEPTH, batch, CONV_W - 1, 2 * D_FF), F32)
    gv = jnp.zeros((DEPTH, dec_batch * dec_seq, GMLP_W), F32)
    ks = jnp.zeros(cache_k.shape, F32)
    vs = jnp.zeros(cache_v.shape, F32)
    cs = jnp.zeros(state_conv.shape, F32)
    for l in range(DEPTH):
        sink_l = sinks[l]
        xp, kp, vp = _mixer_prompt(l, xp, mixer_shared + (sink_l,) + mixer_ln + gmlp_w + mixer_tail, kp, vp)
        xp, cp = _ffn_prompt(l, xp, ffn_w, cp)
        xs, ks, vs, gv = _mixer_sample(l, xs, mixer_shared + (sink_l,) + mixer_ln + gmlp_w + mixer_tail,
                                       cache_k, cache_v, ks, vs, gv)
        xs, cs = _ffn_sample(l, xs, ffn_w, state_conv, cs)

    kv_prompt = (DEPTH, batch, WINDOW, N_KV_HEADS, HEAD_DIM)
    kv_sample_t = (DEPTH, dec_batch, N_KV_HEADS, HEAD_DIM, WINDOW)
    return (xp, xs.reshape(x_sample.shape),
            kp.reshape(kv_prompt), vp.reshape(kv_prompt), cp,
            jnp.transpose(ks.reshape(kv_sample_t), (0, 1, 4, 2, 3)),
            jnp.transpose(vs.reshape(kv_sample_t), (0, 1, 4, 2, 3)), cs,
            gv.reshape(DEPTH, dec_batch, dec_seq, GMLP_W))
```

```python
import functools

import jax
import jax.numpy as jnp
import numpy as np
from jax import lax
from jax.experimental import pallas as pl
from jax.experimental.pallas import tpu as pltpu

D_MODEL = 1024
DEPTH = 4
HEAD_DIM = 64
N_HEADS = 16
N_KV_HEADS = 4
GQA_GROUP = N_HEADS // N_KV_HEADS
ATTN_W = N_HEADS * HEAD_DIM
KV_W = N_KV_HEADS * HEAD_DIM
WINDOW = 128
BLOCK = 128
CHUNK = 128
GMLP_CH = 128
GMLP_GROUPS = 6
GMLP_W = GMLP_GROUPS * GMLP_CH
D_FF = 2816
CONV_W = 3
EPS = 1e-5
NEG = -1e30

K0, V0, GU0, GV0, GA0, GB0, REST_COLS = 1024, 1280, 1536, 2304, 3072, 4096, 5120

V7X_VMEM_BYTES = 64 * 1024 * 1024
VMEM_LIMIT_BYTES = V7X_VMEM_BYTES - 8 * 1024 * 1024
F32_SUBLANES = 8

TM_PROMPT = 512
TM_PROMPT_FFN = 512
FFN_SUBTILE = 256
SEQ_PER_STEP_MIXER = 16
SEQ_PER_STEP_FFN = 32
FFN_CHUNK = 128
FFN_DOWN_K = 256
FFN_LOOKAHEAD = 8
ATTN_LOOKAHEAD = 2
PROJ_TILE = 256

F32 = jnp.float32
BF16 = jnp.bfloat16


def _rmsnorm(x, g):
    return x * lax.rsqrt(jnp.mean(x * x, axis=-1, keepdims=True) + EPS) * g


def _layernorm(x, g, b):
    mu = jnp.mean(x, axis=-1, keepdims=True)
    xc = x - mu
    var = jnp.mean(xc * xc, axis=-1, keepdims=True)
    return xc * lax.rsqrt(var + EPS) * g + b


def _gelu(x):
    c = np.sqrt(2.0 / np.pi).astype(np.float32)
    return x * (0.5 * (1.0 + jnp.tanh(c * (x + 0.044715 * (x * x * x)))))


def _sigmoid(x):
    return 1.0 / (1.0 + jnp.exp(-x))


def _dot(a, b):
    return jnp.dot(a, b, preferred_element_type=F32)


def _dot_nt(a, b):
    return lax.dot_general(a, b, (((1,), (1,)), ((), ())), preferred_element_type=F32)


def _kv_lane_masks(rows):
    lane = lax.broadcasted_iota(jnp.int32, (rows, KV_W), 1)
    return [(lane >= h * HEAD_DIM) & (lane < (h + 1) * HEAD_DIM) for h in range(N_KV_HEADS)]


def _select_kv_lanes(masks, parts):
    out = parts[N_KV_HEADS - 1]
    for h in range(N_KV_HEADS - 2, -1, -1):
        out = jnp.where(masks[h], parts[h], out)
    return out


def _softmax_pv(s, sink, vv):
    m = jnp.maximum(jnp.max(s, axis=-1, keepdims=True), sink)
    p = jnp.exp(s - m)
    denom = jnp.sum(p, axis=-1, keepdims=True) + jnp.exp(sink - m)
    return _dot(p.astype(BF16), vv) / denom


def _gmlp_mix(vn, ws_ref, bs_ref, period):
    vnb = vn.astype(BF16)
    row = lax.broadcasted_iota(jnp.int32, (CHUNK, CHUNK), 0)
    col = lax.broadcasted_iota(jnp.int32, (CHUNK, CHUNK), 1)
    keep = (col <= row) & (col >= row - (row & (period - 1)))
    reps = CHUNK // period
    w, bias = [], []
    for g in range(GMLP_GROUPS):
        if reps == 1:
            wg, bg = ws_ref[g], bs_ref[g]
        else:
            top = ws_ref[g, 0:period, :]
            wg = jnp.concatenate([pltpu.roll(top, i * period, 1) if i else top for i in range(reps)], axis=0)
            bg = jnp.concatenate([bs_ref[g, 0:period, :]] * reps, axis=0)
        w.append(jnp.where(keep, wg, 0.0).astype(BF16))
        bias.append(bg)
    rows = []
    for c in range(vn.shape[0] // CHUNK):
        cols = []
        for g in range(GMLP_GROUPS):
            blk = vnb[c * CHUNK:(c + 1) * CHUNK, g * GMLP_CH:(g + 1) * GMLP_CH]
            cols.append(_dot(w[g], blk) + bias[g])
        rows.append(jnp.concatenate(cols, axis=1))
    return rows[0] if len(rows) == 1 else jnp.concatenate(rows, axis=0)


class _GateProjection:
    def __init__(self, hb, wr_ref):
        self.hb, self.wr_ref, self.tiles = hb, wr_ref, []

    def issue(self, n_tiles):
        for _ in range(n_tiles):
            c0 = GU0 + len(self.tiles) * PROJ_TILE
            if c0 < REST_COLS:
                self.tiles.append(_dot(self.hb, self.wr_ref[:, c0:c0 + PROJ_TILE]))

    def columns(self, c0, c1):
        self.issue((REST_COLS - GU0) // PROJ_TILE)
        return jnp.concatenate(self.tiles[(c0 - GU0) // PROJ_TILE:(c1 - GU0) // PROJ_TILE], axis=1)


def _gate_merge_out(x, o, proj, lng_ref, lnb_ref, ws_ref, bs_ref, wpa_ref, wpb_ref, wout_ref, period):
    u = _gelu(proj.columns(GU0, GV0))
    vn = _layernorm(_gelu(proj.columns(GV0, GA0)), lng_ref[...], lnb_ref[...])
    ga = proj.columns(GA0, GB0)
    gb = proj.columns(GB0, REST_COLS)
    oa = _dot(o.astype(BF16), wpa_ref[...])
    s_gate = u * _gmlp_mix(vn, ws_ref, bs_ref, period)
    merged = _sigmoid(ga) * oa + _sigmoid(gb) * _dot(s_gate.astype(BF16), wpb_ref[...])
    return x + _dot(merged.astype(BF16), wout_ref[...]), vn


def _mixer_prompt_kernel(x_ref, xprev_ref, nm_ref, wq_ref, wr_ref, sink_ref, lng_ref, lnb_ref, ws_ref, bs_ref,
                         wpa_ref, wpb_ref, wout_ref,
                         xo_ref, kt_ref, vt_ref):
    i = pl.program_id(1)
    tm = x_ref.shape[0]
    x = x_ref[...]
    hb = _rmsnorm(x, nm_ref[...]).astype(BF16)

    hb_prev = _rmsnorm(xprev_ref[...], nm_ref[...]).astype(BF16)
    kv_prev = _dot(hb_prev, wr_ref[:, K0:GU0])
    k = _dot(hb, wr_ref[:, K0:V0])
    v = _dot(hb, wr_ref[:, V0:GU0])
    q = _dot(hb, wq_ref[...]) * (HEAD_DIM ** -0.5)
    proj = _GateProjection(hb, wr_ref)

    kk_all = jnp.concatenate([kv_prev[:, 0:KV_W], k], axis=0).astype(BF16)
    vv_all = jnp.concatenate([kv_prev[:, KV_W:2 * KV_W], v], axis=0).astype(BF16)
    kt_ref[...] = k[tm - WINDOW:tm]
    vt_ref[...] = v[tm - WINDOW:tm]

    rows = N_KV_HEADS * BLOCK
    t = lax.broadcasted_iota(jnp.int32, (rows, 2 * BLOCK), 0) & (BLOCK - 1)
    c = lax.broadcasted_iota(jnp.int32, (rows, 2 * BLOCK), 1)
    band = (c > t) & (c <= t + WINDOW)
    bias = jnp.where(band, 0.0, NEG).astype(F32)
    first_key = jnp.where(i > 0, 0, BLOCK)
    bias_first = jnp.where(band & (c >= first_key), 0.0, NEG).astype(F32)
    head_of_row = lax.broadcasted_iota(jnp.int32, (rows, 1), 0) >> (BLOCK.bit_length() - 1)
    masks = _kv_lane_masks(BLOCK)

    items = [(j, g) for j in range(tm // BLOCK) for g in range(GQA_GROUP)]

    def scores(item):
        j, g = item
        qg = q[j * BLOCK:(j + 1) * BLOCK, g * KV_W:(g + 1) * KV_W]
        lhs = jnp.concatenate([jnp.where(masks[h], qg, 0.0) for h in range(N_KV_HEADS)], axis=0).astype(BF16)
        return _dot_nt(lhs, kk_all[j * BLOCK:(j + 2) * BLOCK]) + (bias_first if j == 0 else bias)

    fill = -(-((REST_COLS - GU0) // PROJ_TILE) // len(items))
    o_groups = {}
    ahead = [scores(item) for item in items[:ATTN_LOOKAHEAD]]
    for n, (j, g) in enumerate(items):
        s = ahead.pop(0)
        if n + ATTN_LOOKAHEAD < len(items):
            ahead.append(scores(items[n + ATTN_LOOKAHEAD]))
        proj.issue(fill)
        sink = jnp.full((rows, 1), sink_ref[(N_KV_HEADS - 1) * GQA_GROUP + g], F32)
        for h in range(N_KV_HEADS - 2, -1, -1):
            sink = jnp.where(head_of_row == h, sink_ref[h * GQA_GROUP + g], sink)
        r = _softmax_pv(s, sink, vv_all[j * BLOCK:(j + 2) * BLOCK])
        o_groups[(j, g)] = _select_kv_lanes(masks, [r[h * BLOCK:(h + 1) * BLOCK] for h in range(N_KV_HEADS)])
    o_blocks = [jnp.concatenate([o_groups[(j, g)] for g in range(GQA_GROUP)], axis=1) for j in range(tm // BLOCK)]
    o = o_blocks[0] if len(o_blocks) == 1 else jnp.concatenate(o_blocks, axis=0)

    xo_ref[...], _ = _gate_merge_out(x, o, proj, lng_ref, lnb_ref, ws_ref, bs_ref, wpa_ref, wpb_ref, wout_ref, CHUNK)


def _mixer_sample_kernel(x_ref, nm_ref, wq_ref, wr_ref, sink_ref, lng_ref, lnb_ref, ws_ref, bs_ref,
                         wpa_ref, wpb_ref, wout_ref, ckt_ref, cvt_ref,
                         xo_ref, wkt_ref, wvt_ref, vn_ref):
    n_seq, _, past = ckt_ref.shape
    m_rows = x_ref.shape[0]
    t_new = m_rows // n_seq
    assert m_rows == past and past == BLOCK
    x = x_ref[...]
    hb = _rmsnorm(x, nm_ref[...]).astype(BF16)
    q = _dot(hb, wq_ref[...]) * (HEAD_DIM ** -0.5)
    k = _dot(hb, wr_ref[:, K0:V0])
    v = _dot(hb, wr_ref[:, V0:GU0])
    kt = jnp.transpose(k)
    vt = jnp.transpose(v)
    proj = _GateProjection(hb, wr_ref)

    n_keys = 2 * BLOCK
    rows = N_HEADS * t_new
    t = lax.broadcasted_iota(jnp.int32, (rows, n_keys), 0) & (t_new - 1)
    c = lax.broadcasted_iota(jnp.int32, (rows, n_keys), 1)
    diff = t + past - c
    bias = jnp.where((diff >= 0) & (diff < WINDOW) & (c < past + t_new), 0.0, NEG).astype(F32)
    head_slot = lax.broadcasted_iota(jnp.int32, (rows, 1), 0) >> (t_new.bit_length() - 1)
    sink = jnp.zeros((rows, 1), F32)
    for g in range(GQA_GROUP):
        for h in range(N_KV_HEADS):
            sink = jnp.where(head_slot == g * N_KV_HEADS + h, sink_ref[h * GQA_GROUP + g], sink)
    masks = _kv_lane_masks(t_new)
    zero_rows = jnp.zeros((BLOCK - t_new, KV_W), F32)
    keep_old = lax.broadcasted_iota(jnp.int32, (KV_W, past), 1) < past - t_new

    def new_window(old_t, new_t, b):
        shift = (past - t_new - b * t_new) % past
        placed = pltpu.roll(new_t, shift, 1) if shift else new_t
        return jnp.where(keep_old, pltpu.roll(old_t, past - t_new, 1), placed)

    s_list, v_list = [], []
    for b in range(n_seq):
        kct = ckt_ref[b]
        vct = cvt_ref[b]
        wkt_ref[b] = new_window(kct, kt, b)
        wvt_ref[b] = new_window(vct, vt, b)
        rows_b = slice(b * t_new, (b + 1) * t_new)
        k_pad = jnp.concatenate([k[rows_b], zero_rows], axis=0).astype(BF16)
        v_pad = jnp.concatenate([v[rows_b], zero_rows], axis=0).astype(BF16)
        qb = q[rows_b]
        lhs = jnp.concatenate(
            [jnp.where(masks[h], qb[:, g * KV_W:(g + 1) * KV_W], 0.0)
             for g in range(GQA_GROUP) for h in range(N_KV_HEADS)], axis=0).astype(BF16)
        s_list.append(jnp.concatenate([_dot(lhs, kct.astype(BF16)), _dot_nt(lhs, k_pad)], axis=1) + bias)
        v_list.append((vct.astype(BF16), v_pad))
    proj.issue((REST_COLS - GU0) // PROJ_TILE)
    o_rows = []
    for b in range(n_seq):
        s = s_list[b]
        vct_b, v_pad = v_list[b]
        m = jnp.maximum(jnp.max(s, axis=-1, keepdims=True), sink)
        p = jnp.exp(s - m)
        denom = jnp.sum(p, axis=-1, keepdims=True) + jnp.exp(sink - m)
        pb = p.astype(BF16)
        r = (_dot_nt(pb[:, 0:past], vct_b) + _dot(pb[:, past:n_keys], v_pad)) / denom
        o_groups = []
        for g in range(GQA_GROUP):
            base = g * N_KV_HEADS * t_new
            o_groups.append(_select_kv_lanes(
                masks, [r[base + h * t_new:base + (h + 1) * t_new] for h in range(N_KV_HEADS)]))
        o_rows.append(jnp.concatenate(o_groups, axis=1))
    o = jnp.concatenate(o_rows, axis=0)

    xo_ref[...], vn_ref[...] = _gate_merge_out(x, o, proj, lng_ref, lnb_ref, ws_ref, bs_ref,
                                               wpa_ref, wpb_ref, wout_ref, t_new)


def _conv_gate(za, zb, sa, sb, cw_ref, cb_ref, a0, b0, width):
    def conv(z, shifted, c0):
        z1, z2 = shifted
        cols = slice(c0, c0 + width)
        return cb_ref[:, cols] + ((z2 * cw_ref[0:1, cols] + z1 * cw_ref[1:2, cols]) + z * cw_ref[2:3, cols])

    a = conv(za, sa, a0)
    b = conv(zb, sb, b0)
    c0 = np.sqrt(2.0 / np.pi).astype(np.float32)
    c1 = np.float32(c0 * np.float32(0.044715))
    th = jnp.tanh(a * (c1 * (a * a) + c0))
    half_ab = (0.5 * a) * b
    return half_ab + half_ab * th


def _ffn_body(x, nf_ref, wup_ref, cw_ref, cb_ref, wdn_ref, nfin_ref, shift_fn, final):
    hb = _rmsnorm(x, nf_ref[...]).astype(BF16)

    def up(c):
        a0 = c * FFN_CHUNK
        w = jnp.concatenate([wup_ref[:, a0:a0 + FFN_CHUNK], wup_ref[:, D_FF + a0:D_FF + a0 + FFN_CHUNK]], axis=1)
        return _dot(hb, w)

    n_chunks = D_FF // FFN_CHUNK
    per_down = FFN_DOWN_K // FFN_CHUNK
    acc = x
    ahead = [up(c) for c in range(FFN_LOOKAHEAD)]
    pending = []
    for c in range(n_chunks):
        z = ahead.pop(0)
        if c + FFN_LOOKAHEAD < n_chunks:
            ahead.append(up(c + FFN_LOOKAHEAD))
        a0 = c * FFN_CHUNK
        b0 = D_FF + a0
        za, zb = z[:, 0:FFN_CHUNK], z[:, FFN_CHUNK:2 * FFN_CHUNK]
        pending.append(_conv_gate(za, zb, shift_fn(za, a0), shift_fn(zb, b0), cw_ref, cb_ref, a0, b0, FFN_CHUNK))
        if len(pending) == per_down:
            gated = pending[0] if per_down == 1 else jnp.concatenate(pending, axis=1)
            r0 = a0 + FFN_CHUNK - FFN_DOWN_K
            acc = acc + _dot(gated.astype(BF16), wdn_ref[r0:r0 + FFN_DOWN_K, :])
            pending = []
    if final:
        acc = _rmsnorm(acc, nfin_ref[...])
    return acc


def _ffn_prompt_kernel(x_ref, nf_ref, wup_ref, cw_ref, cb_ref, wdn_ref, nfin_ref,
                       xo_ref, ct_ref, carry_ref, *, final):
    sub = carry_ref.shape[0]

    @pl.when(pl.program_id(1) == 0)
    def _():
        carry_ref[...] = jnp.zeros_like(carry_ref)

    def shift_fn(z, c0):
        rows = z.shape[0]
        cols = slice(c0, c0 + z.shape[1])
        ext = jnp.concatenate([carry_ref[:, cols], z], axis=0)
        last = z[rows - sub:rows]
        carry_ref[:, cols] = last
        ct_ref[:, cols] = pltpu.roll(last, CONV_W - 1, 0)[0:CONV_W - 1]
        return pltpu.roll(ext, 1, 0)[sub:], pltpu.roll(ext, 2, 0)[sub:]

    def sub_tile(n, carry):
        r0 = pl.multiple_of(n * FFN_SUBTILE, FFN_SUBTILE)
        xo_ref[pl.ds(r0, FFN_SUBTILE), :] = _ffn_body(
            x_ref[pl.ds(r0, FFN_SUBTILE), :], nf_ref, wup_ref, cw_ref, cb_ref, wdn_ref, nfin_ref, shift_fn, final)
        return carry

    lax.fori_loop(0, x_ref.shape[0] // FFN_SUBTILE, sub_tile, 0)


def _ffn_sample_kernel(x_ref, nf_ref, wup_ref, cw_ref, cb_ref, wdn_ref, nfin_ref, st_ref,
                       xo_ref, ct_ref, *, final):
    n_seq = st_ref.shape[0]
    m_rows = x_ref.shape[0]
    t_new = m_rows // n_seq

    def shift_fn(z, c0):
        width = z.shape[1]
        cols = slice(c0, c0 + width)
        z3 = z.reshape(n_seq, t_new, width)
        st = st_ref[:, :, cols]
        p0 = jnp.broadcast_to(st[:, 0:1, :], z3.shape)
        p1 = jnp.broadcast_to(st[:, 1:2, :], z3.shape)
        t = lax.broadcasted_iota(jnp.int32, z3.shape, 1)
        r1 = pltpu.roll(z3, 1, 1)
        r2 = pltpu.roll(z3, 2, 1)
        ct_ref[:, :, cols] = r2[:, 0:CONV_W - 1, :]
        z1 = jnp.where(t == 0, p1, r1)
        z2 = jnp.where(t == 0, p0, jnp.where(t == 1, p1, r2))
        return z1.reshape(m_rows, width), z2.reshape(m_rows, width)

    xo_ref[...] = _ffn_body(x_ref[...], nf_ref, wup_ref, cw_ref, cb_ref, wdn_ref, nfin_ref, shift_fn, final)


def _drop_aliased(kernel_fn, n_in, n_aliased):
    def wrapped(*refs):
        return kernel_fn(*refs[:n_in], *refs[n_in + n_aliased:])
    return wrapped


def _layer_spec(layer, shape):
    nd = len(shape)
    return pl.BlockSpec((None,) + tuple(shape), lambda *_: (layer,) + (0,) * nd, pipeline_mode=pl.Buffered(1))


_SMEM_SPEC = pl.BlockSpec(memory_space=pltpu.SMEM)
_ANY_SPEC = pl.BlockSpec(memory_space=pl.ANY)


def _compiler_params(n_axes):
    return pltpu.CompilerParams(dimension_semantics=("arbitrary",) * n_axes,
                                vmem_limit_bytes=VMEM_LIMIT_BYTES)


def _mixer_weight_specs(layer):
    return [
        _layer_spec(layer, (1, D_MODEL)),
        _layer_spec(layer, (D_MODEL, ATTN_W)),
        _layer_spec(layer, (D_MODEL, REST_COLS)),
        _SMEM_SPEC,
        _layer_spec(layer, (1, GMLP_W)),
        _layer_spec(layer, (1, GMLP_W)),
        _layer_spec(layer, (GMLP_GROUPS, CHUNK, CHUNK)),
        _layer_spec(layer, (GMLP_GROUPS, CHUNK, 1)),
        _layer_spec(layer, (ATTN_W, D_MODEL)),
        _layer_spec(layer, (GMLP_W, D_MODEL)),
        _layer_spec(layer, (D_MODEL, D_MODEL)),
    ]


def _ffn_weight_specs(layer):
    return [
        _layer_spec(layer, (1, D_MODEL)),
        _layer_spec(layer, (D_MODEL, 2 * D_FF)),
        _layer_spec(layer, (CONV_W, 2 * D_FF)),
        _layer_spec(layer, (1, 2 * D_FF)),
        _layer_spec(layer, (D_FF, D_MODEL)),
        pl.BlockSpec((1, D_MODEL), lambda *_: (0, 0), pipeline_mode=pl.Buffered(1)),
    ]


def _call_layer(kernel_fn, grid, inputs, in_specs, out_specs, out_shapes, carried, scratch, name):
    n_in = len(inputs)
    first_carried = len(out_shapes) - len(carried)
    aliases = {n_in + n: first_carried + n for n in range(len(carried))}
    return pl.pallas_call(
        _drop_aliased(kernel_fn, n_in, len(carried)), grid=grid,
        in_specs=list(in_specs) + [_ANY_SPEC] * len(carried), out_specs=out_specs, out_shape=out_shapes,
        scratch_shapes=scratch, input_output_aliases=aliases,
        compiler_params=_compiler_params(len(grid)), name=name,
    )(*inputs, *carried)


def _mixer_prompt(layer, x, mixer_w, k_tails, v_tails):
    batch, seq, _ = x.shape
    tm = TM_PROMPT
    tile = pl.BlockSpec((None, tm, D_MODEL), lambda b, i: (b, i, 0))
    prev_block = pl.BlockSpec((None, BLOCK, D_MODEL), lambda b, i: (b, jnp.maximum(i * (tm // BLOCK) - 1, 0), 0))
    tail = pl.BlockSpec((None, None, WINDOW, KV_W), lambda b, i: (layer, b, 0, 0))
    tail_shape = jax.ShapeDtypeStruct(k_tails.shape, F32)
    return _call_layer(
        _mixer_prompt_kernel, (batch, seq // tm), (x, x) + mixer_w,
        [tile, prev_block] + _mixer_weight_specs(layer), [tile, tail, tail],
        [jax.ShapeDtypeStruct(x.shape, F32), tail_shape, tail_shape], [k_tails, v_tails], [], "mixer_prompt")


def _mixer_sample(layer, x, mixer_w, cache_k, cache_v, win_k, win_v, vn_all):
    n_rows = x.shape[0]
    _, n_seq_total, _, past = cache_k.shape
    t_new = n_rows // n_seq_total
    ns = SEQ_PER_STEP_MIXER
    m = ns * t_new
    tile = pl.BlockSpec((m, D_MODEL), lambda i: (i, 0))
    win = pl.BlockSpec((None, ns, KV_W, past), lambda i: (layer, i, 0, 0))
    vn_spec = pl.BlockSpec((None, m, GMLP_W), lambda i: (layer, i, 0))
    return _call_layer(
        _mixer_sample_kernel, (n_seq_total // ns,), (x,) + mixer_w + (cache_k, cache_v),
        [tile] + _mixer_weight_specs(layer) + [win, win], [tile, win, win, vn_spec],
        [jax.ShapeDtypeStruct(x.shape, F32), jax.ShapeDtypeStruct(win_k.shape, F32),
         jax.ShapeDtypeStruct(win_v.shape, F32), jax.ShapeDtypeStruct(vn_all.shape, F32)],
        [win_k, win_v, vn_all], [], "mixer_sample")


def _ffn_prompt(layer, x, ffn_w, conv_tails):
    batch, seq, _ = x.shape
    tm = TM_PROMPT_FFN
    tile = pl.BlockSpec((None, tm, D_MODEL), lambda b, i: (b, i, 0))
    tail = pl.BlockSpec((None, None, CONV_W - 1, 2 * D_FF), lambda b, i: (layer, b, 0, 0))
    return _call_layer(
        functools.partial(_ffn_prompt_kernel, final=layer == DEPTH - 1), (batch, seq // tm), (x,) + ffn_w,
        [tile] + _ffn_weight_specs(layer), [tile, tail],
        [jax.ShapeDtypeStruct(x.shape, F32), jax.ShapeDtypeStruct(conv_tails.shape, F32)],
        [conv_tails], [pltpu.VMEM((F32_SUBLANES, 2 * D_FF), F32)], "ffn_prompt")


def _ffn_sample(layer, x, ffn_w, state, new_state):
    n_rows = x.shape[0]
    n_seq_total = state.shape[1]
    t_new = n_rows // n_seq_total
    ns = SEQ_PER_STEP_FFN
    m = ns * t_new
    tile = pl.BlockSpec((m, D_MODEL), lambda i: (i, 0))
    st = pl.BlockSpec((None, ns, CONV_W - 1, 2 * D_FF), lambda i: (layer, i, 0, 0))
    return _call_layer(
        functools.partial(_ffn_sample_kernel, final=layer == DEPTH - 1), (n_seq_total // ns,),
        (x,) + ffn_w + (state,), [tile] + _ffn_weight_specs(layer) + [st], [tile, st],
        [jax.ShapeDtypeStruct(x.shape, F32), jax.ShapeDtypeStruct(state.shape, F32)],
        [new_state], [], "ffn_sample")


def _heads_group_major(w, axis):
    shape = w.shape
    split = shape[:axis] + (N_KV_HEADS, GQA_GROUP, HEAD_DIM) + shape[axis + 1:]
    return jnp.swapaxes(w.reshape(split), axis, axis + 1).reshape(shape)


def kernel(x_prompt, x_sample, cache_win_k, cache_win_v, state_conv, norm_mix, w_in, sinks, gmlp_ln_g, gmlp_ln_b, gmlp_ws, gmlp_bs, w_branch_attn, w_branch_gmlp, w_out, norm_ffn, w_up, conv_w, conv_b, w_down, norm_final):
    batch = x_prompt.shape[0]
    dec_batch, dec_seq, _ = x_sample.shape
    w_in_b = w_in.astype(BF16)
    w_q_b = _heads_group_major(w_in_b[:, :, :ATTN_W], 2)
    w_pa_b = _heads_group_major(w_branch_attn, 1).astype(BF16)
    mixer_shared = (norm_mix.reshape(DEPTH, 1, D_MODEL), w_q_b, w_in_b)
    mixer_ln = (gmlp_ln_g.reshape(DEPTH, 1, GMLP_W), gmlp_ln_b.reshape(DEPTH, 1, GMLP_W))
    mixer_tail = (w_pa_b, w_branch_gmlp.astype(BF16), w_out.astype(BF16))
    gmlp_w = (gmlp_ws, gmlp_bs.reshape(DEPTH, GMLP_GROUPS, CHUNK, 1))
    ffn_w = (norm_ffn.reshape(DEPTH, 1, D_MODEL), w_up.astype(BF16), conv_w, conv_b.reshape(DEPTH, 1, 2 * D_FF),
             w_down.astype(BF16), norm_final.reshape(1, D_MODEL))
    cache_k = jnp.transpose(cache_win_k, (0, 1, 3, 4, 2)).reshape(DEPTH, dec_batch, KV_W, WINDOW)
    cache_v = jnp.transpose(cache_win_v, (0, 1, 3, 4, 2)).reshape(DEPTH, dec_batch, KV_W, WINDOW)

    xp = x_prompt
    xs = x_sample.reshape(dec_batch * dec_seq, D_MODEL)
    kp = jnp.zeros((DEPTH, batch, WINDOW, KV_W), F32)
    vp = jnp.zeros((DEPTH, batch, WINDOW, KV_W), F32)
    cp = jnp.zeros((DEPTH, batch, CONV_W - 1, 2 * D_FF), F32)
    gv = jnp.zeros((DEPTH, dec_batch * dec_seq, GMLP_W), F32)
    ks = jnp.zeros(cache_k.shape, F32)
    vs = jnp.zeros(cache_v.shape, F32)
    cs = jnp.zeros(state_conv.shape, F32)
    for l in range(DEPTH):
        sink_l = sinks[l]
        xp, kp, vp = _mixer_prompt(l, xp, mixer_shared + (sink_l,) + mixer_ln + gmlp_w + mixer_tail, kp, vp)
        xp, cp = _ffn_prompt(l, xp, ffn_w, cp)
        xs, ks, vs, gv = _mixer_sample(l, xs, mixer_shared + (sink_l,) + mixer_ln + gmlp_w + mixer_tail,
                                       cache_k, cache_v, ks, vs, gv)
        xs, cs = _ffn_sample(l, xs, ffn_w, state_conv, cs)

    kv_prompt = (DEPTH, batch, WINDOW, N_KV_HEADS, HEAD_DIM)
    kv_sample_t = (DEPTH, dec_batch, N_KV_HEADS, HEAD_DIM, WINDOW)
    return (xp, xs.reshape(x_sample.shape),
            kp.reshape(kv_prompt), vp.reshape(kv_prompt), cp,
            jnp.transpose(ks.reshape(kv_sample_t), (0, 1, 4, 2, 3)),
            jnp.transpose(vs.reshape(kv_sample_t), (0, 1, 4, 2, 3)), cs,
            gv.reshape(DEPTH, dec_batch, dec_seq, GMLP_W))
```

```python
import functools

import jax
import jax.numpy as jnp
import numpy as np
from jax import lax
from jax.experimental import pallas as pl
from jax.experimental.pallas import tpu as pltpu

D_MODEL = 1024
DEPTH = 4
HEAD_DIM = 64
N_HEADS = 16
N_KV_HEADS = 4
GQA_GROUP = N_HEADS // N_KV_HEADS
ATTN_W = N_HEADS * HEAD_DIM
KV_W = N_KV_HEADS * HEAD_DIM
WINDOW = 128
BLOCK = 128
CHUNK = 128
GMLP_CH = 128
GMLP_GROUPS = 6
GMLP_W = GMLP_GROUPS * GMLP_CH
D_FF = 2816
CONV_W = 3
EPS = 1e-5
NEG = -1e30

K0, V0, GU0, GV0, GA0, GB0, REST_COLS = 1024, 1280, 1536, 2304, 3072, 4096, 5120

V7X_VMEM_BYTES = 64 * 1024 * 1024
VMEM_LIMIT_BYTES = V7X_VMEM_BYTES - 8 * 1024 * 1024
F32_SUBLANES = 8

TM_PROMPT = 512
TM_PROMPT_FFN = 512
FFN_SUBTILE = 256
SEQ_PER_STEP_MIXER = 16
SEQ_PER_STEP_FFN = 32
FFN_CHUNK = 128
FFN_DOWN_K = 256
FFN_LOOKAHEAD = 8
ATTN_LOOKAHEAD = 2
PROJ_TILE = 256

F32 = jnp.float32
BF16 = jnp.bfloat16


def _rmsnorm(x, g):
    return x * lax.rsqrt(jnp.mean(x * x, axis=-1, keepdims=True) + EPS) * g


def _layernorm(x, g, b):
    mu = jnp.mean(x, axis=-1, keepdims=True)
    xc = x - mu
    var = jnp.mean(xc * xc, axis=-1, keepdims=True)
    return xc * lax.rsqrt(var + EPS) * g + b


def _gelu(x):
    c = np.sqrt(2.0 / np.pi).astype(np.float32)
    return x * (0.5 * (1.0 + jnp.tanh(c * (x + 0.044715 * (x * x * x)))))


def _sigmoid(x):
    return 1.0 / (1.0 + jnp.exp(-x))


def _dot(a, b):
    return jnp.dot(a, b, preferred_element_type=F32)


def _dot_nt(a, b):
    return lax.dot_general(a, b, (((1,), (1,)), ((), ())), preferred_element_type=F32)


def _kv_lane_masks(rows):
    lane = lax.broadcasted_iota(jnp.int32, (rows, KV_W), 1)
    return [(lane >= h * HEAD_DIM) & (lane < (h + 1) * HEAD_DIM) for h in range(N_KV_HEADS)]


def _select_kv_lanes(masks, parts):
    out = parts[N_KV_HEADS - 1]
    for h in range(N_KV_HEADS - 2, -1, -1):
        out = jnp.where(masks[h], parts[h], out)
    return out


def _softmax_pv(s, sink, vv):
    m = jnp.maximum(jnp.max(s, axis=-1, keepdims=True), sink)
    p = jnp.exp(s - m)
    denom = jnp.sum(p, axis=-1, keepdims=True) + jnp.exp(sink - m)
    return _dot(p.astype(BF16), vv) / denom


def _gmlp_mix(vn, ws_ref, bs_ref, period):
    vnb = vn.astype(BF16)
    row = lax.broadcasted_iota(jnp.int32, (CHUNK, CHUNK), 0)
    col = lax.broadcasted_iota(jnp.int32, (CHUNK, CHUNK), 1)
    keep = (col <= row) & (col >= row - (row & (period - 1)))
    reps = CHUNK // period
    w, bias = [], []
    for g in range(GMLP_GROUPS):
        if reps == 1:
            wg, bg = ws_ref[g], bs_ref[g]
        else:
            top = ws_ref[g, 0:period, :]
            wg = jnp.concatenate([pltpu.roll(top, i * period, 1) if i else top for i in range(reps)], axis=0)
            bg = jnp.concatenate([bs_ref[g, 0:period, :]] * reps, axis=0)
        w.append(jnp.where(keep, wg, 0.0).astype(BF16))
        bias.append(bg)
    rows = []
    for c in range(vn.shape[0] // CHUNK):
        cols = []
        for g in range(GMLP_GROUPS):
            blk = vnb[c * CHUNK:(c + 1) * CHUNK, g * GMLP_CH:(g + 1) * GMLP_CH]
            cols.append(_dot(w[g], blk) + bias[g])
        rows.append(jnp.concatenate(cols, axis=1))
    return rows[0] if len(rows) == 1 else jnp.concatenate(rows, axis=0)


class _GateProjection:
    def __init__(self, hb, wr_ref):
        self.hb, self.wr_ref, self.tiles = hb, wr_ref, []

    def issue(self, n_tiles):
        for _ in range(n_tiles):
            c0 = GU0 + len(self.tiles) * PROJ_TILE
            if c0 < REST_COLS:
                self.tiles.append(_dot(self.hb, self.wr_ref[:, c0:c0 + PROJ_TILE]))

    def columns(self, c0, c1):
        self.issue((REST_COLS - GU0) // PROJ_TILE)
        return jnp.concatenate(self.tiles[(c0 - GU0) // PROJ_TILE:(c1 - GU0) // PROJ_TILE], axis=1)


def _gate_merge_out(x, o, proj, lng_ref, lnb_ref, ws_ref, bs_ref, wpa_ref, wpb_ref, wout_ref, period):
    u = _gelu(proj.columns(GU0, GV0))
    vn = _layernorm(_gelu(proj.columns(GV0, GA0)), lng_ref[...], lnb_ref[...])
    ga = proj.columns(GA0, GB0)
    gb = proj.columns(GB0, REST_COLS)
    oa = _dot(o.astype(BF16), wpa_ref[...])
    s_gate = u * _gmlp_mix(vn, ws_ref, bs_ref, period)
    merged = _sigmoid(ga) * oa + _sigmoid(gb) * _dot(s_gate.astype(BF16), wpb_ref[...])
    return x + _dot(merged.astype(BF16), wout_ref[...]), vn


def _mixer_prompt_kernel(x_ref, xprev_ref, nm_ref, wq_ref, wr_ref, sink_ref, lng_ref, lnb_ref, ws_ref, bs_ref,
                         wpa_ref, wpb_ref, wout_ref,
                         xo_ref, kt_ref, vt_ref):
    i = pl.program_id(1)
    tm = x_ref.shape[0]
    x = x_ref[...]
    hb = _rmsnorm(x, nm_ref[...]).astype(BF16)

    hb_prev = _rmsnorm(xprev_ref[...], nm_ref[...]).astype(BF16)
    kv_prev = _dot(hb_prev, wr_ref[:, K0:GU0])
    k = _dot(hb, wr_ref[:, K0:V0])
    v = _dot(hb, wr_ref[:, V0:GU0])
    q = _dot(hb, wq_ref[...]) * (HEAD_DIM ** -0.5)
    proj = _GateProjection(hb, wr_ref)

    kk_all = jnp.concatenate([kv_prev[:, 0:KV_W], k], axis=0).astype(BF16)
    vv_all = jnp.concatenate([kv_prev[:, KV_W:2 * KV_W], v], axis=0).astype(BF16)
    kt_ref[...] = k[tm - WINDOW:tm]
    vt_ref[...] = v[tm - WINDOW:tm]

    rows = N_KV_HEADS * BLOCK
    t = lax.broadcasted_iota(jnp.int32, (rows, 2 * BLOCK), 0) & (BLOCK - 1)
    c = lax.broadcasted_iota(jnp.int32, (rows, 2 * BLOCK), 1)
    band = (c > t) & (c <= t + WINDOW)
    bias = jnp.where(band, 0.0, NEG).astype(F32)
    first_key = jnp.where(i > 0, 0, BLOCK)
    bias_first = jnp.where(band & (c >= first_key), 0.0, NEG).astype(F32)
    head_of_row = lax.broadcasted_iota(jnp.int32, (rows, 1), 0) >> (BLOCK.bit_length() - 1)
    masks = _kv_lane_masks(BLOCK)

    items = [(j, g) for j in range(tm // BLOCK) for g in range(GQA_GROUP)]

    def scores(item):
        j, g = item
        qg = q[j * BLOCK:(j + 1) * BLOCK, g * KV_W:(g + 1) * KV_W]
        lhs = jnp.concatenate([jnp.where(masks[h], qg, 0.0) for h in range(N_KV_HEADS)], axis=0).astype(BF16)
        return _dot_nt(lhs, kk_all[j * BLOCK:(j + 2) * BLOCK]) + (bias_first if j == 0 else bias)

    fill = -(-((REST_COLS - GU0) // PROJ_TILE) // len(items))
    o_groups = {}
    ahead = [scores(item) for item in items[:ATTN_LOOKAHEAD]]
    for n, (j, g) in enumerate(items):
        s = ahead.pop(0)
        if n + ATTN_LOOKAHEAD < len(items):
            ahead.append(scores(items[n + ATTN_LOOKAHEAD]))
        proj.issue(fill)
        sink = jnp.full((rows, 1), sink_ref[(N_KV_HEADS - 1) * GQA_GROUP + g], F32)
        for h in range(N_KV_HEADS - 2, -1, -1):
            sink = jnp.where(head_of_row == h, sink_ref[h * GQA_GROUP + g], sink)
        r = _softmax_pv(s, sink, vv_all[j * BLOCK:(j + 2) * BLOCK])
        o_groups[(j, g)] = _select_kv_lanes(masks, [r[h * BLOCK:(h + 1) * BLOCK] for h in range(N_KV_HEADS)])
    o_blocks = [jnp.concatenate([o_groups[(j, g)] for g in range(GQA_GROUP)], axis=1) for j in range(tm // BLOCK)]
    o = o_blocks[0] if len(o_blocks) == 1 else jnp.concatenate(o_blocks, axis=0)

    xo_ref[...], _ = _gate_merge_out(x, o, proj, lng_ref, lnb_ref, ws_ref, bs_ref, wpa_ref, wpb_ref, wout_ref, CHUNK)


def _mixer_sample_kernel(x_ref, nm_ref, wq_ref, wr_ref, sink_ref, lng_ref, lnb_ref, ws_ref, bs_ref,
                         wpa_ref, wpb_ref, wout_ref, ckt_ref, cvt_ref,
                         xo_ref, wkt_ref, wvt_ref, vn_ref):
    n_seq, _, past = ckt_ref.shape
    m_rows = x_ref.shape[0]
    t_new = m_rows // n_seq
    assert m_rows == past and past == BLOCK
    x = x_ref[...]
    hb = _rmsnorm(x, nm_ref[...]).astype(BF16)
    q = _dot(hb, wq_ref[...]) * (HEAD_DIM ** -0.5)
    k = _dot(hb, wr_ref[:, K0:V0])
    v = _dot(hb, wr_ref[:, V0:GU0])
    kt = jnp.transpose(k)
    vt = jnp.transpose(v)
    proj = _GateProjection(hb, wr_ref)

    n_keys = 2 * BLOCK
    rows = N_HEADS * t_new
    t = lax.broadcasted_iota(jnp.int32, (rows, n_keys), 0) & (t_new - 1)
    c = lax.broadcasted_iota(jnp.int32, (rows, n_keys), 1)
    diff = t + past - c
    bias = jnp.where((diff >= 0) & (diff < WINDOW) & (c < past + t_new), 0.0, NEG).astype(F32)
    head_slot = lax.broadcasted_iota(jnp.int32, (rows, 1), 0) >> (t_new.bit_length() - 1)
    sink = jnp.zeros((rows, 1), F32)
    for g in range(GQA_GROUP):
        for h in range(N_KV_HEADS):
            sink = jnp.where(head_slot == g * N_KV_HEADS + h, sink_ref[h * GQA_GROUP + g], sink)
    masks = _kv_lane_masks(t_new)
    zero_rows = jnp.zeros((BLOCK - t_new, KV_W), F32)
    keep_old = lax.broadcasted_iota(jnp.int32, (KV_W, past), 1) < past - t_new

    def new_window(old_t, new_t, b):
        shift = (past - t_new - b * t_new) % past
        placed = pltpu.roll(new_t, shift, 1) if shift else new_t
        return jnp.where(keep_old, pltpu.roll(old_t, past - t_new, 1), placed)

    s_list, v_list = [], []
    for b in range(n_seq):
        kct = ckt_ref[b]
        vct = cvt_ref[b]
        wkt_ref[b] = new_window(kct, kt, b)
        wvt_ref[b] = new_window(vct, vt, b)
        rows_b = slice(b * t_new, (b + 1) * t_new)
        k_pad = jnp.concatenate([k[rows_b], zero_rows], axis=0).astype(BF16)
        v_pad = jnp.concatenate([v[rows_b], zero_rows], axis=0).astype(BF16)
        qb = q[rows_b]
        lhs = jnp.concatenate(
            [jnp.where(masks[h], qb[:, g * KV_W:(g + 1) * KV_W], 0.0)
             for g in range(GQA_GROUP) for h in range(N_KV_HEADS)], axis=0).astype(BF16)
        s_list.append(jnp.concatenate([_dot(lhs, kct.astype(BF16)), _dot_nt(lhs, k_pad)], axis=1) + bias)
        v_list.append((vct.astype(BF16), v_pad))
    proj.issue((REST_COLS - GU0) // PROJ_TILE)
    o_rows = []
    for b in range(n_seq):
        s = s_list[b]
        vct_b, v_pad = v_list[b]
        m = jnp.maximum(jnp.max(s, axis=-1, keepdims=True), sink)
        p = jnp.exp(s - m)
        denom = jnp.sum(p, axis=-1, keepdims=True) + jnp.exp(sink - m)
        pb = p.astype(BF16)
        r = (_dot_nt(pb[:, 0:past], vct_b) + _dot(pb[:, past:n_keys], v_pad)) / denom
        o_groups = []
        for g in range(GQA_GROUP):
            base = g * N_KV_HEADS * t_new
            o_groups.append(_select_kv_lanes(
                masks, [r[base + h * t_new:base + (h + 1) * t_new] for h in range(N_KV_HEADS)]))
        o_rows.append(jnp.concatenate(o_groups, axis=1))
    o = jnp.concatenate(o_rows, axis=0)

    xo_ref[...], vn_ref[...] = _gate_merge_out(x, o, proj, lng_ref, lnb_ref, ws_ref, bs_ref,
                                               wpa_ref, wpb_ref, wout_ref, t_new)


def _conv_gate(za, zb, sa, sb, cw_ref, cb_ref, a0, b0, width):
    def conv(z, shifted, c0):
        z1, z2 = shifted
        cols = slice(c0, c0 + width)
        return cb_ref[:, cols] + ((z2 * cw_ref[0:1, cols] + z1 * cw_ref[1:2, cols]) + z * cw_ref[2:3, cols])

    a = conv(za, sa, a0)
    b = conv(zb, sb, b0)
    c0 = np.sqrt(2.0 / np.pi).astype(np.float32)
    c1 = np.float32(c0 * np.float32(0.044715))
    th = jnp.tanh(a * (c1 * (a * a) + c0))
    half_ab = (0.5 * a) * b
    return half_ab + half_ab * th


def _ffn_body(x, nf_ref, wup_ref, cw_ref, cb_ref, wdn_ref, nfin_ref, shift_fn, final):
    hb = _rmsnorm(x, nf_ref[...]).astype(BF16)

    def up(c):
        a0 = c * FFN_CHUNK
        w = jnp.concatenate([wup_ref[:, a0:a0 + FFN_CHUNK], wup_ref[:, D_FF + a0:D_FF + a0 + FFN_CHUNK]], axis=1)
        return _dot(hb, w)

    n_chunks = D_FF // FFN_CHUNK
    per_down = FFN_DOWN_K // FFN_CHUNK
    acc = x
    ahead = [up(c) for c in range(FFN_LOOKAHEAD)]
    pending = []
    for c in range(n_chunks):
        z = ahead.pop(0)
        if c + FFN_LOOKAHEAD < n_chunks:
            ahead.append(up(c + FFN_LOOKAHEAD))
        a0 = c * FFN_CHUNK
        b0 = D_FF + a0
        za, zb = z[:, 0:FFN_CHUNK], z[:, FFN_CHUNK:2 * FFN_CHUNK]
        pending.append(_conv_gate(za, zb, shift_fn(za, a0), shift_fn(zb, b0), cw_ref, cb_ref, a0, b0, FFN_CHUNK))
        if len(pending) == per_down:
            gated = pending[0] if per_down == 1 else jnp.concatenate(pending, axis=1)
            r0 = a0 + FFN_CHUNK - FFN_DOWN_K
            acc = acc + _dot(gated.astype(BF16), wdn_ref[r0:r0 + FFN_DOWN_K, :])
            pending = []
    if final:
        acc = _rmsnorm(acc, nfin_ref[...])
    return acc


def _ffn_kernel(xp_ref, xs_ref, nf_ref, wup_ref, cw_ref, cb_ref, wdn_ref, nfin_ref, st_ref,
                xpo_ref, xso_ref, ctp_ref, cts_ref, carry_ref, *, final, prompt_steps, steps_per_seq):
    step = pl.program_id(0)
    weights = (nf_ref, wup_ref, cw_ref, cb_ref, wdn_ref, nfin_ref)

    @pl.when(step < prompt_steps)
    def _():
        sub = carry_ref.shape[0]

        @pl.when(step % steps_per_seq == 0)
        def _():
            carry_ref[...] = jnp.zeros_like(carry_ref)

        def shift_fn(z, c0):
            rows = z.shape[0]
            cols = slice(c0, c0 + z.shape[1])
            ext = jnp.concatenate([carry_ref[:, cols], z], axis=0)
            last = z[rows - sub:rows]
            carry_ref[:, cols] = last
            ctp_ref[:, cols] = pltpu.roll(last, CONV_W - 1, 0)[0:CONV_W - 1]
            return pltpu.roll(ext, 1, 0)[sub:], pltpu.roll(ext, 2, 0)[sub:]

        def sub_tile(n, carry):
            r0 = pl.multiple_of(n * FFN_SUBTILE, FFN_SUBTILE)
            xpo_ref[pl.ds(r0, FFN_SUBTILE), :] = _ffn_body(
                xp_ref[pl.ds(r0, FFN_SUBTILE), :], *weights, shift_fn, final)
            return carry

        lax.fori_loop(0, xp_ref.shape[0] // FFN_SUBTILE, sub_tile, 0)

    @pl.when(step >= prompt_steps)
    def _():
        n_seq = st_ref.shape[0]
        m_rows = xs_ref.shape[0]
        t_new = m_rows // n_seq

        def shift_fn(z, c0):
            width = z.shape[1]
            cols = slice(c0, c0 + width)
            z3 = z.reshape(n_seq, t_new, width)
            st = st_ref[:, :, cols]
            p0 = jnp.broadcast_to(st[:, 0:1, :], z3.shape)
            p1 = jnp.broadcast_to(st[:, 1:2, :], z3.shape)
            t = lax.broadcasted_iota(jnp.int32, z3.shape, 1)
            r1 = pltpu.roll(z3, 1, 1)
            r2 = pltpu.roll(z3, 2, 1)
            cts_ref[:, :, cols] = r2[:, 0:CONV_W - 1, :]
            z1 = jnp.where(t == 0, p1, r1)
            z2 = jnp.where(t == 0, p0, jnp.where(t == 1, p1, r2))
            return z1.reshape(m_rows, width), z2.reshape(m_rows, width)

        xso_ref[...] = _ffn_body(xs_ref[...], *weights, shift_fn, final)


def _drop_aliased(kernel_fn, n_in, n_aliased):
    def wrapped(*refs):
        return kernel_fn(*refs[:n_in], *refs[n_in + n_aliased:])
    return wrapped


def _layer_spec(layer, shape):
    nd = len(shape)
    return pl.BlockSpec((None,) + tuple(shape), lambda *_: (layer,) + (0,) * nd, pipeline_mode=pl.Buffered(1))


_SMEM_SPEC = pl.BlockSpec(memory_space=pltpu.SMEM)
_ANY_SPEC = pl.BlockSpec(memory_space=pl.ANY)


def _compiler_params(n_axes):
    return pltpu.CompilerParams(dimension_semantics=("arbitrary",) * n_axes,
                                vmem_limit_bytes=VMEM_LIMIT_BYTES)


def _mixer_weight_specs(layer):
    return [
        _layer_spec(layer, (1, D_MODEL)),
        _layer_spec(layer, (D_MODEL, ATTN_W)),
        _layer_spec(layer, (D_MODEL, REST_COLS)),
        _SMEM_SPEC,
        _layer_spec(layer, (1, GMLP_W)),
        _layer_spec(layer, (1, GMLP_W)),
        _layer_spec(layer, (GMLP_GROUPS, CHUNK, CHUNK)),
        _layer_spec(layer, (GMLP_GROUPS, CHUNK, 1)),
        _layer_spec(layer, (ATTN_W, D_MODEL)),
        _layer_spec(layer, (GMLP_W, D_MODEL)),
        _layer_spec(layer, (D_MODEL, D_MODEL)),
    ]


def _ffn_weight_specs(layer):
    return [
        _layer_spec(layer, (1, D_MODEL)),
        _layer_spec(layer, (D_MODEL, 2 * D_FF)),
        _layer_spec(layer, (CONV_W, 2 * D_FF)),
        _layer_spec(layer, (1, 2 * D_FF)),
        _layer_spec(layer, (D_FF, D_MODEL)),
        pl.BlockSpec((1, D_MODEL), lambda *_: (0, 0), pipeline_mode=pl.Buffered(1)),
    ]


def _call_layer(kernel_fn, grid, inputs, in_specs, out_specs, out_shapes, carried, scratch, name):
    n_in = len(inputs)
    first_carried = len(out_shapes) - len(carried)
    aliases = {n_in + n: first_carried + n for n in range(len(carried))}
    return pl.pallas_call(
        _drop_aliased(kernel_fn, n_in, len(carried)), grid=grid,
        in_specs=list(in_specs) + [_ANY_SPEC] * len(carried), out_specs=out_specs, out_shape=out_shapes,
        scratch_shapes=scratch, input_output_aliases=aliases,
        compiler_params=_compiler_params(len(grid)), name=name,
    )(*inputs, *carried)


def _mixer_prompt(layer, x, mixer_w, k_tails, v_tails):
    batch, seq, _ = x.shape
    tm = TM_PROMPT
    tile = pl.BlockSpec((None, tm, D_MODEL), lambda b, i: (b, i, 0))
    prev_block = pl.BlockSpec((None, BLOCK, D_MODEL), lambda b, i: (b, jnp.maximum(i * (tm // BLOCK) - 1, 0), 0))
    tail = pl.BlockSpec((None, None, WINDOW, KV_W), lambda b, i: (layer, b, 0, 0))
    tail_shape = jax.ShapeDtypeStruct(k_tails.shape, F32)
    return _call_layer(
        _mixer_prompt_kernel, (batch, seq // tm), (x, x) + mixer_w,
        [tile, prev_block] + _mixer_weight_specs(layer), [tile, tail, tail],
        [jax.ShapeDtypeStruct(x.shape, F32), tail_shape, tail_shape], [k_tails, v_tails], [], "mixer_prompt")


def _mixer_sample(layer, x, mixer_w, cache_k, cache_v, win_k, win_v, vn_all):
    n_rows = x.shape[0]
    _, n_seq_total, _, past = cache_k.shape
    t_new = n_rows // n_seq_total
    ns = SEQ_PER_STEP_MIXER
    m = ns * t_new
    tile = pl.BlockSpec((m, D_MODEL), lambda i: (i, 0))
    win = pl.BlockSpec((None, ns, KV_W, past), lambda i: (layer, i, 0, 0))
    vn_spec = pl.BlockSpec((None, m, GMLP_W), lambda i: (layer, i, 0))
    return _call_layer(
        _mixer_sample_kernel, (n_seq_total // ns,), (x,) + mixer_w + (cache_k, cache_v),
        [tile] + _mixer_weight_specs(layer) + [win, win], [tile, win, win, vn_spec],
        [jax.ShapeDtypeStruct(x.shape, F32), jax.ShapeDtypeStruct(win_k.shape, F32),
         jax.ShapeDtypeStruct(win_v.shape, F32), jax.ShapeDtypeStruct(vn_all.shape, F32)],
        [win_k, win_v, vn_all], [], "mixer_sample")


def _ffn(layer, xp, xs, ffn_w, state, conv_tails, new_state):
    batch, seq, _ = xp.shape
    n_rows_s = xs.shape[0]
    n_seq_total = state.shape[1]
    t_new = n_rows_s // n_seq_total
    tm = TM_PROMPT_FFN
    ns = SEQ_PER_STEP_FFN
    m = ns * t_new
    steps_per_seq = seq // tm
    prompt_steps = batch * steps_per_seq
    sample_steps = n_seq_total // ns

    def p_idx(s):
        return jnp.minimum(s, prompt_steps - 1)

    def s_idx(s):
        return jnp.maximum(s - prompt_steps, 0)

    tile_p = pl.BlockSpec((tm, D_MODEL), lambda s: (p_idx(s), 0))
    tile_s = pl.BlockSpec((m, D_MODEL), lambda s: (s_idx(s), 0))
    st = pl.BlockSpec((None, ns, CONV_W - 1, 2 * D_FF), lambda s: (layer, s_idx(s), 0, 0))
    tail_p = pl.BlockSpec((None, None, CONV_W - 1, 2 * D_FF), lambda s: (layer, p_idx(s) // steps_per_seq, 0, 0))
    xp2 = xp.reshape(batch * seq, D_MODEL)
    kernel_fn = functools.partial(_ffn_kernel, final=layer == DEPTH - 1, prompt_steps=prompt_steps,
                                  steps_per_seq=steps_per_seq)
    xpo, xso, conv_tails, new_state = _call_layer(
        kernel_fn, (prompt_steps + sample_steps,), (xp2, xs) + ffn_w + (state,),
        [tile_p, tile_s] + _ffn_weight_specs(layer) + [st], [tile_p, tile_s, tail_p, st],
        [jax.ShapeDtypeStruct(xp2.shape, F32), jax.ShapeDtypeStruct(xs.shape, F32),
         jax.ShapeDtypeStruct(conv_tails.shape, F32), jax.ShapeDtypeStruct(new_state.shape, F32)],
        [conv_tails, new_state], [pltpu.VMEM((F32_SUBLANES, 2 * D_FF), F32)], "ffn")
    return xpo.reshape(xp.shape), xso, conv_tails, new_state


def _heads_group_major(w, axis):
    shape = w.shape
    split = shape[:axis] + (N_KV_HEADS, GQA_GROUP, HEAD_DIM) + shape[axis + 1:]
    return jnp.swapaxes(w.reshape(split), axis, axis + 1).reshape(shape)


def kernel(x_prompt, x_sample, cache_win_k, cache_win_v, state_conv, norm_mix, w_in, sinks, gmlp_ln_g, gmlp_ln_b, gmlp_ws, gmlp_bs, w_branch_attn, w_branch_gmlp, w_out, norm_ffn, w_up, conv_w, conv_b, w_down, norm_final):
    batch = x_prompt.shape[0]
    dec_batch, dec_seq, _ = x_sample.shape
    w_in_b = w_in.astype(BF16)
    w_q_b = _heads_group_major(w_in_b[:, :, :ATTN_W], 2)
    w_pa_b = _heads_group_major(w_branch_attn, 1).astype(BF16)
    mixer_shared = (norm_mix.reshape(DEPTH, 1, D_MODEL), w_q_b, w_in_b)
    mixer_ln = (gmlp_ln_g.reshape(DEPTH, 1, GMLP_W), gmlp_ln_b.reshape(DEPTH, 1, GMLP_W))
    mixer_tail = (w_pa_b, w_branch_gmlp.astype(BF16), w_out.astype(BF16))
    gmlp_w = (gmlp_ws, gmlp_bs.reshape(DEPTH, GMLP_GROUPS, CHUNK, 1))
    ffn_w = (norm_ffn.reshape(DEPTH, 1, D_MODEL), w_up.astype(BF16), conv_w, conv_b.reshape(DEPTH, 1, 2 * D_FF),
             w_down.astype(BF16), norm_final.reshape(1, D_MODEL))
    cache_k = jnp.transpose(cache_win_k, (0, 1, 3, 4, 2)).reshape(DEPTH, dec_batch, KV_W, WINDOW)
    cache_v = jnp.transpose(cache_win_v, (0, 1, 3, 4, 2)).reshape(DEPTH, dec_batch, KV_W, WINDOW)

    xp = x_prompt
    xs = x_sample.reshape(dec_batch * dec_seq, D_MODEL)
    kp = jnp.zeros((DEPTH, batch, WINDOW, KV_W), F32)
    vp = jnp.zeros((DEPTH, batch, WINDOW, KV_W), F32)
    cp = jnp.zeros((DEPTH, batch, CONV_W - 1, 2 * D_FF), F32)
    gv = jnp.zeros((DEPTH, dec_batch * dec_seq, GMLP_W), F32)
    ks = jnp.zeros(cache_k.shape, F32)
    vs = jnp.zeros(cache_v.shape, F32)
    cs = jnp.zeros(state_conv.shape, F32)
    for l in range(DEPTH):
        sink_l = sinks[l]
        xp, kp, vp = _mixer_prompt(l, xp, mixer_shared + (sink_l,) + mixer_ln + gmlp_w + mixer_tail, kp, vp)
        xs, ks, vs, gv = _mixer_sample(l, xs, mixer_shared + (sink_l,) + mixer_ln + gmlp_w + mixer_tail,
                                       cache_k, cache_v, ks, vs, gv)
        xp, xs, cp, cs = _ffn(l, xp, xs, ffn_w, state_conv, cp, cs)

    kv_prompt = (DEPTH, batch, WINDOW, N_KV_HEADS, HEAD_DIM)
    kv_sample_t = (DEPTH, dec_batch, N_KV_HEADS, HEAD_DIM, WINDOW)
    return (xp, xs.reshape(x_sample.shape),
            kp.reshape(kv_prompt), vp.reshape(kv_prompt), cp,
            jnp.transpose(ks.reshape(kv_sample_t), (0, 1, 4, 2, 3)),
            jnp.transpose(vs.reshape(kv_sample_t), (0, 1, 4, 2, 3)), cs,
            gv.reshape(DEPTH, dec_batch, dec_seq, GMLP_W))
```

```python
import functools

import jax
import jax.numpy as jnp
import numpy as np
from jax import lax
from jax.experimental import pallas as pl
from jax.experimental.pallas import tpu as pltpu

D_MODEL = 1024
DEPTH = 4
HEAD_DIM = 64
N_HEADS = 16
N_KV_HEADS = 4
GQA_GROUP = N_HEADS // N_KV_HEADS
ATTN_W = N_HEADS * HEAD_DIM
KV_W = N_KV_HEADS * HEAD_DIM
WINDOW = 128
BLOCK = 128
CHUNK = 128
GMLP_CH = 128
GMLP_GROUPS = 6
GMLP_W = GMLP_GROUPS * GMLP_CH
D_FF = 2816
CONV_W = 3
EPS = 1e-5
NEG = -1e30

K0, V0, GU0, GV0, GA0, GB0, REST_COLS = 1024, 1280, 1536, 2304, 3072, 4096, 5120

V7X_VMEM_BYTES = 64 * 1024 * 1024
VMEM_LIMIT_BYTES = V7X_VMEM_BYTES - 8 * 1024 * 1024
F32_SUBLANES = 8

TM_PROMPT = 512
TM_PROMPT_FFN = 512
FFN_SUBTILE = 256
SEQ_PER_STEP_MIXER = 16
SEQ_PER_STEP_FFN = 32
FFN_CHUNK = 128
FFN_DOWN_K = 256
FFN_LOOKAHEAD = 8
ATTN_LOOKAHEAD = 2
PROJ_TILE = 256

F32 = jnp.float32
BF16 = jnp.bfloat16


def _rmsnorm(x, g):
    return x * lax.rsqrt(jnp.mean(x * x, axis=-1, keepdims=True) + EPS) * g


def _layernorm(x, g, b):
    mu = jnp.mean(x, axis=-1, keepdims=True)
    xc = x - mu
    var = jnp.mean(xc * xc, axis=-1, keepdims=True)
    return xc * lax.rsqrt(var + EPS) * g + b


def _gelu(x):
    c = np.sqrt(2.0 / np.pi).astype(np.float32)
    return x * (0.5 * (1.0 + jnp.tanh(c * (x + 0.044715 * (x * x * x)))))


def _sigmoid(x):
    return 1.0 / (1.0 + jnp.exp(-x))


def _dot(a, b):
    return jnp.dot(a, b, preferred_element_type=F32)


def _dot_nt(a, b):
    return lax.dot_general(a, b, (((1,), (1,)), ((), ())), preferred_element_type=F32)


def _kv_lane_masks(rows):
    lane = lax.broadcasted_iota(jnp.int32, (rows, KV_W), 1)
    return [(lane >= h * HEAD_DIM) & (lane < (h + 1) * HEAD_DIM) for h in range(N_KV_HEADS)]


def _select_kv_lanes(masks, parts):
    out = parts[N_KV_HEADS - 1]
    for h in range(N_KV_HEADS - 2, -1, -1):
        out = jnp.where(masks[h], parts[h], out)
    return out


def _softmax_pv(s, sink, vv):
    m = jnp.maximum(jnp.max(s, axis=-1, keepdims=True), sink)
    p = jnp.exp(s - m)
    denom = jnp.sum(p, axis=-1, keepdims=True) + jnp.exp(sink - m)
    return _dot(p.astype(BF16), vv) / denom


def _gmlp_mix(vn, ws_ref, bs_ref, period):
    vnb = vn.astype(BF16)
    row = lax.broadcasted_iota(jnp.int32, (CHUNK, CHUNK), 0)
    col = lax.broadcasted_iota(jnp.int32, (CHUNK, CHUNK), 1)
    keep = (col <= row) & (col >= row - (row & (period - 1)))
    reps = CHUNK // period
    w, bias = [], []
    for g in range(GMLP_GROUPS):
        if reps == 1:
            wg, bg = ws_ref[g], bs_ref[g]
        else:
            top = ws_ref[g, 0:period, :]
            wg = jnp.concatenate([pltpu.roll(top, i * period, 1) if i else top for i in range(reps)], axis=0)
            bg = jnp.concatenate([bs_ref[g, 0:period, :]] * reps, axis=0)
        w.append(jnp.where(keep, wg, 0.0).astype(BF16))
        bias.append(bg)
    rows = []
    for c in range(vn.shape[0] // CHUNK):
        cols = []
        for g in range(GMLP_GROUPS):
            blk = vnb[c * CHUNK:(c + 1) * CHUNK, g * GMLP_CH:(g + 1) * GMLP_CH]
            cols.append(_dot(w[g], blk) + bias[g])
        rows.append(jnp.concatenate(cols, axis=1))
    return rows[0] if len(rows) == 1 else jnp.concatenate(rows, axis=0)


class _GateProjection:
    def __init__(self, hb, wr_ref):
        self.hb, self.wr_ref, self.tiles = hb, wr_ref, []

    def issue(self, n_tiles):
        for _ in range(n_tiles):
            c0 = GU0 + len(self.tiles) * PROJ_TILE
            if c0 < REST_COLS:
                self.tiles.append(_dot(self.hb, self.wr_ref[:, c0:c0 + PROJ_TILE]))

    def columns(self, c0, c1):
        self.issue((REST_COLS - GU0) // PROJ_TILE)
        return jnp.concatenate(self.tiles[(c0 - GU0) // PROJ_TILE:(c1 - GU0) // PROJ_TILE], axis=1)


def _gate_merge_out(x, o, proj, lng_ref, lnb_ref, ws_ref, bs_ref, wpa_ref, wpb_ref, wout_ref, period):
    u = _gelu(proj.columns(GU0, GV0))
    vn = _layernorm(_gelu(proj.columns(GV0, GA0)), lng_ref[...], lnb_ref[...])
    ga = proj.columns(GA0, GB0)
    gb = proj.columns(GB0, REST_COLS)
    oa = _dot(o.astype(BF16), wpa_ref[...])
    s_gate = u * _gmlp_mix(vn, ws_ref, bs_ref, period)
    merged = _sigmoid(ga) * oa + _sigmoid(gb) * _dot(s_gate.astype(BF16), wpb_ref[...])
    return x + _dot(merged.astype(BF16), wout_ref[...]), vn


def _mixer_prompt_kernel(x_ref, xprev_ref, nm_ref, wq_ref, wr_ref, sink_ref, lng_ref, lnb_ref, ws_ref, bs_ref,
                         wpa_ref, wpb_ref, wout_ref,
                         xo_ref, kt_ref, vt_ref):
    i = pl.program_id(1)
    tm = x_ref.shape[0]
    x = x_ref[...]
    hb = _rmsnorm(x, nm_ref[...]).astype(BF16)

    hb_prev = _rmsnorm(xprev_ref[...], nm_ref[...]).astype(BF16)
    kv_prev = _dot(hb_prev, wr_ref[:, K0:GU0])
    k = _dot(hb, wr_ref[:, K0:V0])
    v = _dot(hb, wr_ref[:, V0:GU0])
    q = _dot(hb, wq_ref[...]) * (HEAD_DIM ** -0.5)
    proj = _GateProjection(hb, wr_ref)

    kk_all = jnp.concatenate([kv_prev[:, 0:KV_W], k], axis=0).astype(BF16)
    vv_all = jnp.concatenate([kv_prev[:, KV_W:2 * KV_W], v], axis=0).astype(BF16)
    kt_ref[...] = k[tm - WINDOW:tm]
    vt_ref[...] = v[tm - WINDOW:tm]

    rows = N_KV_HEADS * BLOCK
    t = lax.broadcasted_iota(jnp.int32, (rows, 2 * BLOCK), 0) & (BLOCK - 1)
    c = lax.broadcasted_iota(jnp.int32, (rows, 2 * BLOCK), 1)
    band = (c > t) & (c <= t + WINDOW)
    bias = jnp.where(band, 0.0, NEG).astype(F32)
    first_key = jnp.where(i > 0, 0, BLOCK)
    bias_first = jnp.where(band & (c >= first_key), 0.0, NEG).astype(F32)
    head_of_row = lax.broadcasted_iota(jnp.int32, (rows, 1), 0) >> (BLOCK.bit_length() - 1)
    masks = _kv_lane_masks(BLOCK)

    items = [(j, g) for j in range(tm // BLOCK) for g in range(GQA_GROUP)]

    def scores(item):
        j, g = item
        qg = q[j * BLOCK:(j + 1) * BLOCK, g * KV_W:(g + 1) * KV_W]
        lhs = jnp.concatenate([jnp.where(masks[h], qg, 0.0) for h in range(N_KV_HEADS)], axis=0).astype(BF16)
        return _dot_nt(lhs, kk_all[j * BLOCK:(j + 2) * BLOCK]) + (bias_first if j == 0 else bias)

    fill = -(-((REST_COLS - GU0) // PROJ_TILE) // len(items))
    o_groups = {}
    ahead = [scores(item) for item in items[:ATTN_LOOKAHEAD]]
    for n, (j, g) in enumerate(items):
        s = ahead.pop(0)
        if n + ATTN_LOOKAHEAD < len(items):
            ahead.append(scores(items[n + ATTN_LOOKAHEAD]))
        proj.issue(fill)
        sink = jnp.full((rows, 1), sink_ref[(N_KV_HEADS - 1) * GQA_GROUP + g], F32)
        for h in range(N_KV_HEADS - 2, -1, -1):
            sink = jnp.where(head_of_row == h, sink_ref[h * GQA_GROUP + g], sink)
        r = _softmax_pv(s, sink, vv_all[j * BLOCK:(j + 2) * BLOCK])
        o_groups[(j, g)] = _select_kv_lanes(masks, [r[h * BLOCK:(h + 1) * BLOCK] for h in range(N_KV_HEADS)])
    o_blocks = [jnp.concatenate([o_groups[(j, g)] for g in range(GQA_GROUP)], axis=1) for j in range(tm // BLOCK)]
    o = o_blocks[0] if len(o_blocks) == 1 else jnp.concatenate(o_blocks, axis=0)

    xo_ref[...], _ = _gate_merge_out(x, o, proj, lng_ref, lnb_ref, ws_ref, bs_ref, wpa_ref, wpb_ref, wout_ref, CHUNK)


def _mixer_sample_kernel(x_ref, nm_ref, wq_ref, wr_ref, sink_ref, lng_ref, lnb_ref, ws_ref, bs_ref,
                         wpa_ref, wpb_ref, wout_ref, ckt_ref, cvt_ref,
                         xo_ref, wkt_ref, wvt_ref, vn_ref):
    n_seq, _, past = ckt_ref.shape
    m_rows = x_ref.shape[0]
    t_new = m_rows // n_seq
    assert m_rows == past and past == BLOCK
    x = x_ref[...]
    hb = _rmsnorm(x, nm_ref[...]).astype(BF16)
    q = _dot(hb, wq_ref[...]) * (HEAD_DIM ** -0.5)
    k = _dot(hb, wr_ref[:, K0:V0])
    v = _dot(hb, wr_ref[:, V0:GU0])
    kt = jnp.transpose(k)
    vt = jnp.transpose(v)
    proj = _GateProjection(hb, wr_ref)

    n_keys = 2 * BLOCK
    rows = N_HEADS * t_new
    t = lax.broadcasted_iota(jnp.int32, (rows, n_keys), 0) & (t_new - 1)
    c = lax.broadcasted_iota(jnp.int32, (rows, n_keys), 1)
    diff = t + past - c
    bias = jnp.where((diff >= 0) & (diff < WINDOW) & (c < past + t_new), 0.0, NEG).astype(F32)
    head_slot = lax.broadcasted_iota(jnp.int32, (rows, 1), 0) >> (t_new.bit_length() - 1)
    sink = jnp.zeros((rows, 1), F32)
    for g in range(GQA_GROUP):
        for h in range(N_KV_HEADS):
            sink = jnp.where(head_slot == g * N_KV_HEADS + h, sink_ref[h * GQA_GROUP + g], sink)
    masks = _kv_lane_masks(t_new)
    zero_rows = jnp.zeros((BLOCK - t_new, KV_W), F32)
    keep_old = lax.broadcasted_iota(jnp.int32, (KV_W, past), 1) < past - t_new

    def new_window(old_t, new_t, b):
        shift = (past - t_new - b * t_new) % past
        placed = pltpu.roll(new_t, shift, 1) if shift else new_t
        return jnp.where(keep_old, pltpu.roll(old_t, past - t_new, 1), placed)

    s_list, v_list = [], []
    for b in range(n_seq):
        kct = ckt_ref[b]
        vct = cvt_ref[b]
        wkt_ref[b] = new_window(kct, kt, b)
        wvt_ref[b] = new_window(vct, vt, b)
        rows_b = slice(b * t_new, (b + 1) * t_new)
        k_pad = jnp.concatenate([k[rows_b], zero_rows], axis=0).astype(BF16)
        v_pad = jnp.concatenate([v[rows_b], zero_rows], axis=0).astype(BF16)
        qb = q[rows_b]
        lhs = jnp.concatenate(
            [jnp.where(masks[h], qb[:, g * KV_W:(g + 1) * KV_W], 0.0)
             for g in range(GQA_GROUP) for h in range(N_KV_HEADS)], axis=0).astype(BF16)
        s_list.append(jnp.concatenate([_dot(lhs, kct.astype(BF16)), _dot_nt(lhs, k_pad)], axis=1) + bias)
        v_list.append((vct.astype(BF16), v_pad))
    proj.issue((REST_COLS - GU0) // PROJ_TILE)
    o_rows = []
    for b in range(n_seq):
        s = s_list[b]
        vct_b, v_pad = v_list[b]
        m = jnp.maximum(jnp.max(s, axis=-1, keepdims=True), sink)
        p = jnp.exp(s - m)
        denom = jnp.sum(p, axis=-1, keepdims=True) + jnp.exp(sink - m)
        pb = p.astype(BF16)
        r = (_dot_nt(pb[:, 0:past], vct_b) + _dot(pb[:, past:n_keys], v_pad)) / denom
        o_groups = []
        for g in range(GQA_GROUP):
            base = g * N_KV_HEADS * t_new
            o_groups.append(_select_kv_lanes(
                masks, [r[base + h * t_new:base + (h + 1) * t_new] for h in range(N_KV_HEADS)]))
        o_rows.append(jnp.concatenate(o_groups, axis=1))
    o = jnp.concatenate(o_rows, axis=0)

    xo_ref[...], vn_ref[...] = _gate_merge_out(x, o, proj, lng_ref, lnb_ref, ws_ref, bs_ref,
                                               wpa_ref, wpb_ref, wout_ref, t_new)


def _conv_gate(za, zb, sa, sb, cw_ref, cb_ref, a0, b0, width):
    def conv(z, shifted, c0):
        z1, z2 = shifted
        cols = slice(c0, c0 + width)
        return cb_ref[:, cols] + ((z2 * cw_ref[0:1, cols] + z1 * cw_ref[1:2, cols]) + z * cw_ref[2:3, cols])

    a = conv(za, sa, a0)
    b = conv(zb, sb, b0)
    c0 = np.sqrt(2.0 / np.pi).astype(np.float32)
    c1 = np.float32(c0 * np.float32(0.044715))
    th = jnp.tanh(a * (c1 * (a * a) + c0))
    half_ab = (0.5 * a) * b
    return half_ab + half_ab * th


def _ffn_body(xs, nf_ref, wup_ref, cw_ref, cb_ref, wdn_ref, nfin_ref, shift_fn, final):
    hbs = [_rmsnorm(x, nf_ref[...]).astype(BF16) for x in xs]
    n_chunks = D_FF // FFN_CHUNK
    per_down = FFN_DOWN_K // FFN_CHUNK
    stages = [(n, c) for n in range(len(xs)) for c in range(n_chunks)]

    def up(stage):
        n, c = stage
        a0 = c * FFN_CHUNK
        w = jnp.concatenate([wup_ref[:, a0:a0 + FFN_CHUNK], wup_ref[:, D_FF + a0:D_FF + a0 + FFN_CHUNK]], axis=1)
        return _dot(hbs[n], w)

    accs = list(xs)
    ahead = [up(stage) for stage in stages[:FFN_LOOKAHEAD]]
    pending = []
    for i, (n, c) in enumerate(stages):
        z = ahead.pop(0)
        if i + FFN_LOOKAHEAD < len(stages):
            ahead.append(up(stages[i + FFN_LOOKAHEAD]))
        a0 = c * FFN_CHUNK
        b0 = D_FF + a0
        za, zb = z[:, 0:FFN_CHUNK], z[:, FFN_CHUNK:2 * FFN_CHUNK]
        pending.append(_conv_gate(za, zb, shift_fn(za, a0), shift_fn(zb, b0), cw_ref, cb_ref, a0, b0, FFN_CHUNK))
        if len(pending) == per_down:
            gated = pending[0] if per_down == 1 else jnp.concatenate(pending, axis=1)
            r0 = a0 + FFN_CHUNK - FFN_DOWN_K
            accs[n] = accs[n] + _dot(gated.astype(BF16), wdn_ref[r0:r0 + FFN_DOWN_K, :])
            pending = []
    if final:
        accs = [_rmsnorm(acc, nfin_ref[...]) for acc in accs]
    return accs


def _ffn_kernel(xp_ref, xs_ref, nf_ref, wup_ref, cw_ref, cb_ref, wdn_ref, nfin_ref, st_ref,
                xpo_ref, xso_ref, ctp_ref, cts_ref, carry_ref, *, final, prompt_steps, steps_per_seq):
    step = pl.program_id(0)
    weights = (nf_ref, wup_ref, cw_ref, cb_ref, wdn_ref, nfin_ref)

    @pl.when(step < prompt_steps)
    def _():
        sub = carry_ref.shape[0]

        @pl.when(step % steps_per_seq == 0)
        def _():
            carry_ref[...] = jnp.zeros_like(carry_ref)

        def shift_fn(z, c0):
            rows = z.shape[0]
            cols = slice(c0, c0 + z.shape[1])
            ext = jnp.concatenate([carry_ref[:, cols], z], axis=0)
            last = z[rows - sub:rows]
            carry_ref[:, cols] = last
            ctp_ref[:, cols] = pltpu.roll(last, CONV_W - 1, 0)[0:CONV_W - 1]
            return pltpu.roll(ext, 1, 0)[sub:], pltpu.roll(ext, 2, 0)[sub:]

        tiles = [slice(r0, r0 + FFN_SUBTILE) for r0 in range(0, xp_ref.shape[0], FFN_SUBTILE)]
        outs = _ffn_body([xp_ref[rows, :] for rows in tiles], *weights, shift_fn, final)
        for rows, out in zip(tiles, outs):
            xpo_ref[rows, :] = out

    @pl.when(step >= prompt_steps)
    def _():
        n_seq = st_ref.shape[0]
        m_rows = xs_ref.shape[0]
        t_new = m_rows // n_seq

        def shift_fn(z, c0):
            width = z.shape[1]
            cols = slice(c0, c0 + width)
            z3 = z.reshape(n_seq, t_new, width)
            st = st_ref[:, :, cols]
            p0 = jnp.broadcast_to(st[:, 0:1, :], z3.shape)
            p1 = jnp.broadcast_to(st[:, 1:2, :], z3.shape)
            t = lax.broadcasted_iota(jnp.int32, z3.shape, 1)
            r1 = pltpu.roll(z3, 1, 1)
            r2 = pltpu.roll(z3, 2, 1)
            cts_ref[:, :, cols] = r2[:, 0:CONV_W - 1, :]
            z1 = jnp.where(t == 0, p1, r1)
            z2 = jnp.where(t == 0, p0, jnp.where(t == 1, p1, r2))
            return z1.reshape(m_rows, width), z2.reshape(m_rows, width)

        (xso_ref[...],) = _ffn_body([xs_ref[...]], *weights, shift_fn, final)


def _drop_aliased(kernel_fn, n_in, n_aliased):
    def wrapped(*refs):
        return kernel_fn(*refs[:n_in], *refs[n_in + n_aliased:])
    return wrapped


def _layer_spec(layer, shape):
    nd = len(shape)
    return pl.BlockSpec((None,) + tuple(shape), lambda *_: (layer,) + (0,) * nd, pipeline_mode=pl.Buffered(1))


_SMEM_SPEC = pl.BlockSpec(memory_space=pltpu.SMEM)
_ANY_SPEC = pl.BlockSpec(memory_space=pl.ANY)


def _compiler_params(n_axes):
    return pltpu.CompilerParams(dimension_semantics=("arbitrary",) * n_axes,
                                vmem_limit_bytes=VMEM_LIMIT_BYTES)


def _mixer_weight_specs(layer):
    return [
        _layer_spec(layer, (1, D_MODEL)),
        _layer_spec(layer, (D_MODEL, ATTN_W)),
        _layer_spec(layer, (D_MODEL, REST_COLS)),
        _SMEM_SPEC,
        _layer_spec(layer, (1, GMLP_W)),
        _layer_spec(layer, (1, GMLP_W)),
        _layer_spec(layer, (GMLP_GROUPS, CHUNK, CHUNK)),
        _layer_spec(layer, (GMLP_GROUPS, CHUNK, 1)),
        _layer_spec(layer, (ATTN_W, D_MODEL)),
        _layer_spec(layer, (GMLP_W, D_MODEL)),
        _layer_spec(layer, (D_MODEL, D_MODEL)),
    ]


def _ffn_weight_specs(layer):
    return [
        _layer_spec(layer, (1, D_MODEL)),
        _layer_spec(layer, (D_MODEL, 2 * D_FF)),
        _layer_spec(layer, (CONV_W, 2 * D_FF)),
        _layer_spec(layer, (1, 2 * D_FF)),
        _layer_spec(layer, (D_FF, D_MODEL)),
        pl.BlockSpec((1, D_MODEL), lambda *_: (0, 0), pipeline_mode=pl.Buffered(1)),
    ]


def _call_layer(kernel_fn, grid, inputs, in_specs, out_specs, out_shapes, carried, scratch, name):
    n_in = len(inputs)
    first_carried = len(out_shapes) - len(carried)
    aliases = {n_in + n: first_carried + n for n in range(len(carried))}
    return pl.pallas_call(
        _drop_aliased(kernel_fn, n_in, len(carried)), grid=grid,
        in_specs=list(in_specs) + [_ANY_SPEC] * len(carried), out_specs=out_specs, out_shape=out_shapes,
        scratch_shapes=scratch, input_output_aliases=aliases,
        compiler_params=_compiler_params(len(grid)), name=name,
    )(*inputs, *carried)


def _mixer_prompt(layer, x, mixer_w, k_tails, v_tails):
    batch, seq, _ = x.shape
    tm = TM_PROMPT
    tile = pl.BlockSpec((None, tm, D_MODEL), lambda b, i: (b, i, 0))
    prev_block = pl.BlockSpec((None, BLOCK, D_MODEL), lambda b, i: (b, jnp.maximum(i * (tm // BLOCK) - 1, 0), 0))
    tail = pl.BlockSpec((None, None, WINDOW, KV_W), lambda b, i: (layer, b, 0, 0))
    tail_shape = jax.ShapeDtypeStruct(k_tails.shape, F32)
    return _call_layer(
        _mixer_prompt_kernel, (batch, seq // tm), (x, x) + mixer_w,
        [tile, prev_block] + _mixer_weight_specs(layer), [tile, tail, tail],
        [jax.ShapeDtypeStruct(x.shape, F32), tail_shape, tail_shape], [k_tails, v_tails], [], "mixer_prompt")


def _mixer_sample(layer, x, mixer_w, cache_k, cache_v, win_k, win_v, vn_all):
    n_rows = x.shape[0]
    _, n_seq_total, _, past = cache_k.shape
    t_new = n_rows // n_seq_total
    ns = SEQ_PER_STEP_MIXER
    m = ns * t_new
    tile = pl.BlockSpec((m, D_MODEL), lambda i: (i, 0))
    win = pl.BlockSpec((None, ns, KV_W, past), lambda i: (layer, i, 0, 0))
    vn_spec = pl.BlockSpec((None, m, GMLP_W), lambda i: (layer, i, 0))
    return _call_layer(
        _mixer_sample_kernel, (n_seq_total // ns,), (x,) + mixer_w + (cache_k, cache_v),
        [tile] + _mixer_weight_specs(layer) + [win, win], [tile, win, win, vn_spec],
        [jax.ShapeDtypeStruct(x.shape, F32), jax.ShapeDtypeStruct(win_k.shape, F32),
         jax.ShapeDtypeStruct(win_v.shape, F32), jax.ShapeDtypeStruct(vn_all.shape, F32)],
        [win_k, win_v, vn_all], [], "mixer_sample")


def _ffn(layer, xp, xs, ffn_w, state, conv_tails, new_state):
    batch, seq, _ = xp.shape
    n_rows_s = xs.shape[0]
    n_seq_total = state.shape[1]
    t_new = n_rows_s // n_seq_total
    tm = TM_PROMPT_FFN
    ns = SEQ_PER_STEP_FFN
    m = ns * t_new
    steps_per_seq = seq // tm
    prompt_steps = batch * steps_per_seq
    sample_steps = n_seq_total // ns

    def p_idx(s):
        return jnp.minimum(s, prompt_steps - 1)

    def s_idx(s):
        return jnp.maximum(s - prompt_steps, 0)

    tile_p = pl.BlockSpec((tm, D_MODEL), lambda s: (p_idx(s), 0))
    tile_s = pl.BlockSpec((m, D_MODEL), lambda s: (s_idx(s), 0))
    st = pl.BlockSpec((None, ns, CONV_W - 1, 2 * D_FF), lambda s: (layer, s_idx(s), 0, 0))
    tail_p = pl.BlockSpec((None, None, CONV_W - 1, 2 * D_FF), lambda s: (layer, p_idx(s) // steps_per_seq, 0, 0))
    xp2 = xp.reshape(batch * seq, D_MODEL)
    kernel_fn = functools.partial(_ffn_kernel, final=layer == DEPTH - 1, prompt_steps=prompt_steps,
                                  steps_per_seq=steps_per_seq)
    xpo, xso, conv_tails, new_state = _call_layer(
        kernel_fn, (prompt_steps + sample_steps,), (xp2, xs) + ffn_w + (state,),
        [tile_p, tile_s] + _ffn_weight_specs(layer) + [st], [tile_p, tile_s, tail_p, st],
        [jax.ShapeDtypeStruct(xp2.shape, F32), jax.ShapeDtypeStruct(xs.shape, F32),
         jax.ShapeDtypeStruct(conv_tails.shape, F32), jax.ShapeDtypeStruct(new_state.shape, F32)],
        [conv_tails, new_state], [pltpu.VMEM((F32_SUBLANES, 2 * D_FF), F32)], "ffn")
    return xpo.reshape(xp.shape), xso, conv_tails, new_state


def _heads_group_major(w, axis):
    shape = w.shape
    split = shape[:axis] + (N_KV_HEADS, GQA_GROUP, HEAD_DIM) + shape[axis + 1:]
    return jnp.swapaxes(w.reshape(split), axis, axis + 1).reshape(shape)


def kernel(x_prompt, x_sample, cache_win_k, cache_win_v, state_conv, norm_mix, w_in, sinks, gmlp_ln_g, gmlp_ln_b, gmlp_ws, gmlp_bs, w_branch_attn, w_branch_gmlp, w_out, norm_ffn, w_up, conv_w, conv_b, w_down, norm_final):
    batch = x_prompt.shape[0]
    dec_batch, dec_seq, _ = x_sample.shape
    w_in_b = w_in.astype(BF16)
    w_q_b = _heads_group_major(w_in_b[:, :, :ATTN_W], 2)
    w_pa_b = _heads_group_major(w_branch_attn, 1).astype(BF16)
    mixer_shared = (norm_mix.reshape(DEPTH, 1, D_MODEL), w_q_b, w_in_b)
    mixer_ln = (gmlp_ln_g.reshape(DEPTH, 1, GMLP_W), gmlp_ln_b.reshape(DEPTH, 1, GMLP_W))
    mixer_tail = (w_pa_b, w_branch_gmlp.astype(BF16), w_out.astype(BF16))
    gmlp_w = (gmlp_ws, gmlp_bs.reshape(DEPTH, GMLP_GROUPS, CHUNK, 1))
    ffn_w = (norm_ffn.reshape(DEPTH, 1, D_MODEL), w_up.astype(BF16), conv_w, conv_b.reshape(DEPTH, 1, 2 * D_FF),
             w_down.astype(BF16), norm_final.reshape(1, D_MODEL))
    cache_k = jnp.transpose(cache_win_k, (0, 1, 3, 4, 2)).reshape(DEPTH, dec_batch, KV_W, WINDOW)
    cache_v = jnp.transpose(cache_win_v, (0, 1, 3, 4, 2)).reshape(DEPTH, dec_batch, KV_W, WINDOW)

    xp = x_prompt
    xs = x_sample.reshape(dec_batch * dec_seq, D_MODEL)
    kp = jnp.zeros((DEPTH, batch, WINDOW, KV_W), F32)
    vp = jnp.zeros((DEPTH, batch, WINDOW, KV_W), F32)
    cp = jnp.zeros((DEPTH, batch, CONV_W - 1, 2 * D_FF), F32)
    gv = jnp.zeros((DEPTH, dec_batch * dec_seq, GMLP_W), F32)
    ks = jnp.zeros(cache_k.shape, F32)
    vs = jnp.zeros(cache_v.shape, F32)
    cs = jnp.zeros(state_conv.shape, F32)
    for l in range(DEPTH):
        sink_l = sinks[l]
        xp, kp, vp = _mixer_prompt(l, xp, mixer_shared + (sink_l,) + mixer_ln + gmlp_w + mixer_tail, kp, vp)
        xs, ks, vs, gv = _mixer_sample(l, xs, mixer_shared + (sink_l,) + mixer_ln + gmlp_w + mixer_tail,
                                       cache_k, cache_v, ks, vs, gv)
        xp, xs, cp, cs = _ffn(l, xp, xs, ffn_w, state_conv, cp, cs)

    kv_prompt = (DEPTH, batch, WINDOW, N_KV_HEADS, HEAD_DIM)
    kv_sample_t = (DEPTH, dec_batch, N_KV_HEADS, HEAD_DIM, WINDOW)
    return (xp, xs.reshape(x_sample.shape),
            kp.reshape(kv_prompt), vp.reshape(kv_prompt), cp,
            jnp.transpose(ks.reshape(kv_sample_t), (0, 1, 4, 2, 3)),
            jnp.transpose(vs.reshape(kv_sample_t), (0, 1, 4, 2, 3)), cs,
            gv.reshape(DEPTH, dec_batch, dec_seq, GMLP_W))
```

```python
import functools

import jax
import jax.numpy as jnp
import numpy as np
from jax import lax
from jax.experimental import pallas as pl
from jax.experimental.pallas import tpu as pltpu

D_MODEL = 1024
DEPTH = 4
HEAD_DIM = 64
N_HEADS = 16
N_KV_HEADS = 4
GQA_GROUP = N_HEADS // N_KV_HEADS
ATTN_W = N_HEADS * HEAD_DIM
KV_W = N_KV_HEADS * HEAD_DIM
WINDOW = 128
BLOCK = 128
CHUNK = 128
GMLP_CH = 128
GMLP_GROUPS = 6
GMLP_W = GMLP_GROUPS * GMLP_CH
D_FF = 2816
CONV_W = 3
EPS = 1e-5
NEG = -1e30

K0, V0, GU0, GV0, GA0, GB0, REST_COLS = 1024, 1280, 1536, 2304, 3072, 4096, 5120

V7X_VMEM_BYTES = 64 * 1024 * 1024
VMEM_LIMIT_BYTES = V7X_VMEM_BYTES - 8 * 1024 * 1024
F32_SUBLANES = 8

TM_PROMPT = 512
TM_PROMPT_FFN = 1024
FFN_SUBTILE = 256
SEQ_PER_STEP_MIXER = 16
SEQ_PER_STEP_FFN = 32
FFN_CHUNK = 128
FFN_DOWN_K = 256
FFN_LOOKAHEAD = 8
ATTN_LOOKAHEAD = 2
PROJ_TILE = 256

F32 = jnp.float32
BF16 = jnp.bfloat16


def _rmsnorm(x, g):
    return x * lax.rsqrt(jnp.mean(x * x, axis=-1, keepdims=True) + EPS) * g


def _layernorm(x, g, b):
    mu = jnp.mean(x, axis=-1, keepdims=True)
    xc = x - mu
    var = jnp.mean(xc * xc, axis=-1, keepdims=True)
    return xc * lax.rsqrt(var + EPS) * g + b


def _gelu(x):
    c = np.sqrt(2.0 / np.pi).astype(np.float32)
    return x * (0.5 * (1.0 + jnp.tanh(c * (x + 0.044715 * (x * x * x)))))


def _sigmoid(x):
    return 1.0 / (1.0 + jnp.exp(-x))


def _dot(a, b):
    return jnp.dot(a, b, preferred_element_type=F32)


def _dot_nt(a, b):
    return lax.dot_general(a, b, (((1,), (1,)), ((), ())), preferred_element_type=F32)


def _kv_lane_masks(rows):
    lane = lax.broadcasted_iota(jnp.int32, (rows, KV_W), 1)
    return [(lane >= h * HEAD_DIM) & (lane < (h + 1) * HEAD_DIM) for h in range(N_KV_HEADS)]


def _select_kv_lanes(masks, parts):
    out = parts[N_KV_HEADS - 1]
    for h in range(N_KV_HEADS - 2, -1, -1):
        out = jnp.where(masks[h], parts[h], out)
    return out


def _softmax_pv(s, sink, vv):
    m = jnp.maximum(jnp.max(s, axis=-1, keepdims=True), sink)
    p = jnp.exp(s - m)
    denom = jnp.sum(p, axis=-1, keepdims=True) + jnp.exp(sink - m)
    return _dot(p.astype(BF16), vv) / denom


def _gmlp_mix(vn, ws_ref, bs_ref, period):
    vnb = vn.astype(BF16)
    row = lax.broadcasted_iota(jnp.int32, (CHUNK, CHUNK), 0)
    col = lax.broadcasted_iota(jnp.int32, (CHUNK, CHUNK), 1)
    keep = (col <= row) & (col >= row - (row & (period - 1)))
    reps = CHUNK // period
    w, bias = [], []
    for g in range(GMLP_GROUPS):
        if reps == 1:
            wg, bg = ws_ref[g], bs_ref[g]
        else:
            top = ws_ref[g, 0:period, :]
            wg = jnp.concatenate([pltpu.roll(top, i * period, 1) if i else top for i in range(reps)], axis=0)
            bg = jnp.concatenate([bs_ref[g, 0:period, :]] * reps, axis=0)
        w.append(jnp.where(keep, wg, 0.0).astype(BF16))
        bias.append(bg)
    rows = []
    for c in range(vn.shape[0] // CHUNK):
        cols = []
        for g in range(GMLP_GROUPS):
            blk = vnb[c * CHUNK:(c + 1) * CHUNK, g * GMLP_CH:(g + 1) * GMLP_CH]
            cols.append(_dot(w[g], blk) + bias[g])
        rows.append(jnp.concatenate(cols, axis=1))
    return rows[0] if len(rows) == 1 else jnp.concatenate(rows, axis=0)


class _GateProjection:
    def __init__(self, hb, wr_ref):
        self.hb, self.wr_ref, self.tiles = hb, wr_ref, []

    def issue(self, n_tiles):
        for _ in range(n_tiles):
            c0 = GU0 + len(self.tiles) * PROJ_TILE
            if c0 < REST_COLS:
                self.tiles.append(_dot(self.hb, self.wr_ref[:, c0:c0 + PROJ_TILE]))

    def columns(self, c0, c1):
        self.issue((REST_COLS - GU0) // PROJ_TILE)
        return jnp.concatenate(self.tiles[(c0 - GU0) // PROJ_TILE:(c1 - GU0) // PROJ_TILE], axis=1)


def _gate_merge_out(x, o, proj, lng_ref, lnb_ref, ws_ref, bs_ref, wpa_ref, wpb_ref, wout_ref, period):
    u = _gelu(proj.columns(GU0, GV0))
    vn = _layernorm(_gelu(proj.columns(GV0, GA0)), lng_ref[...], lnb_ref[...])
    ga = proj.columns(GA0, GB0)
    gb = proj.columns(GB0, REST_COLS)
    oa = _dot(o.astype(BF16), wpa_ref[...])
    s_gate = u * _gmlp_mix(vn, ws_ref, bs_ref, period)
    merged = _sigmoid(ga) * oa + _sigmoid(gb) * _dot(s_gate.astype(BF16), wpb_ref[...])
    return x + _dot(merged.astype(BF16), wout_ref[...]), vn


def _mixer_prompt_kernel(x_ref, xprev_ref, nm_ref, wq_ref, wr_ref, sink_ref, lng_ref, lnb_ref, ws_ref, bs_ref,
                         wpa_ref, wpb_ref, wout_ref,
                         xo_ref, kt_ref, vt_ref):
    i = pl.program_id(1)
    tm = x_ref.shape[0]
    x = x_ref[...]
    hb = _rmsnorm(x, nm_ref[...]).astype(BF16)

    hb_prev = _rmsnorm(xprev_ref[...], nm_ref[...]).astype(BF16)
    kv_prev = _dot(hb_prev, wr_ref[:, K0:GU0])
    k = _dot(hb, wr_ref[:, K0:V0])
    v = _dot(hb, wr_ref[:, V0:GU0])
    q = _dot(hb, wq_ref[...]) * (HEAD_DIM ** -0.5)
    proj = _GateProjection(hb, wr_ref)

    kk_all = jnp.concatenate([kv_prev[:, 0:KV_W], k], axis=0).astype(BF16)
    vv_all = jnp.concatenate([kv_prev[:, KV_W:2 * KV_W], v], axis=0).astype(BF16)
    kt_ref[...] = k[tm - WINDOW:tm]
    vt_ref[...] = v[tm - WINDOW:tm]

    rows = N_KV_HEADS * BLOCK
    t = lax.broadcasted_iota(jnp.int32, (rows, 2 * BLOCK), 0) & (BLOCK - 1)
    c = lax.broadcasted_iota(jnp.int32, (rows, 2 * BLOCK), 1)
    band = (c > t) & (c <= t + WINDOW)
    bias = jnp.where(band, 0.0, NEG).astype(F32)
    first_key = jnp.where(i > 0, 0, BLOCK)
    bias_first = jnp.where(band & (c >= first_key), 0.0, NEG).astype(F32)
    head_of_row = lax.broadcasted_iota(jnp.int32, (rows, 1), 0) >> (BLOCK.bit_length() - 1)
    masks = _kv_lane_masks(BLOCK)

    items = [(j, g) for j in range(tm // BLOCK) for g in range(GQA_GROUP)]

    def scores(item):
        j, g = item
        qg = q[j * BLOCK:(j + 1) * BLOCK, g * KV_W:(g + 1) * KV_W]
        lhs = jnp.concatenate([jnp.where(masks[h], qg, 0.0) for h in range(N_KV_HEADS)], axis=0).astype(BF16)
        return _dot_nt(lhs, kk_all[j * BLOCK:(j + 2) * BLOCK]) + (bias_first if j == 0 else bias)

    fill = -(-((REST_COLS - GU0) // PROJ_TILE) // len(items))
    o_groups = {}
    ahead = [scores(item) for item in items[:ATTN_LOOKAHEAD]]
    for n, (j, g) in enumerate(items):
        s = ahead.pop(0)
        if n + ATTN_LOOKAHEAD < len(items):
            ahead.append(scores(items[n + ATTN_LOOKAHEAD]))
        proj.issue(fill)
        sink = jnp.full((rows, 1), sink_ref[(N_KV_HEADS - 1) * GQA_GROUP + g], F32)
        for h in range(N_KV_HEADS - 2, -1, -1):
            sink = jnp.where(head_of_row == h, sink_ref[h * GQA_GROUP + g], sink)
        r = _softmax_pv(s, sink, vv_all[j * BLOCK:(j + 2) * BLOCK])
        o_groups[(j, g)] = _select_kv_lanes(masks, [r[h * BLOCK:(h + 1) * BLOCK] for h in range(N_KV_HEADS)])
    o_blocks = [jnp.concatenate([o_groups[(j, g)] for g in range(GQA_GROUP)], axis=1) for j in range(tm // BLOCK)]
    o = o_blocks[0] if len(o_blocks) == 1 else jnp.concatenate(o_blocks, axis=0)

    xo_ref[...], _ = _gate_merge_out(x, o, proj, lng_ref, lnb_ref, ws_ref, bs_ref, wpa_ref, wpb_ref, wout_ref, CHUNK)


def _mixer_sample_kernel(x_ref, nm_ref, wq_ref, wr_ref, sink_ref, lng_ref, lnb_ref, ws_ref, bs_ref,
                         wpa_ref, wpb_ref, wout_ref, ckt_ref, cvt_ref,
                         xo_ref, wkt_ref, wvt_ref, vn_ref):
    n_seq, _, past = ckt_ref.shape
    m_rows = x_ref.shape[0]
    t_new = m_rows // n_seq
    assert m_rows == past and past == BLOCK
    x = x_ref[...]
    hb = _rmsnorm(x, nm_ref[...]).astype(BF16)
    q = _dot(hb, wq_ref[...]) * (HEAD_DIM ** -0.5)
    k = _dot(hb, wr_ref[:, K0:V0])
    v = _dot(hb, wr_ref[:, V0:GU0])
    kt = jnp.transpose(k)
    vt = jnp.transpose(v)
    proj = _GateProjection(hb, wr_ref)

    n_keys = 2 * BLOCK
    rows = N_HEADS * t_new
    t = lax.broadcasted_iota(jnp.int32, (rows, n_keys), 0) & (t_new - 1)
    c = lax.broadcasted_iota(jnp.int32, (rows, n_keys), 1)
    diff = t + past - c
    bias = jnp.where((diff >= 0) & (diff < WINDOW) & (c < past + t_new), 0.0, NEG).astype(F32)
    head_slot = lax.broadcasted_iota(jnp.int32, (rows, 1), 0) >> (t_new.bit_length() - 1)
    sink = jnp.zeros((rows, 1), F32)
    for g in range(GQA_GROUP):
        for h in range(N_KV_HEADS):
            sink = jnp.where(head_slot == g * N_KV_HEADS + h, sink_ref[h * GQA_GROUP + g], sink)
    masks = _kv_lane_masks(t_new)
    zero_rows = jnp.zeros((BLOCK - t_new, KV_W), F32)
    keep_old = lax.broadcasted_iota(jnp.int32, (KV_W, past), 1) < past - t_new

    def new_window(old_t, new_t, b):
        shift = (past - t_new - b * t_new) % past
        placed = pltpu.roll(new_t, shift, 1) if shift else new_t
        return jnp.where(keep_old, pltpu.roll(old_t, past - t_new, 1), placed)

    s_list, v_list = [], []
    for b in range(n_seq):
        kct = ckt_ref[b]
        vct = cvt_ref[b]
        wkt_ref[b] = new_window(kct, kt, b)
        wvt_ref[b] = new_window(vct, vt, b)
        rows_b = slice(b * t_new, (b + 1) * t_new)
        k_pad = jnp.concatenate([k[rows_b], zero_rows], axis=0).astype(BF16)
        v_pad = jnp.concatenate([v[rows_b], zero_rows], axis=0).astype(BF16)
        qb = q[rows_b]
        lhs = jnp.concatenate(
            [jnp.where(masks[h], qb[:, g * KV_W:(g + 1) * KV_W], 0.0)
             for g in range(GQA_GROUP) for h in range(N_KV_HEADS)], axis=0).astype(BF16)
        s_list.append(jnp.concatenate([_dot(lhs, kct.astype(BF16)), _dot_nt(lhs, k_pad)], axis=1) + bias)
        v_list.append((vct.astype(BF16), v_pad))
    proj.issue((REST_COLS - GU0) // PROJ_TILE)
    o_rows = []
    for b in range(n_seq):
        s = s_list[b]
        vct_b, v_pad = v_list[b]
        m = jnp.maximum(jnp.max(s, axis=-1, keepdims=True), sink)
        p = jnp.exp(s - m)
        denom = jnp.sum(p, axis=-1, keepdims=True) + jnp.exp(sink - m)
        pb = p.astype(BF16)
        r = (_dot_nt(pb[:, 0:past], vct_b) + _dot(pb[:, past:n_keys], v_pad)) / denom
        o_groups = []
        for g in range(GQA_GROUP):
            base = g * N_KV_HEADS * t_new
            o_groups.append(_select_kv_lanes(
                masks, [r[base + h * t_new:base + (h + 1) * t_new] for h in range(N_KV_HEADS)]))
        o_rows.append(jnp.concatenate(o_groups, axis=1))
    o = jnp.concatenate(o_rows, axis=0)

    xo_ref[...], vn_ref[...] = _gate_merge_out(x, o, proj, lng_ref, lnb_ref, ws_ref, bs_ref,
                                               wpa_ref, wpb_ref, wout_ref, t_new)


def _conv_gate(za, zb, sa, sb, cw_ref, cb_ref, a0, b0, width):
    def conv(z, shifted, c0):
        z1, z2 = shifted
        cols = slice(c0, c0 + width)
        return cb_ref[:, cols] + ((z2 * cw_ref[0:1, cols] + z1 * cw_ref[1:2, cols]) + z * cw_ref[2:3, cols])

    a = conv(za, sa, a0)
    b = conv(zb, sb, b0)
    c0 = np.sqrt(2.0 / np.pi).astype(np.float32)
    c1 = np.float32(c0 * np.float32(0.044715))
    th = jnp.tanh(a * (c1 * (a * a) + c0))
    half_ab = (0.5 * a) * b
    return half_ab + half_ab * th


def _ffn_body(xs, nf_ref, wup_ref, cw_ref, cb_ref, wdn_ref, nfin_ref, shift_fn, final):
    hbs = [_rmsnorm(x, nf_ref[...]).astype(BF16) for x in xs]
    n_chunks = D_FF // FFN_CHUNK
    per_down = FFN_DOWN_K // FFN_CHUNK
    stages = [(n, c) for n in range(len(xs)) for c in range(n_chunks)]

    def up(stage):
        n, c = stage
        a0 = c * FFN_CHUNK
        w = jnp.concatenate([wup_ref[:, a0:a0 + FFN_CHUNK], wup_ref[:, D_FF + a0:D_FF + a0 + FFN_CHUNK]], axis=1)
        return _dot(hbs[n], w)

    accs = list(xs)
    ahead = [up(stage) for stage in stages[:FFN_LOOKAHEAD]]
    pending = []
    for i, (n, c) in enumerate(stages):
        z = ahead.pop(0)
        if i + FFN_LOOKAHEAD < len(stages):
            ahead.append(up(stages[i + FFN_LOOKAHEAD]))
        a0 = c * FFN_CHUNK
        b0 = D_FF + a0
        za, zb = z[:, 0:FFN_CHUNK], z[:, FFN_CHUNK:2 * FFN_CHUNK]
        pending.append(_conv_gate(za, zb, shift_fn(za, a0), shift_fn(zb, b0), cw_ref, cb_ref, a0, b0, FFN_CHUNK))
        if len(pending) == per_down:
            gated = pending[0] if per_down == 1 else jnp.concatenate(pending, axis=1)
            r0 = a0 + FFN_CHUNK - FFN_DOWN_K
            accs[n] = accs[n] + _dot(gated.astype(BF16), wdn_ref[r0:r0 + FFN_DOWN_K, :])
            pending = []
    if final:
        accs = [_rmsnorm(acc, nfin_ref[...]) for acc in accs]
    return accs


def _ffn_kernel(xp_ref, xs_ref, nf_ref, wup_ref, cw_ref, cb_ref, wdn_ref, nfin_ref, st_ref,
                xpo_ref, xso_ref, ctp_ref, cts_ref, carry_ref, *, final, prompt_steps, steps_per_seq):
    step = pl.program_id(0)
    weights = (nf_ref, wup_ref, cw_ref, cb_ref, wdn_ref, nfin_ref)

    @pl.when(step < prompt_steps)
    def _():
        sub = carry_ref.shape[0]

        @pl.when(step % steps_per_seq == 0)
        def _():
            carry_ref[...] = jnp.zeros_like(carry_ref)

        def shift_fn(z, c0):
            rows = z.shape[0]
            cols = slice(c0, c0 + z.shape[1])
            ext = jnp.concatenate([carry_ref[:, cols], z], axis=0)
            last = z[rows - sub:rows]
            carry_ref[:, cols] = last
            ctp_ref[:, cols] = pltpu.roll(last, CONV_W - 1, 0)[0:CONV_W - 1]
            return pltpu.roll(ext, 1, 0)[sub:], pltpu.roll(ext, 2, 0)[sub:]

        tiles = [slice(r0, r0 + FFN_SUBTILE) for r0 in range(0, xp_ref.shape[0], FFN_SUBTILE)]
        outs = _ffn_body([xp_ref[rows, :] for rows in tiles], *weights, shift_fn, final)
        for rows, out in zip(tiles, outs):
            xpo_ref[rows, :] = out

    @pl.when(step >= prompt_steps)
    def _():
        n_seq = st_ref.shape[0]
        m_rows = xs_ref.shape[0]
        t_new = m_rows // n_seq

        def shift_fn(z, c0):
            width = z.shape[1]
            cols = slice(c0, c0 + width)
            z3 = z.reshape(n_seq, t_new, width)
            st = st_ref[:, :, cols]
            p0 = jnp.broadcast_to(st[:, 0:1, :], z3.shape)
            p1 = jnp.broadcast_to(st[:, 1:2, :], z3.shape)
            t = lax.broadcasted_iota(jnp.int32, z3.shape, 1)
            r1 = pltpu.roll(z3, 1, 1)
            r2 = pltpu.roll(z3, 2, 1)
            cts_ref[:, :, cols] = r2[:, 0:CONV_W - 1, :]
            z1 = jnp.where(t == 0, p1, r1)
            z2 = jnp.where(t == 0, p0, jnp.where(t == 1, p1, r2))
            return z1.reshape(m_rows, width), z2.reshape(m_rows, width)

        (xso_ref[...],) = _ffn_body([xs_ref[...]], *weights, shift_fn, final)


def _drop_aliased(kernel_fn, n_in, n_aliased):
    def wrapped(*refs):
        return kernel_fn(*refs[:n_in], *refs[n_in + n_aliased:])
    return wrapped


def _layer_spec(layer, shape):
    nd = len(shape)
    return pl.BlockSpec((None,) + tuple(shape), lambda *_: (layer,) + (0,) * nd, pipeline_mode=pl.Buffered(1))


_SMEM_SPEC = pl.BlockSpec(memory_space=pltpu.SMEM)
_ANY_SPEC = pl.BlockSpec(memory_space=pl.ANY)


def _compiler_params(n_axes):
    return pltpu.CompilerParams(dimension_semantics=("arbitrary",) * n_axes,
                                vmem_limit_bytes=VMEM_LIMIT_BYTES)


def _mixer_weight_specs(layer):
    return [
        _layer_spec(layer, (1, D_MODEL)),
        _layer_spec(layer, (D_MODEL, ATTN_W)),
        _layer_spec(layer, (D_MODEL, REST_COLS)),
        _SMEM_SPEC,
        _layer_spec(layer, (1, GMLP_W)),
        _layer_spec(layer, (1, GMLP_W)),
        _layer_spec(layer, (GMLP_GROUPS, CHUNK, CHUNK)),
        _layer_spec(layer, (GMLP_GROUPS, CHUNK, 1)),
        _layer_spec(layer, (ATTN_W, D_MODEL)),
        _layer_spec(layer, (GMLP_W, D_MODEL)),
        _layer_spec(layer, (D_MODEL, D_MODEL)),
    ]


def _ffn_weight_specs(layer):
    return [
        _layer_spec(layer, (1, D_MODEL)),
        _layer_spec(layer, (D_MODEL, 2 * D_FF)),
        _layer_spec(layer, (CONV_W, 2 * D_FF)),
        _layer_spec(layer, (1, 2 * D_FF)),
        _layer_spec(layer, (D_FF, D_MODEL)),
        pl.BlockSpec((1, D_MODEL), lambda *_: (0, 0), pipeline_mode=pl.Buffered(1)),
    ]


def _call_layer(kernel_fn, grid, inputs, in_specs, out_specs, out_shapes, carried, scratch, name):
    n_in = len(inputs)
    first_carried = len(out_shapes) - len(carried)
    aliases = {n_in + n: first_carried + n for n in range(len(carried))}
    return pl.pallas_call(
        _drop_aliased(kernel_fn, n_in, len(carried)), grid=grid,
        in_specs=list(in_specs) + [_ANY_SPEC] * len(carried), out_specs=out_specs, out_shape=out_shapes,
        scratch_shapes=scratch, input_output_aliases=aliases,
        compiler_params=_compiler_params(len(grid)), name=name,
    )(*inputs, *carried)


def _mixer_prompt(layer, x, mixer_w, k_tails, v_tails):
    batch, seq, _ = x.shape
    tm = TM_PROMPT
    tile = pl.BlockSpec((None, tm, D_MODEL), lambda b, i: (b, i, 0))
    prev_block = pl.BlockSpec((None, BLOCK, D_MODEL), lambda b, i: (b, jnp.maximum(i * (tm // BLOCK) - 1, 0), 0))
    tail = pl.BlockSpec((None, None, WINDOW, KV_W), lambda b, i: (layer, b, 0, 0))
    tail_shape = jax.ShapeDtypeStruct(k_tails.shape, F32)
    return _call_layer(
        _mixer_prompt_kernel, (batch, seq // tm), (x, x) + mixer_w,
        [tile, prev_block] + _mixer_weight_specs(layer), [tile, tail, tail],
        [jax.ShapeDtypeStruct(x.shape, F32), tail_shape, tail_shape], [k_tails, v_tails], [], "mixer_prompt")


def _mixer_sample(layer, x, mixer_w, cache_k, cache_v, win_k, win_v, vn_all):
    n_rows = x.shape[0]
    _, n_seq_total, _, past = cache_k.shape
    t_new = n_rows // n_seq_total
    ns = SEQ_PER_STEP_MIXER
    m = ns * t_new
    tile = pl.BlockSpec((m, D_MODEL), lambda i: (i, 0))
    win = pl.BlockSpec((None, ns, KV_W, past), lambda i: (layer, i, 0, 0))
    vn_spec = pl.BlockSpec((None, m, GMLP_W), lambda i: (layer, i, 0))
    return _call_layer(
        _mixer_sample_kernel, (n_seq_total // ns,), (x,) + mixer_w + (cache_k, cache_v),
        [tile] + _mixer_weight_specs(layer) + [win, win], [tile, win, win, vn_spec],
        [jax.ShapeDtypeStruct(x.shape, F32), jax.ShapeDtypeStruct(win_k.shape, F32),
         jax.ShapeDtypeStruct(win_v.shape, F32), jax.ShapeDtypeStruct(vn_all.shape, F32)],
        [win_k, win_v, vn_all], [], "mixer_sample")


def _ffn(layer, xp, xs, ffn_w, state, conv_tails, new_state):
    batch, seq, _ = xp.shape
    n_rows_s = xs.shape[0]
    n_seq_total = state.shape[1]
    t_new = n_rows_s // n_seq_total
    tm = TM_PROMPT_FFN
    ns = SEQ_PER_STEP_FFN
    m = ns * t_new
    steps_per_seq = seq // tm
    prompt_steps = batch * steps_per_seq
    sample_steps = n_seq_total // ns

    def p_idx(s):
        return jnp.minimum(s, prompt_steps - 1)

    def s_idx(s):
        return jnp.maximum(s - prompt_steps, 0)

    tile_p = pl.BlockSpec((tm, D_MODEL), lambda s: (p_idx(s), 0))
    tile_s = pl.BlockSpec((m, D_MODEL), lambda s: (s_idx(s), 0))
    st = pl.BlockSpec((None, ns, CONV_W - 1, 2 * D_FF), lambda s: (layer, s_idx(s), 0, 0))
    tail_p = pl.BlockSpec((None, None, CONV_W - 1, 2 * D_FF), lambda s: (layer, p_idx(s) // steps_per_seq, 0, 0))
    xp2 = xp.reshape(batch * seq, D_MODEL)
    kernel_fn = functools.partial(_ffn_kernel, final=layer == DEPTH - 1, prompt_steps=prompt_steps,
                                  steps_per_seq=steps_per_seq)
    xpo, xso, conv_tails, new_state = _call_layer(
        kernel_fn, (prompt_steps + sample_steps,), (xp2, xs) + ffn_w + (state,),
        [tile_p, tile_s] + _ffn_weight_specs(layer) + [st], [tile_p, tile_s, tail_p, st],
        [jax.ShapeDtypeStruct(xp2.shape, F32), jax.ShapeDtypeStruct(xs.shape, F32),
         jax.ShapeDtypeStruct(conv_tails.shape, F32), jax.ShapeDtypeStruct(new_state.shape, F32)],
        [conv_tails, new_state], [pltpu.VMEM((F32_SUBLANES, 2 * D_FF), F32)], "ffn")
    return xpo.reshape(xp.shape), xso, conv_tails, new_state


def _heads_group_major(w, axis):
    shape = w.shape
    split = shape[:axis] + (N_KV_HEADS, GQA_GROUP, HEAD_DIM) + shape[axis + 1:]
    return jnp.swapaxes(w.reshape(split), axis, axis + 1).reshape(shape)


def kernel(x_prompt, x_sample, cache_win_k, cache_win_v, state_conv, norm_mix, w_in, sinks, gmlp_ln_g, gmlp_ln_b, gmlp_ws, gmlp_bs, w_branch_attn, w_branch_gmlp, w_out, norm_ffn, w_up, conv_w, conv_b, w_down, norm_final):
    batch = x_prompt.shape[0]
    dec_batch, dec_seq, _ = x_sample.shape
    w_in_b = w_in.astype(BF16)
    w_q_b = _heads_group_major(w_in_b[:, :, :ATTN_W], 2)
    w_pa_b = _heads_group_major(w_branch_attn, 1).astype(BF16)
    mixer_shared = (norm_mix.reshape(DEPTH, 1, D_MODEL), w_q_b, w_in_b)
    mixer_ln = (gmlp_ln_g.reshape(DEPTH, 1, GMLP_W), gmlp_ln_b.reshape(DEPTH, 1, GMLP_W))
    mixer_tail = (w_pa_b, w_branch_gmlp.astype(BF16), w_out.astype(BF16))
    gmlp_w = (gmlp_ws, gmlp_bs.reshape(DEPTH, GMLP_GROUPS, CHUNK, 1))
    ffn_w = (norm_ffn.reshape(DEPTH, 1, D_MODEL), w_up.astype(BF16), conv_w, conv_b.reshape(DEPTH, 1, 2 * D_FF),
             w_down.astype(BF16), norm_final.reshape(1, D_MODEL))
    cache_k = jnp.transpose(cache_win_k, (0, 1, 3, 4, 2)).reshape(DEPTH, dec_batch, KV_W, WINDOW)
    cache_v = jnp.transpose(cache_win_v, (0, 1, 3, 4, 2)).reshape(DEPTH, dec_batch, KV_W, WINDOW)

    xp = x_prompt
    xs = x_sample.reshape(dec_batch * dec_seq, D_MODEL)
    kp = jnp.zeros((DEPTH, batch, WINDOW, KV_W), F32)
    vp = jnp.zeros((DEPTH, batch, WINDOW, KV_W), F32)
    cp = jnp.zeros((DEPTH, batch, CONV_W - 1, 2 * D_FF), F32)
    gv = jnp.zeros((DEPTH, dec_batch * dec_seq, GMLP_W), F32)
    ks = jnp.zeros(cache_k.shape, F32)
    vs = jnp.zeros(cache_v.shape, F32)
    cs = jnp.zeros(state_conv.shape, F32)
    for l in range(DEPTH):
        sink_l = sinks[l]
        xp, kp, vp = _mixer_prompt(l, xp, mixer_shared + (sink_l,) + mixer_ln + gmlp_w + mixer_tail, kp, vp)
        xs, ks, vs, gv = _mixer_sample(l, xs, mixer_shared + (sink_l,) + mixer_ln + gmlp_w + mixer_tail,
                                       cache_k, cache_v, ks, vs, gv)
        xp, xs, cp, cs = _ffn(l, xp, xs, ffn_w, state_conv, cp, cs)

    kv_prompt = (DEPTH, batch, WINDOW, N_KV_HEADS, HEAD_DIM)
    kv_sample_t = (DEPTH, dec_batch, N_KV_HEADS, HEAD_DIM, WINDOW)
    return (xp, xs.reshape(x_sample.shape),
            kp.reshape(kv_prompt), vp.reshape(kv_prompt), cp,
            jnp.transpose(ks.reshape(kv_sample_t), (0, 1, 4, 2, 3)),
            jnp.transpose(vs.reshape(kv_sample_t), (0, 1, 4, 2, 3)), cs,
            gv.reshape(DEPTH, dec_batch, dec_seq, GMLP_W))
```

```python
import functools

import jax
import jax.numpy as jnp
import numpy as np
from jax import lax
from jax.experimental import pallas as pl
from jax.experimental.pallas import tpu as pltpu

D_MODEL = 1024
DEPTH = 4
HEAD_DIM = 64
N_HEADS = 16
N_KV_HEADS = 4
GQA_GROUP = N_HEADS // N_KV_HEADS
ATTN_W = N_HEADS * HEAD_DIM
KV_W = N_KV_HEADS * HEAD_DIM
WINDOW = 128
BLOCK = 128
CHUNK = 128
GMLP_CH = 128
GMLP_GROUPS = 6
GMLP_W = GMLP_GROUPS * GMLP_CH
D_FF = 2816
CONV_W = 3
EPS = 1e-5
NEG = -1e30

K0, V0, GU0, GV0, GA0, GB0, REST_COLS = 1024, 1280, 1536, 2304, 3072, 4096, 5120

V7X_VMEM_BYTES = 64 * 1024 * 1024
VMEM_LIMIT_BYTES = V7X_VMEM_BYTES - 8 * 1024 * 1024
F32_SUBLANES = 8

TM_PROMPT = 512
TM_PROMPT_FFN = 512
FFN_SUBTILE = 256
SEQ_PER_STEP_MIXER = 16
SEQ_PER_STEP_FFN = 32
FFN_CHUNK = 128
FFN_DOWN_K = 256
FFN_LOOKAHEAD = 8
ATTN_LOOKAHEAD = 2
PROJ_TILE = 256

F32 = jnp.float32
BF16 = jnp.bfloat16


def _rmsnorm(x, g):
    return x * lax.rsqrt(jnp.mean(x * x, axis=-1, keepdims=True) + EPS) * g


def _layernorm(x, g, b):
    mu = jnp.mean(x, axis=-1, keepdims=True)
    xc = x - mu
    var = jnp.mean(xc * xc, axis=-1, keepdims=True)
    return xc * lax.rsqrt(var + EPS) * g + b


def _gelu(x):
    c = np.sqrt(2.0 / np.pi).astype(np.float32)
    return x * (0.5 * (1.0 + jnp.tanh(c * (x + 0.044715 * (x * x * x)))))


def _sigmoid(x):
    return 1.0 / (1.0 + jnp.exp(-x))


def _dot(a, b):
    return jnp.dot(a, b, preferred_element_type=F32)


def _dot_nt(a, b):
    return lax.dot_general(a, b, (((1,), (1,)), ((), ())), preferred_element_type=F32)


def _kv_lane_masks(rows):
    lane = lax.broadcasted_iota(jnp.int32, (rows, KV_W), 1)
    return [(lane >= h * HEAD_DIM) & (lane < (h + 1) * HEAD_DIM) for h in range(N_KV_HEADS)]


def _select_kv_lanes(masks, parts):
    out = parts[N_KV_HEADS - 1]
    for h in range(N_KV_HEADS - 2, -1, -1):
        out = jnp.where(masks[h], parts[h], out)
    return out


def _softmax_pv(s, sink, vv):
    m = jnp.maximum(jnp.max(s, axis=-1, keepdims=True), sink)
    p = jnp.exp(s - m)
    denom = jnp.sum(p, axis=-1, keepdims=True) + jnp.exp(sink - m)
    return _dot(p.astype(BF16), vv) / denom


def _gmlp_mix(vn, ws_ref, bs_ref, period):
    vnb = vn.astype(BF16)
    row = lax.broadcasted_iota(jnp.int32, (CHUNK, CHUNK), 0)
    col = lax.broadcasted_iota(jnp.int32, (CHUNK, CHUNK), 1)
    keep = (col <= row) & (col >= row - (row & (period - 1)))
    reps = CHUNK // period
    w, bias = [], []
    for g in range(GMLP_GROUPS):
        if reps == 1:
            wg, bg = ws_ref[g], bs_ref[g]
        else:
            top = ws_ref[g, 0:period, :]
            wg = jnp.concatenate([pltpu.roll(top, i * period, 1) if i else top for i in range(reps)], axis=0)
            bg = jnp.concatenate([bs_ref[g, 0:period, :]] * reps, axis=0)
        w.append(jnp.where(keep, wg, 0.0).astype(BF16))
        bias.append(bg)
    rows = []
    for c in range(vn.shape[0] // CHUNK):
        cols = []
        for g in range(GMLP_GROUPS):
            blk = vnb[c * CHUNK:(c + 1) * CHUNK, g * GMLP_CH:(g + 1) * GMLP_CH]
            cols.append(_dot(w[g], blk) + bias[g])
        rows.append(jnp.concatenate(cols, axis=1))
    return rows[0] if len(rows) == 1 else jnp.concatenate(rows, axis=0)


class _GateProjection:
    def __init__(self, hb, wr_ref):
        self.hb, self.wr_ref, self.tiles = hb, wr_ref, []

    def issue(self, n_tiles):
        for _ in range(n_tiles):
            c0 = GU0 + len(self.tiles) * PROJ_TILE
            if c0 < REST_COLS:
                self.tiles.append(_dot(self.hb, self.wr_ref[:, c0:c0 + PROJ_TILE]))

    def columns(self, c0, c1):
        self.issue((REST_COLS - GU0) // PROJ_TILE)
        return jnp.concatenate(self.tiles[(c0 - GU0) // PROJ_TILE:(c1 - GU0) // PROJ_TILE], axis=1)


def _gate_merge_out(x, o, proj, lng_ref, lnb_ref, ws_ref, bs_ref, wpa_ref, wpb_ref, wout_ref, period):
    u = _gelu(proj.columns(GU0, GV0))
    vn = _layernorm(_gelu(proj.columns(GV0, GA0)), lng_ref[...], lnb_ref[...])
    ga = proj.columns(GA0, GB0)
    gb = proj.columns(GB0, REST_COLS)
    oa = _dot(o.astype(BF16), wpa_ref[...])
    s_gate = u * _gmlp_mix(vn, ws_ref, bs_ref, period)
    merged = _sigmoid(ga) * oa + _sigmoid(gb) * _dot(s_gate.astype(BF16), wpb_ref[...])
    return x + _dot(merged.astype(BF16), wout_ref[...]), vn


def _mixer_prompt_kernel(x_ref, xprev_ref, nm_ref, wq_ref, wr_ref, sink_ref, lng_ref, lnb_ref, ws_ref, bs_ref,
                         wpa_ref, wpb_ref, wout_ref,
                         xo_ref, kt_ref, vt_ref):
    i = pl.program_id(1)
    tm = x_ref.shape[0]
    x = x_ref[...]
    hb = _rmsnorm(x, nm_ref[...]).astype(BF16)

    hb_prev = _rmsnorm(xprev_ref[...], nm_ref[...]).astype(BF16)
    kv_prev = _dot(hb_prev, wr_ref[:, K0:GU0])
    k = _dot(hb, wr_ref[:, K0:V0])
    v = _dot(hb, wr_ref[:, V0:GU0])
    q = _dot(hb, wq_ref[...]) * (HEAD_DIM ** -0.5)
    proj = _GateProjection(hb, wr_ref)

    kk_all = jnp.concatenate([kv_prev[:, 0:KV_W], k], axis=0).astype(BF16)
    vv_all = jnp.concatenate([kv_prev[:, KV_W:2 * KV_W], v], axis=0).astype(BF16)
    kt_ref[...] = k[tm - WINDOW:tm]
    vt_ref[...] = v[tm - WINDOW:tm]

    rows = N_KV_HEADS * BLOCK
    t = lax.broadcasted_iota(jnp.int32, (rows, 2 * BLOCK), 0) & (BLOCK - 1)
    c = lax.broadcasted_iota(jnp.int32, (rows, 2 * BLOCK), 1)
    band = (c > t) & (c <= t + WINDOW)
    bias = jnp.where(band, 0.0, NEG).astype(F32)
    first_key = jnp.where(i > 0, 0, BLOCK)
    bias_first = jnp.where(band & (c >= first_key), 0.0, NEG).astype(F32)
    head_of_row = lax.broadcasted_iota(jnp.int32, (rows, 1), 0) >> (BLOCK.bit_length() - 1)
    masks = _kv_lane_masks(BLOCK)

    items = [(j, g) for j in range(tm // BLOCK) for g in range(GQA_GROUP)]

    def scores(item):
        j, g = item
        qg = q[j * BLOCK:(j + 1) * BLOCK, g * KV_W:(g + 1) * KV_W]
        lhs = jnp.concatenate([jnp.where(masks[h], qg, 0.0) for h in range(N_KV_HEADS)], axis=0).astype(BF16)
        return _dot_nt(lhs, kk_all[j * BLOCK:(j + 2) * BLOCK]) + (bias_first if j == 0 else bias)

    fill = -(-((REST_COLS - GU0) // PROJ_TILE) // len(items))
    o_groups = {}
    ahead = [scores(item) for item in items[:ATTN_LOOKAHEAD]]
    for n, (j, g) in enumerate(items):
        s = ahead.pop(0)
        if n + ATTN_LOOKAHEAD < len(items):
            ahead.append(scores(items[n + ATTN_LOOKAHEAD]))
        proj.issue(fill)
        sink = jnp.full((rows, 1), sink_ref[(N_KV_HEADS - 1) * GQA_GROUP + g], F32)
        for h in range(N_KV_HEADS - 2, -1, -1):
            sink = jnp.where(head_of_row == h, sink_ref[h * GQA_GROUP + g], sink)
        r = _softmax_pv(s, sink, vv_all[j * BLOCK:(j + 2) * BLOCK])
        o_groups[(j, g)] = _select_kv_lanes(masks, [r[h * BLOCK:(h + 1) * BLOCK] for h in range(N_KV_HEADS)])
    o_blocks = [jnp.concatenate([o_groups[(j, g)] for g in range(GQA_GROUP)], axis=1) for j in range(tm // BLOCK)]
    o = o_blocks[0] if len(o_blocks) == 1 else jnp.concatenate(o_blocks, axis=0)

    xo_ref[...], _ = _gate_merge_out(x, o, proj, lng_ref, lnb_ref, ws_ref, bs_ref, wpa_ref, wpb_ref, wout_ref, CHUNK)


def _mixer_sample_kernel(x_ref, nm_ref, wq_ref, wr_ref, sink_ref, lng_ref, lnb_ref, ws_ref, bs_ref,
                         wpa_ref, wpb_ref, wout_ref, ckt_ref, cvt_ref,
                         xo_ref, wkt_ref, wvt_ref, vn_ref):
    n_seq, _, past = ckt_ref.shape
    m_rows = x_ref.shape[0]
    t_new = m_rows // n_seq
    assert m_rows == past and past == BLOCK
    x = x_ref[...]
    hb = _rmsnorm(x, nm_ref[...]).astype(BF16)
    q = _dot(hb, wq_ref[...]) * (HEAD_DIM ** -0.5)
    k = _dot(hb, wr_ref[:, K0:V0])
    v = _dot(hb, wr_ref[:, V0:GU0])
    kt = jnp.transpose(k)
    vt = jnp.transpose(v)
    proj = _GateProjection(hb, wr_ref)

    n_keys = 2 * BLOCK
    rows = N_HEADS * t_new
    t = lax.broadcasted_iota(jnp.int32, (rows, n_keys), 0) & (t_new - 1)
    c = lax.broadcasted_iota(jnp.int32, (rows, n_keys), 1)
    diff = t + past - c
    bias = jnp.where((diff >= 0) & (diff < WINDOW) & (c < past + t_new), 0.0, NEG).astype(F32)
    head_slot = lax.broadcasted_iota(jnp.int32, (rows, 1), 0) >> (t_new.bit_length() - 1)
    sink = jnp.zeros((rows, 1), F32)
    for g in range(GQA_GROUP):
        for h in range(N_KV_HEADS):
            sink = jnp.where(head_slot == g * N_KV_HEADS + h, sink_ref[h * GQA_GROUP + g], sink)
    masks = _kv_lane_masks(t_new)
    zero_rows = jnp.zeros((BLOCK - t_new, KV_W), F32)
    keep_old = lax.broadcasted_iota(jnp.int32, (KV_W, past), 1) < past - t_new

    def new_window(old_t, new_t, b):
        shift = (past - t_new - b * t_new) % past
        placed = pltpu.roll(new_t, shift, 1) if shift else new_t
        return jnp.where(keep_old, pltpu.roll(old_t, past - t_new, 1), placed)

    s_list, v_list = [], []
    for b in range(n_seq):
        kct = ckt_ref[b]
        vct = cvt_ref[b]
        wkt_ref[b] = new_window(kct, kt, b)
        wvt_ref[b] = new_window(vct, vt, b)
        rows_b = slice(b * t_new, (b + 1) * t_new)
        k_pad = jnp.concatenate([k[rows_b], zero_rows], axis=0).astype(BF16)
        v_pad = jnp.concatenate([v[rows_b], zero_rows], axis=0).astype(BF16)
        qb = q[rows_b]
        lhs = jnp.concatenate(
            [jnp.where(masks[h], qb[:, g * KV_W:(g + 1) * KV_W], 0.0)
             for g in range(GQA_GROUP) for h in range(N_KV_HEADS)], axis=0).astype(BF16)
        s_list.append(jnp.concatenate([_dot(lhs, kct.astype(BF16)), _dot_nt(lhs, k_pad)], axis=1) + bias)
        v_list.append((vct.astype(BF16), v_pad))
    proj.issue((REST_COLS - GU0) // PROJ_TILE)
    o_rows = []
    for b in range(n_seq):
        s = s_list[b]
        vct_b, v_pad = v_list[b]
        m = jnp.maximum(jnp.max(s, axis=-1, keepdims=True), sink)
        p = jnp.exp(s - m)
        denom = jnp.sum(p, axis=-1, keepdims=True) + jnp.exp(sink - m)
        pb = p.astype(BF16)
        r = (_dot_nt(pb[:, 0:past], vct_b) + _dot(pb[:, past:n_keys], v_pad)) / denom
        o_groups = []
        for g in range(GQA_GROUP):
            base = g * N_KV_HEADS * t_new
            o_groups.append(_select_kv_lanes(
                masks, [r[base + h * t_new:base + (h + 1) * t_new] for h in range(N_KV_HEADS)]))
        o_rows.append(jnp.concatenate(o_groups, axis=1))
    o = jnp.concatenate(o_rows, axis=0)

    xo_ref[...], vn_ref[...] = _gate_merge_out(x, o, proj, lng_ref, lnb_ref, ws_ref, bs_ref,
                                               wpa_ref, wpb_ref, wout_ref, t_new)


def _conv_gate(za, zb, sa, sb, cw_ref, cb_ref, a0, b0, width):
    def conv(z, shifted, c0):
        z1, z2 = shifted
        cols = slice(c0, c0 + width)
        return cb_ref[:, cols] + ((z2 * cw_ref[0:1, cols] + z1 * cw_ref[1:2, cols]) + z * cw_ref[2:3, cols])

    a = conv(za, sa, a0)
    b = conv(zb, sb, b0)
    c0 = np.sqrt(2.0 / np.pi).astype(np.float32)
    c1 = np.float32(c0 * np.float32(0.044715))
    th = jnp.tanh(a * (c1 * (a * a) + c0))
    half_ab = (0.5 * a) * b
    return half_ab + half_ab * th


def _ffn_body(xs, nf_ref, wup_ref, cw_ref, cb_ref, wdn_ref, nfin_ref, shift_fn, final):
    hbs = [_rmsnorm(x, nf_ref[...]).astype(BF16) for x in xs]
    n_chunks = D_FF // FFN_CHUNK
    per_down = FFN_DOWN_K // FFN_CHUNK
    stages = [(n, c) for n in range(len(xs)) for c in range(n_chunks)]

    def up(stage):
        n, c = stage
        a0 = c * FFN_CHUNK
        w = jnp.concatenate([wup_ref[:, a0:a0 + FFN_CHUNK], wup_ref[:, D_FF + a0:D_FF + a0 + FFN_CHUNK]], axis=1)
        return _dot(hbs[n], w)

    accs = list(xs)
    ahead = [up(stage) for stage in stages[:FFN_LOOKAHEAD]]
    pending = []
    for i, (n, c) in enumerate(stages):
        z = ahead.pop(0)
        if i + FFN_LOOKAHEAD < len(stages):
            ahead.append(up(stages[i + FFN_LOOKAHEAD]))
        a0 = c * FFN_CHUNK
        b0 = D_FF + a0
        za, zb = z[:, 0:FFN_CHUNK], z[:, FFN_CHUNK:2 * FFN_CHUNK]
        pending.append(_conv_gate(za, zb, shift_fn(za, a0), shift_fn(zb, b0), cw_ref, cb_ref, a0, b0, FFN_CHUNK))
        if len(pending) == per_down:
            gated = pending[0] if per_down == 1 else jnp.concatenate(pending, axis=1)
            r0 = a0 + FFN_CHUNK - FFN_DOWN_K
            accs[n] = accs[n] + _dot(gated.astype(BF16), wdn_ref[r0:r0 + FFN_DOWN_K, :])
            pending = []
    if final:
        accs = [_rmsnorm(acc, nfin_ref[...]) for acc in accs]
    return accs


def _ffn_kernel(xp_ref, xs_ref, nf_ref, wup_ref, cw_ref, cb_ref, wdn_ref, nfin_ref, st_ref, *rest,
                final, prompt_steps, steps_per_seq):
    if final:
        xpo_ref, xso_ref, ctp_ref, cts_ref, carry_ref = rest
    else:
        wup_next_ref, wdn_next_ref, xpo_ref, xso_ref, wup_cast_ref, wdn_cast_ref, ctp_ref, cts_ref, carry_ref = rest
        wup_cast_ref[...] = wup_next_ref[...].astype(BF16)
        wdn_cast_ref[...] = wdn_next_ref[...].astype(BF16)
    step = pl.program_id(0)
    weights = (nf_ref, wup_ref, cw_ref, cb_ref, wdn_ref, nfin_ref)

    @pl.when(step < prompt_steps)
    def _():
        sub = carry_ref.shape[0]

        @pl.when(step % steps_per_seq == 0)
        def _():
            carry_ref[...] = jnp.zeros_like(carry_ref)

        def shift_fn(z, c0):
            rows = z.shape[0]
            cols = slice(c0, c0 + z.shape[1])
            ext = jnp.concatenate([carry_ref[:, cols], z], axis=0)
            last = z[rows - sub:rows]
            carry_ref[:, cols] = last
            ctp_ref[:, cols] = pltpu.roll(last, CONV_W - 1, 0)[0:CONV_W - 1]
            return pltpu.roll(ext, 1, 0)[sub:], pltpu.roll(ext, 2, 0)[sub:]

        tiles = [slice(r0, r0 + FFN_SUBTILE) for r0 in range(0, xp_ref.shape[0], FFN_SUBTILE)]
        outs = _ffn_body([xp_ref[rows, :] for rows in tiles], *weights, shift_fn, final)
        for rows, out in zip(tiles, outs):
            xpo_ref[rows, :] = out

    @pl.when(step >= prompt_steps)
    def _():
        n_seq = st_ref.shape[0]
        m_rows = xs_ref.shape[0]
        t_new = m_rows // n_seq

        def shift_fn(z, c0):
            width = z.shape[1]
            cols = slice(c0, c0 + width)
            z3 = z.reshape(n_seq, t_new, width)
            st = st_ref[:, :, cols]
            p0 = jnp.broadcast_to(st[:, 0:1, :], z3.shape)
            p1 = jnp.broadcast_to(st[:, 1:2, :], z3.shape)
            t = lax.broadcasted_iota(jnp.int32, z3.shape, 1)
            r1 = pltpu.roll(z3, 1, 1)
            r2 = pltpu.roll(z3, 2, 1)
            cts_ref[:, :, cols] = r2[:, 0:CONV_W - 1, :]
            z1 = jnp.where(t == 0, p1, r1)
            z2 = jnp.where(t == 0, p0, jnp.where(t == 1, p1, r2))
            return z1.reshape(m_rows, width), z2.reshape(m_rows, width)

        (xso_ref[...],) = _ffn_body([xs_ref[...]], *weights, shift_fn, final)


def _drop_aliased(kernel_fn, n_in, n_aliased):
    def wrapped(*refs):
        return kernel_fn(*refs[:n_in], *refs[n_in + n_aliased:])
    return wrapped


def _layer_spec(layer, shape):
    nd = len(shape)
    return pl.BlockSpec((None,) + tuple(shape), lambda *_: (layer,) + (0,) * nd, pipeline_mode=pl.Buffered(1))


def _resident_spec(shape):
    nd = len(shape)
    return pl.BlockSpec(tuple(shape), lambda *_: (0,) * nd, pipeline_mode=pl.Buffered(1))


_SMEM_SPEC = pl.BlockSpec(memory_space=pltpu.SMEM)
_ANY_SPEC = pl.BlockSpec(memory_space=pl.ANY)


def _compiler_params(n_axes):
    return pltpu.CompilerParams(dimension_semantics=("arbitrary",) * n_axes,
                                vmem_limit_bytes=VMEM_LIMIT_BYTES)


def _mixer_weight_specs(layer):
    return [
        _layer_spec(layer, (1, D_MODEL)),
        _layer_spec(layer, (D_MODEL, ATTN_W)),
        _layer_spec(layer, (D_MODEL, REST_COLS)),
        _SMEM_SPEC,
        _layer_spec(layer, (1, GMLP_W)),
        _layer_spec(layer, (1, GMLP_W)),
        _layer_spec(layer, (GMLP_GROUPS, CHUNK, CHUNK)),
        _layer_spec(layer, (GMLP_GROUPS, CHUNK, 1)),
        _layer_spec(layer, (ATTN_W, D_MODEL)),
        _layer_spec(layer, (GMLP_W, D_MODEL)),
        _layer_spec(layer, (D_MODEL, D_MODEL)),
    ]


def _ffn_weight_specs(layer):
    return [
        _layer_spec(layer, (1, D_MODEL)),
        _resident_spec((D_MODEL, 2 * D_FF)),
        _layer_spec(layer, (CONV_W, 2 * D_FF)),
        _layer_spec(layer, (1, 2 * D_FF)),
        _resident_spec((D_FF, D_MODEL)),
        _resident_spec((1, D_MODEL)),
    ]


def _call_layer(kernel_fn, grid, inputs, in_specs, out_specs, out_shapes, carried, scratch, name):
    n_in = len(inputs)
    first_carried = len(out_shapes) - len(carried)
    aliases = {n_in + n: first_carried + n for n in range(len(carried))}
    return pl.pallas_call(
        _drop_aliased(kernel_fn, n_in, len(carried)), grid=grid,
        in_specs=list(in_specs) + [_ANY_SPEC] * len(carried), out_specs=out_specs, out_shape=out_shapes,
        scratch_shapes=scratch, input_output_aliases=aliases,
        compiler_params=_compiler_params(len(grid)), name=name,
    )(*inputs, *carried)


def _mixer_prompt(layer, x, mixer_w, k_tails, v_tails):
    batch, seq, _ = x.shape
    tm = TM_PROMPT
    tile = pl.BlockSpec((None, tm, D_MODEL), lambda b, i: (b, i, 0))
    prev_block = pl.BlockSpec((None, BLOCK, D_MODEL), lambda b, i: (b, jnp.maximum(i * (tm // BLOCK) - 1, 0), 0))
    tail = pl.BlockSpec((None, None, WINDOW, KV_W), lambda b, i: (layer, b, 0, 0))
    tail_shape = jax.ShapeDtypeStruct(k_tails.shape, F32)
    return _call_layer(
        _mixer_prompt_kernel, (batch, seq // tm), (x, x) + mixer_w,
        [tile, prev_block] + _mixer_weight_specs(layer), [tile, tail, tail],
        [jax.ShapeDtypeStruct(x.shape, F32), tail_shape, tail_shape], [k_tails, v_tails], [], "mixer_prompt")


def _mixer_sample(layer, x, mixer_w, cache_k, cache_v, win_k, win_v, vn_all):
    n_rows = x.shape[0]
    _, n_seq_total, _, past = cache_k.shape
    t_new = n_rows // n_seq_total
    ns = SEQ_PER_STEP_MIXER
    m = ns * t_new
    tile = pl.BlockSpec((m, D_MODEL), lambda i: (i, 0))
    win = pl.BlockSpec((None, ns, KV_W, past), lambda i: (layer, i, 0, 0))
    vn_spec = pl.BlockSpec((None, m, GMLP_W), lambda i: (layer, i, 0))
    return _call_layer(
        _mixer_sample_kernel, (n_seq_total // ns,), (x,) + mixer_w + (cache_k, cache_v),
        [tile] + _mixer_weight_specs(layer) + [win, win], [tile, win, win, vn_spec],
        [jax.ShapeDtypeStruct(x.shape, F32), jax.ShapeDtypeStruct(win_k.shape, F32),
         jax.ShapeDtypeStruct(win_v.shape, F32), jax.ShapeDtypeStruct(vn_all.shape, F32)],
        [win_k, win_v, vn_all], [], "mixer_sample")


def _ffn(layer, xp, xs, ffn_w, w_up_f32, w_down_f32, state, conv_tails, new_state):
    batch, seq, _ = xp.shape
    n_rows_s = xs.shape[0]
    n_seq_total = state.shape[1]
    t_new = n_rows_s // n_seq_total
    tm = TM_PROMPT_FFN
    ns = SEQ_PER_STEP_FFN
    m = ns * t_new
    steps_per_seq = seq // tm
    prompt_steps = batch * steps_per_seq
    sample_steps = n_seq_total // ns

    def p_idx(s):
        return jnp.minimum(s, prompt_steps - 1)

    def s_idx(s):
        return jnp.maximum(s - prompt_steps, 0)

    tile_p = pl.BlockSpec((tm, D_MODEL), lambda s: (p_idx(s), 0))
    tile_s = pl.BlockSpec((m, D_MODEL), lambda s: (s_idx(s), 0))
    st = pl.BlockSpec((None, ns, CONV_W - 1, 2 * D_FF), lambda s: (layer, s_idx(s), 0, 0))
    tail_p = pl.BlockSpec((None, None, CONV_W - 1, 2 * D_FF), lambda s: (layer, p_idx(s) // steps_per_seq, 0, 0))
    xp2 = xp.reshape(batch * seq, D_MODEL)
    final = layer == DEPTH - 1
    kernel_fn = functools.partial(_ffn_kernel, final=final, prompt_steps=prompt_steps, steps_per_seq=steps_per_seq)
    inputs = (xp2, xs) + ffn_w + (state,)
    in_specs = [tile_p, tile_s] + _ffn_weight_specs(layer) + [st]
    out_specs = [tile_p, tile_s]
    out_shapes = [jax.ShapeDtypeStruct(xp2.shape, F32), jax.ShapeDtypeStruct(xs.shape, F32)]
    if not final:
        up_rows = D_MODEL // prompt_steps
        dn_rows = D_FF // (prompt_steps // 2)
        inputs += (w_up_f32, w_down_f32)
        in_specs += [pl.BlockSpec((None, up_rows, 2 * D_FF), lambda s: (layer + 1, p_idx(s), 0)),
                     pl.BlockSpec((None, dn_rows, D_MODEL), lambda s: (layer + 1, p_idx(s) // 2, 0))]
        out_specs += [pl.BlockSpec((up_rows, 2 * D_FF), lambda s: (p_idx(s), 0)),
                      pl.BlockSpec((dn_rows, D_MODEL), lambda s: (p_idx(s) // 2, 0))]
        out_shapes += [jax.ShapeDtypeStruct((D_MODEL, 2 * D_FF), BF16), jax.ShapeDtypeStruct((D_FF, D_MODEL), BF16)]
    outs = _call_layer(
        kernel_fn, (prompt_steps + sample_steps,), inputs, in_specs, out_specs + [tail_p, st],
        out_shapes + [jax.ShapeDtypeStruct(conv_tails.shape, F32), jax.ShapeDtypeStruct(new_state.shape, F32)],
        [conv_tails, new_state], [pltpu.VMEM((F32_SUBLANES, 2 * D_FF), F32)], "ffn")
    next_w = tuple(outs[2:-2])
    return outs[0].reshape(xp.shape), outs[1], outs[-2], outs[-1], next_w


def _heads_group_major(w, axis):
    shape = w.shape
    split = shape[:axis] + (N_KV_HEADS, GQA_GROUP, HEAD_DIM) + shape[axis + 1:]
    return jnp.swapaxes(w.reshape(split), axis, axis + 1).reshape(shape)


def kernel(x_prompt, x_sample, cache_win_k, cache_win_v, state_conv, norm_mix, w_in, sinks, gmlp_ln_g, gmlp_ln_b, gmlp_ws, gmlp_bs, w_branch_attn, w_branch_gmlp, w_out, norm_ffn, w_up, conv_w, conv_b, w_down, norm_final):
    batch = x_prompt.shape[0]
    dec_batch, dec_seq, _ = x_sample.shape
    w_in_b = w_in.astype(BF16)
    w_q_b = _heads_group_major(w_in_b[:, :, :ATTN_W], 2)
    w_pa_b = _heads_group_major(w_branch_attn, 1).astype(BF16)
    mixer_shared = (norm_mix.reshape(DEPTH, 1, D_MODEL), w_q_b, w_in_b)
    mixer_ln = (gmlp_ln_g.reshape(DEPTH, 1, GMLP_W), gmlp_ln_b.reshape(DEPTH, 1, GMLP_W))
    mixer_tail = (w_pa_b, w_branch_gmlp.astype(BF16), w_out.astype(BF16))
    gmlp_w = (gmlp_ws, gmlp_bs.reshape(DEPTH, GMLP_GROUPS, CHUNK, 1))
    ffn_up_down = (w_up[0].astype(BF16), w_down[0].astype(BF16))
    cache_k = jnp.transpose(cache_win_k, (0, 1, 3, 4, 2)).reshape(DEPTH, dec_batch, KV_W, WINDOW)
    cache_v = jnp.transpose(cache_win_v, (0, 1, 3, 4, 2)).reshape(DEPTH, dec_batch, KV_W, WINDOW)

    xp = x_prompt
    xs = x_sample.reshape(dec_batch * dec_seq, D_MODEL)
    kp = jnp.zeros((DEPTH, batch, WINDOW, KV_W), F32)
    vp = jnp.zeros((DEPTH, batch, WINDOW, KV_W), F32)
    cp = jnp.zeros((DEPTH, batch, CONV_W - 1, 2 * D_FF), F32)
    gv = jnp.zeros((DEPTH, dec_batch * dec_seq, GMLP_W), F32)
    ks = jnp.zeros(cache_k.shape, F32)
    vs = jnp.zeros(cache_v.shape, F32)
    cs = jnp.zeros(state_conv.shape, F32)
    for l in range(DEPTH):
        sink_l = sinks[l]
        xp, kp, vp = _mixer_prompt(l, xp, mixer_shared + (sink_l,) + mixer_ln + gmlp_w + mixer_tail, kp, vp)
        xs, ks, vs, gv = _mixer_sample(l, xs, mixer_shared + (sink_l,) + mixer_ln + gmlp_w + mixer_tail,
                                       cache_k, cache_v, ks, vs, gv)
        ffn_w = (norm_ffn.reshape(DEPTH, 1, D_MODEL), ffn_up_down[0], conv_w, conv_b.reshape(DEPTH, 1, 2 * D_FF),
                 ffn_up_down[1], norm_final.reshape(1, D_MODEL))
        xp, xs, cp, cs, ffn_up_down = _ffn(l, xp, xs, ffn_w, w_up, w_down, state_conv, cp, cs)

    kv_prompt = (DEPTH, batch, WINDOW, N_KV_HEADS, HEAD_DIM)
    kv_sample_t = (DEPTH, dec_batch, N_KV_HEADS, HEAD_DIM, WINDOW)
    return (xp, xs.reshape(x_sample.shape),
            kp.reshape(kv_prompt), vp.reshape(kv_prompt), cp,
            jnp.transpose(ks.reshape(kv_sample_t), (0, 1, 4, 2, 3)),
            jnp.transpose(vs.reshape(kv_sample_t), (0, 1, 4, 2, 3)), cs,
            gv.reshape(DEPTH, dec_batch, dec_seq, GMLP_W))
```

```python
import functools

import jax
import jax.numpy as jnp
import numpy as np
from jax import lax
from jax.experimental import pallas as pl
from jax.experimental.pallas import tpu as pltpu

D_MODEL = 1024
DEPTH = 4
HEAD_DIM = 64
N_HEADS = 16
N_KV_HEADS = 4
GQA_GROUP = N_HEADS // N_KV_HEADS
ATTN_W = N_HEADS * HEAD_DIM
KV_W = N_KV_HEADS * HEAD_DIM
WINDOW = 128
BLOCK = 128
CHUNK = 128
GMLP_CH = 128
GMLP_GROUPS = 6
GMLP_W = GMLP_GROUPS * GMLP_CH
D_FF = 2816
CONV_W = 3
EPS = 1e-5
NEG = -1e30

K0, V0, GU0, GV0, GA0, GB0, REST_COLS = 1024, 1280, 1536, 2304, 3072, 4096, 5120

V7X_VMEM_BYTES = 64 * 1024 * 1024
VMEM_LIMIT_BYTES = V7X_VMEM_BYTES - 8 * 1024 * 1024
F32_SUBLANES = 8

TM_PROMPT = 512
TM_PROMPT_FFN = 512
FFN_SUBTILE = 256
SEQ_PER_STEP_MIXER = 16
SEQ_PER_STEP_FFN = 32
FFN_CHUNK = 128
FFN_DOWN_K = 256
FFN_LOOKAHEAD = 8
ATTN_LOOKAHEAD = 2
PROJ_TILE = 256

F32 = jnp.float32
BF16 = jnp.bfloat16


def _rmsnorm(x, g):
    return x * lax.rsqrt(jnp.mean(x * x, axis=-1, keepdims=True) + EPS) * g


def _layernorm(x, g, b):
    mu = jnp.mean(x, axis=-1, keepdims=True)
    xc = x - mu
    var = jnp.mean(xc * xc, axis=-1, keepdims=True)
    return xc * lax.rsqrt(var + EPS) * g + b


def _gelu(x):
    c = np.sqrt(2.0 / np.pi).astype(np.float32)
    return x * (0.5 * (1.0 + jnp.tanh(c * (x + 0.044715 * (x * x * x)))))


def _sigmoid(x):
    return 1.0 / (1.0 + jnp.exp(-x))


def _dot(a, b):
    return jnp.dot(a, b, preferred_element_type=F32)


def _dot_nt(a, b):
    return lax.dot_general(a, b, (((1,), (1,)), ((), ())), preferred_element_type=F32)


def _kv_lane_masks(rows):
    lane = lax.broadcasted_iota(jnp.int32, (rows, KV_W), 1)
    return [(lane >= h * HEAD_DIM) & (lane < (h + 1) * HEAD_DIM) for h in range(N_KV_HEADS)]


def _select_kv_lanes(masks, parts):
    out = parts[N_KV_HEADS - 1]
    for h in range(N_KV_HEADS - 2, -1, -1):
        out = jnp.where(masks[h], parts[h], out)
    return out


def _softmax_pv(s, sink, vv):
    m = jnp.maximum(jnp.max(s, axis=-1, keepdims=True), sink)
    p = jnp.exp(s - m)
    denom = jnp.sum(p, axis=-1, keepdims=True) + jnp.exp(sink - m)
    return _dot(p.astype(BF16), vv) / denom


def _gmlp_mix(vn, ws_ref, bs_ref, period):
    vnb = vn.astype(BF16)
    row = lax.broadcasted_iota(jnp.int32, (CHUNK, CHUNK), 0)
    col = lax.broadcasted_iota(jnp.int32, (CHUNK, CHUNK), 1)
    keep = (col <= row) & (col >= row - (row & (period - 1)))
    reps = CHUNK // period
    w, bias = [], []
    for g in range(GMLP_GROUPS):
        if reps == 1:
            wg, bg = ws_ref[g], bs_ref[g]
        else:
            top = ws_ref[g, 0:period, :]
            wg = jnp.concatenate([pltpu.roll(top, i * period, 1) if i else top for i in range(reps)], axis=0)
            bg = jnp.concatenate([bs_ref[g, 0:period, :]] * reps, axis=0)
        w.append(jnp.where(keep, wg, 0.0).astype(BF16))
        bias.append(bg)
    rows = []
    for c in range(vn.shape[0] // CHUNK):
        cols = []
        for g in range(GMLP_GROUPS):
            blk = vnb[c * CHUNK:(c + 1) * CHUNK, g * GMLP_CH:(g + 1) * GMLP_CH]
            cols.append(_dot(w[g], blk) + bias[g])
        rows.append(jnp.concatenate(cols, axis=1))
    return rows[0] if len(rows) == 1 else jnp.concatenate(rows, axis=0)


class _GateProjection:
    def __init__(self, hb, wr_ref):
        self.hb, self.wr_ref, self.tiles = hb, wr_ref, []

    def issue(self, n_tiles):
        for _ in range(n_tiles):
            c0 = GU0 + len(self.tiles) * PROJ_TILE
            if c0 < REST_COLS:
                self.tiles.append(_dot(self.hb, self.wr_ref[:, c0:c0 + PROJ_TILE]))

    def columns(self, c0, c1):
        self.issue((REST_COLS - GU0) // PROJ_TILE)
        return jnp.concatenate(self.tiles[(c0 - GU0) // PROJ_TILE:(c1 - GU0) // PROJ_TILE], axis=1)


def _gate_merge_out(x, o, proj, lng_ref, lnb_ref, ws_ref, bs_ref, wpa_ref, wpb_ref, wout_ref, period):
    u = _gelu(proj.columns(GU0, GV0))
    vn = _layernorm(_gelu(proj.columns(GV0, GA0)), lng_ref[...], lnb_ref[...])
    ga = proj.columns(GA0, GB0)
    gb = proj.columns(GB0, REST_COLS)
    oa = _dot(o.astype(BF16), wpa_ref[...])
    s_gate = u * _gmlp_mix(vn, ws_ref, bs_ref, period)
    merged = _sigmoid(ga) * oa + _sigmoid(gb) * _dot(s_gate.astype(BF16), wpb_ref[...])
    return x + _dot(merged.astype(BF16), wout_ref[...]), vn


def _mixer_prompt_kernel(x_ref, xprev_ref, nm_ref, wq_ref, wr_ref, sink_ref, lng_ref, lnb_ref, ws_ref, bs_ref,
                         wpa_ref, wpb_ref, wout_ref, *rest, cast_next):
    if cast_next:
        win_next_ref, wpb_next_ref, wout_next_ref, xo_ref, win_cast_ref, wpb_cast_ref, wout_cast_ref, kt_ref, vt_ref = rest
        win_cast_ref[...] = win_next_ref[...].astype(BF16)
        wpb_cast_ref[...] = wpb_next_ref[...].astype(BF16)
        wout_cast_ref[...] = wout_next_ref[...].astype(BF16)
    else:
        xo_ref, kt_ref, vt_ref = rest
    i = pl.program_id(1)
    tm = x_ref.shape[0]
    x = x_ref[...]
    hb = _rmsnorm(x, nm_ref[...]).astype(BF16)

    hb_prev = _rmsnorm(xprev_ref[...], nm_ref[...]).astype(BF16)
    kv_prev = _dot(hb_prev, wr_ref[:, K0:GU0])
    k = _dot(hb, wr_ref[:, K0:V0])
    v = _dot(hb, wr_ref[:, V0:GU0])
    q = _dot(hb, wq_ref[...]) * (HEAD_DIM ** -0.5)
    proj = _GateProjection(hb, wr_ref)

    kk_all = jnp.concatenate([kv_prev[:, 0:KV_W], k], axis=0).astype(BF16)
    vv_all = jnp.concatenate([kv_prev[:, KV_W:2 * KV_W], v], axis=0).astype(BF16)
    kt_ref[...] = k[tm - WINDOW:tm]
    vt_ref[...] = v[tm - WINDOW:tm]

    rows = N_KV_HEADS * BLOCK
    t = lax.broadcasted_iota(jnp.int32, (rows, 2 * BLOCK), 0) & (BLOCK - 1)
    c = lax.broadcasted_iota(jnp.int32, (rows, 2 * BLOCK), 1)
    band = (c > t) & (c <= t + WINDOW)
    bias = jnp.where(band, 0.0, NEG).astype(F32)
    first_key = jnp.where(i > 0, 0, BLOCK)
    bias_first = jnp.where(band & (c >= first_key), 0.0, NEG).astype(F32)
    head_of_row = lax.broadcasted_iota(jnp.int32, (rows, 1), 0) >> (BLOCK.bit_length() - 1)
    masks = _kv_lane_masks(BLOCK)

    items = [(j, g) for j in range(tm // BLOCK) for g in range(GQA_GROUP)]

    def scores(item):
        j, g = item
        qg = q[j * BLOCK:(j + 1) * BLOCK, g * KV_W:(g + 1) * KV_W]
        lhs = jnp.concatenate([jnp.where(masks[h], qg, 0.0) for h in range(N_KV_HEADS)], axis=0).astype(BF16)
        return _dot_nt(lhs, kk_all[j * BLOCK:(j + 2) * BLOCK]) + (bias_first if j == 0 else bias)

    fill = -(-((REST_COLS - GU0) // PROJ_TILE) // len(items))
    o_groups = {}
    ahead = [scores(item) for item in items[:ATTN_LOOKAHEAD]]
    for n, (j, g) in enumerate(items):
        s = ahead.pop(0)
        if n + ATTN_LOOKAHEAD < len(items):
            ahead.append(scores(items[n + ATTN_LOOKAHEAD]))
        proj.issue(fill)
        sink = jnp.full((rows, 1), sink_ref[(N_KV_HEADS - 1) * GQA_GROUP + g], F32)
        for h in range(N_KV_HEADS - 2, -1, -1):
            sink = jnp.where(head_of_row == h, sink_ref[h * GQA_GROUP + g], sink)
        r = _softmax_pv(s, sink, vv_all[j * BLOCK:(j + 2) * BLOCK])
        o_groups[(j, g)] = _select_kv_lanes(masks, [r[h * BLOCK:(h + 1) * BLOCK] for h in range(N_KV_HEADS)])
    o_blocks = [jnp.concatenate([o_groups[(j, g)] for g in range(GQA_GROUP)], axis=1) for j in range(tm // BLOCK)]
    o = o_blocks[0] if len(o_blocks) == 1 else jnp.concatenate(o_blocks, axis=0)

    xo_ref[...], _ = _gate_merge_out(x, o, proj, lng_ref, lnb_ref, ws_ref, bs_ref, wpa_ref, wpb_ref, wout_ref, CHUNK)


def _mixer_sample_kernel(x_ref, nm_ref, wq_ref, wr_ref, sink_ref, lng_ref, lnb_ref, ws_ref, bs_ref,
                         wpa_ref, wpb_ref, wout_ref, ckt_ref, cvt_ref,
                         xo_ref, wkt_ref, wvt_ref, vn_ref):
    n_seq, _, past = ckt_ref.shape
    m_rows = x_ref.shape[0]
    t_new = m_rows // n_seq
    assert m_rows == past and past == BLOCK
    x = x_ref[...]
    hb = _rmsnorm(x, nm_ref[...]).astype(BF16)
    q = _dot(hb, wq_ref[...]) * (HEAD_DIM ** -0.5)
    k = _dot(hb, wr_ref[:, K0:V0])
    v = _dot(hb, wr_ref[:, V0:GU0])
    kt = jnp.transpose(k)
    vt = jnp.transpose(v)
    proj = _GateProjection(hb, wr_ref)

    n_keys = 2 * BLOCK
    rows = N_HEADS * t_new
    t = lax.broadcasted_iota(jnp.int32, (rows, n_keys), 0) & (t_new - 1)
    c = lax.broadcasted_iota(jnp.int32, (rows, n_keys), 1)
    diff = t + past - c
    bias = jnp.where((diff >= 0) & (diff < WINDOW) & (c < past + t_new), 0.0, NEG).astype(F32)
    head_slot = lax.broadcasted_iota(jnp.int32, (rows, 1), 0) >> (t_new.bit_length() - 1)
    sink = jnp.zeros((rows, 1), F32)
    for g in range(GQA_GROUP):
        for h in range(N_KV_HEADS):
            sink = jnp.where(head_slot == g * N_KV_HEADS + h, sink_ref[h * GQA_GROUP + g], sink)
    masks = _kv_lane_masks(t_new)
    zero_rows = jnp.zeros((BLOCK - t_new, KV_W), F32)
    keep_old = lax.broadcasted_iota(jnp.int32, (KV_W, past), 1) < past - t_new

    def new_window(old_t, new_t, b):
        shift = (past - t_new - b * t_new) % past
        placed = pltpu.roll(new_t, shift, 1) if shift else new_t
        return jnp.where(keep_old, pltpu.roll(old_t, past - t_new, 1), placed)

    s_list, v_list = [], []
    for b in range(n_seq):
        kct = ckt_ref[b]
        vct = cvt_ref[b]
        wkt_ref[b] = new_window(kct, kt, b)
        wvt_ref[b] = new_window(vct, vt, b)
        rows_b = slice(b * t_new, (b + 1) * t_new)
        k_pad = jnp.concatenate([k[rows_b], zero_rows], axis=0).astype(BF16)
        v_pad = jnp.concatenate([v[rows_b], zero_rows], axis=0).astype(BF16)
        qb = q[rows_b]
        lhs = jnp.concatenate(
            [jnp.where(masks[h], qb[:, g * KV_W:(g + 1) * KV_W], 0.0)
             for g in range(GQA_GROUP) for h in range(N_KV_HEADS)], axis=0).astype(BF16)
        s_list.append(jnp.concatenate([_dot(lhs, kct.astype(BF16)), _dot_nt(lhs, k_pad)], axis=1) + bias)
        v_list.append((vct.astype(BF16), v_pad))
    proj.issue((REST_COLS - GU0) // PROJ_TILE)
    o_rows = []
    for b in range(n_seq):
        s = s_list[b]
        vct_b, v_pad = v_list[b]
        m = jnp.maximum(jnp.max(s, axis=-1, keepdims=True), sink)
        p = jnp.exp(s - m)
        denom = jnp.sum(p, axis=-1, keepdims=True) + jnp.exp(sink - m)
        pb = p.astype(BF16)
        r = (_dot_nt(pb[:, 0:past], vct_b) + _dot(pb[:, past:n_keys], v_pad)) / denom
        o_groups = []
        for g in range(GQA_GROUP):
            base = g * N_KV_HEADS * t_new
            o_groups.append(_select_kv_lanes(
                masks, [r[base + h * t_new:base + (h + 1) * t_new] for h in range(N_KV_HEADS)]))
        o_rows.append(jnp.concatenate(o_groups, axis=1))
    o = jnp.concatenate(o_rows, axis=0)

    xo_ref[...], vn_ref[...] = _gate_merge_out(x, o, proj, lng_ref, lnb_ref, ws_ref, bs_ref,
                                               wpa_ref, wpb_ref, wout_ref, t_new)


def _conv_gate(za, zb, sa, sb, cw_ref, cb_ref, a0, b0, width):
    def conv(z, shifted, c0):
        z1, z2 = shifted
        cols = slice(c0, c0 + width)
        return cb_ref[:, cols] + ((z2 * cw_ref[0:1, cols] + z1 * cw_ref[1:2, cols]) + z * cw_ref[2:3, cols])

    a = conv(za, sa, a0)
    b = conv(zb, sb, b0)
    c0 = np.sqrt(2.0 / np.pi).astype(np.float32)
    c1 = np.float32(c0 * np.float32(0.044715))
    th = jnp.tanh(a * (c1 * (a * a) + c0))
    half_ab = (0.5 * a) * b
    return half_ab + half_ab * th


def _ffn_body(xs, nf_ref, wup_ref, cw_ref, cb_ref, wdn_ref, nfin_ref, shift_fn, final):
    hbs = [_rmsnorm(x, nf_ref[...]).astype(BF16) for x in xs]
    n_chunks = D_FF // FFN_CHUNK
    per_down = FFN_DOWN_K // FFN_CHUNK
    stages = [(n, c) for n in range(len(xs)) for c in range(n_chunks)]

    def up(stage):
        n, c = stage
        a0 = c * FFN_CHUNK
        w = jnp.concatenate([wup_ref[:, a0:a0 + FFN_CHUNK], wup_ref[:, D_FF + a0:D_FF + a0 + FFN_CHUNK]], axis=1)
        return _dot(hbs[n], w)

    accs = list(xs)
    ahead = [up(stage) for stage in stages[:FFN_LOOKAHEAD]]
    pending = []
    for i, (n, c) in enumerate(stages):
        z = ahead.pop(0)
        if i + FFN_LOOKAHEAD < len(stages):
            ahead.append(up(stages[i + FFN_LOOKAHEAD]))
        a0 = c * FFN_CHUNK
        b0 = D_FF + a0
        za, zb = z[:, 0:FFN_CHUNK], z[:, FFN_CHUNK:2 * FFN_CHUNK]
        pending.append(_conv_gate(za, zb, shift_fn(za, a0), shift_fn(zb, b0), cw_ref, cb_ref, a0, b0, FFN_CHUNK))
        if len(pending) == per_down:
            gated = pending[0] if per_down == 1 else jnp.concatenate(pending, axis=1)
            r0 = a0 + FFN_CHUNK - FFN_DOWN_K
            accs[n] = accs[n] + _dot(gated.astype(BF16), wdn_ref[r0:r0 + FFN_DOWN_K, :])
            pending = []
    if final:
        accs = [_rmsnorm(acc, nfin_ref[...]) for acc in accs]
    return accs


def _ffn_kernel(xp_ref, xs_ref, nf_ref, wup_ref, cw_ref, cb_ref, wdn_ref, nfin_ref, st_ref, *rest,
                final, prompt_steps, steps_per_seq):
    if final:
        xpo_ref, xso_ref, ctp_ref, cts_ref, carry_ref = rest
    else:
        wup_next_ref, wdn_next_ref, xpo_ref, xso_ref, wup_cast_ref, wdn_cast_ref, ctp_ref, cts_ref, carry_ref = rest
        wup_cast_ref[...] = wup_next_ref[...].astype(BF16)
        wdn_cast_ref[...] = wdn_next_ref[...].astype(BF16)
    step = pl.program_id(0)
    weights = (nf_ref, wup_ref, cw_ref, cb_ref, wdn_ref, nfin_ref)

    @pl.when(step < prompt_steps)
    def _():
        sub = carry_ref.shape[0]

        @pl.when(step % steps_per_seq == 0)
        def _():
            carry_ref[...] = jnp.zeros_like(carry_ref)

        def shift_fn(z, c0):
            rows = z.shape[0]
            cols = slice(c0, c0 + z.shape[1])
            ext = jnp.concatenate([carry_ref[:, cols], z], axis=0)
            last = z[rows - sub:rows]
            carry_ref[:, cols] = last
            ctp_ref[:, cols] = pltpu.roll(last, CONV_W - 1, 0)[0:CONV_W - 1]
            return pltpu.roll(ext, 1, 0)[sub:], pltpu.roll(ext, 2, 0)[sub:]

        tiles = [slice(r0, r0 + FFN_SUBTILE) for r0 in range(0, xp_ref.shape[0], FFN_SUBTILE)]
        outs = _ffn_body([xp_ref[rows, :] for rows in tiles], *weights, shift_fn, final)
        for rows, out in zip(tiles, outs):
            xpo_ref[rows, :] = out

    @pl.when(step >= prompt_steps)
    def _():
        n_seq = st_ref.shape[0]
        m_rows = xs_ref.shape[0]
        t_new = m_rows // n_seq

        def shift_fn(z, c0):
            width = z.shape[1]
            cols = slice(c0, c0 + width)
            z3 = z.reshape(n_seq, t_new, width)
            st = st_ref[:, :, cols]
            p0 = jnp.broadcast_to(st[:, 0:1, :], z3.shape)
            p1 = jnp.broadcast_to(st[:, 1:2, :], z3.shape)
            t = lax.broadcasted_iota(jnp.int32, z3.shape, 1)
            r1 = pltpu.roll(z3, 1, 1)
            r2 = pltpu.roll(z3, 2, 1)
            cts_ref[:, :, cols] = r2[:, 0:CONV_W - 1, :]
            z1 = jnp.where(t == 0, p1, r1)
            z2 = jnp.where(t == 0, p0, jnp.where(t == 1, p1, r2))
            return z1.reshape(m_rows, width), z2.reshape(m_rows, width)

        (xso_ref[...],) = _ffn_body([xs_ref[...]], *weights, shift_fn, final)


def _drop_aliased(kernel_fn, n_in, n_aliased):
    def wrapped(*refs):
        return kernel_fn(*refs[:n_in], *refs[n_in + n_aliased:])
    return wrapped


def _layer_spec(layer, shape):
    nd = len(shape)
    return pl.BlockSpec((None,) + tuple(shape), lambda *_: (layer,) + (0,) * nd, pipeline_mode=pl.Buffered(1))


def _resident_spec(shape):
    nd = len(shape)
    return pl.BlockSpec(tuple(shape), lambda *_: (0,) * nd, pipeline_mode=pl.Buffered(1))


_SMEM_SPEC = pl.BlockSpec(memory_space=pltpu.SMEM)
_ANY_SPEC = pl.BlockSpec(memory_space=pl.ANY)


def _compiler_params(n_axes):
    return pltpu.CompilerParams(dimension_semantics=("arbitrary",) * n_axes,
                                vmem_limit_bytes=VMEM_LIMIT_BYTES)


def _mixer_weight_specs(layer):
    return [
        _layer_spec(layer, (1, D_MODEL)),
        _layer_spec(layer, (D_MODEL, ATTN_W)),
        _resident_spec((D_MODEL, REST_COLS)),
        _SMEM_SPEC,
        _layer_spec(layer, (1, GMLP_W)),
        _layer_spec(layer, (1, GMLP_W)),
        _layer_spec(layer, (GMLP_GROUPS, CHUNK, CHUNK)),
        _layer_spec(layer, (GMLP_GROUPS, CHUNK, 1)),
        _layer_spec(layer, (ATTN_W, D_MODEL)),
        _resident_spec((GMLP_W, D_MODEL)),
        _resident_spec((D_MODEL, D_MODEL)),
    ]


def _ffn_weight_specs(layer):
    return [
        _layer_spec(layer, (1, D_MODEL)),
        _resident_spec((D_MODEL, 2 * D_FF)),
        _layer_spec(layer, (CONV_W, 2 * D_FF)),
        _layer_spec(layer, (1, 2 * D_FF)),
        _resident_spec((D_FF, D_MODEL)),
        _resident_spec((1, D_MODEL)),
    ]


def _call_layer(kernel_fn, grid, inputs, in_specs, out_specs, out_shapes, carried, scratch, name):
    n_in = len(inputs)
    first_carried = len(out_shapes) - len(carried)
    aliases = {n_in + n: first_carried + n for n in range(len(carried))}
    return pl.pallas_call(
        _drop_aliased(kernel_fn, n_in, len(carried)), grid=grid,
        in_specs=list(in_specs) + [_ANY_SPEC] * len(carried), out_specs=out_specs, out_shape=out_shapes,
        scratch_shapes=scratch, input_output_aliases=aliases,
        compiler_params=_compiler_params(len(grid)), name=name,
    )(*inputs, *carried)


def _mixer_prompt(layer, x, mixer_w, next_f32, k_tails, v_tails):
    batch, seq, _ = x.shape
    tm = TM_PROMPT
    tile = pl.BlockSpec((None, tm, D_MODEL), lambda b, i: (b, i, 0))
    prev_block = pl.BlockSpec((None, BLOCK, D_MODEL), lambda b, i: (b, jnp.maximum(i * (tm // BLOCK) - 1, 0), 0))
    tail = pl.BlockSpec((None, None, WINDOW, KV_W), lambda b, i: (layer, b, 0, 0))
    tail_shape = jax.ShapeDtypeStruct(k_tails.shape, F32)
    cast_next = layer < DEPTH - 1
    inputs = (x, x) + mixer_w
    in_specs = [tile, prev_block] + _mixer_weight_specs(layer)
    out_specs = [tile]
    out_shapes = [jax.ShapeDtypeStruct(x.shape, F32)]
    if cast_next:
        steps_per_seq = seq // tm
        n_steps = batch * steps_per_seq
        rows, pb_rows = D_MODEL // n_steps, GMLP_W // (n_steps // 2)

        def flat(b, i):
            return b * steps_per_seq + i

        inputs += next_f32
        in_specs += [pl.BlockSpec((None, rows, REST_COLS), lambda b, i: (layer + 1, flat(b, i), 0)),
                     pl.BlockSpec((None, pb_rows, D_MODEL), lambda b, i: (layer + 1, flat(b, i) // 2, 0)),
                     pl.BlockSpec((None, rows, D_MODEL), lambda b, i: (layer + 1, flat(b, i), 0))]
        out_specs += [pl.BlockSpec((rows, REST_COLS), lambda b, i: (flat(b, i), 0)),
                      pl.BlockSpec((pb_rows, D_MODEL), lambda b, i: (flat(b, i) // 2, 0)),
                      pl.BlockSpec((rows, D_MODEL), lambda b, i: (flat(b, i), 0))]
        out_shapes += [jax.ShapeDtypeStruct((D_MODEL, REST_COLS), BF16), jax.ShapeDtypeStruct((GMLP_W, D_MODEL), BF16),
                       jax.ShapeDtypeStruct((D_MODEL, D_MODEL), BF16)]
    outs = _call_layer(
        functools.partial(_mixer_prompt_kernel, cast_next=cast_next), (batch, seq // tm), inputs, in_specs,
        out_specs + [tail, tail], out_shapes + [tail_shape, tail_shape], [k_tails, v_tails], [], "mixer_prompt")
    return outs[0], outs[-2], outs[-1], tuple(outs[1:-2])


def _mixer_sample(layer, x, mixer_w, cache_k, cache_v, win_k, win_v, vn_all):
    n_rows = x.shape[0]
    _, n_seq_total, _, past = cache_k.shape
    t_new = n_rows // n_seq_total
    ns = SEQ_PER_STEP_MIXER
    m = ns * t_new
    tile = pl.BlockSpec((m, D_MODEL), lambda i: (i, 0))
    win = pl.BlockSpec((None, ns, KV_W, past), lambda i: (layer, i, 0, 0))
    vn_spec = pl.BlockSpec((None, m, GMLP_W), lambda i: (layer, i, 0))
    return _call_layer(
        _mixer_sample_kernel, (n_seq_total // ns,), (x,) + mixer_w + (cache_k, cache_v),
        [tile] + _mixer_weight_specs(layer) + [win, win], [tile, win, win, vn_spec],
        [jax.ShapeDtypeStruct(x.shape, F32), jax.ShapeDtypeStruct(win_k.shape, F32),
         jax.ShapeDtypeStruct(win_v.shape, F32), jax.ShapeDtypeStruct(vn_all.shape, F32)],
        [win_k, win_v, vn_all], [], "mixer_sample")


def _ffn(layer, xp, xs, ffn_w, w_up_f32, w_down_f32, state, conv_tails, new_state):
    batch, seq, _ = xp.shape
    n_rows_s = xs.shape[0]
    n_seq_total = state.shape[1]
    t_new = n_rows_s // n_seq_total
    tm = TM_PROMPT_FFN
    ns = SEQ_PER_STEP_FFN
    m = ns * t_new
    steps_per_seq = seq // tm
    prompt_steps = batch * steps_per_seq
    sample_steps = n_seq_total // ns

    def p_idx(s):
        return jnp.minimum(s, prompt_steps - 1)

    def s_idx(s):
        return jnp.maximum(s - prompt_steps, 0)

    tile_p = pl.BlockSpec((tm, D_MODEL), lambda s: (p_idx(s), 0))
    tile_s = pl.BlockSpec((m, D_MODEL), lambda s: (s_idx(s), 0))
    st = pl.BlockSpec((None, ns, CONV_W - 1, 2 * D_FF), lambda s: (layer, s_idx(s), 0, 0))
    tail_p = pl.BlockSpec((None, None, CONV_W - 1, 2 * D_FF), lambda s: (layer, p_idx(s) // steps_per_seq, 0, 0))
    xp2 = xp.reshape(batch * seq, D_MODEL)
    final = layer == DEPTH - 1
    kernel_fn = functools.partial(_ffn_kernel, final=final, prompt_steps=prompt_steps, steps_per_seq=steps_per_seq)
    inputs = (xp2, xs) + ffn_w + (state,)
    in_specs = [tile_p, tile_s] + _ffn_weight_specs(layer) + [st]
    out_specs = [tile_p, tile_s]
    out_shapes = [jax.ShapeDtypeStruct(xp2.shape, F32), jax.ShapeDtypeStruct(xs.shape, F32)]
    if not final:
        up_rows = D_MODEL // prompt_steps
        dn_rows = D_FF // (prompt_steps // 2)
        inputs += (w_up_f32, w_down_f32)
        in_specs += [pl.BlockSpec((None, up_rows, 2 * D_FF), lambda s: (layer + 1, p_idx(s), 0)),
                     pl.BlockSpec((None, dn_rows, D_MODEL), lambda s: (layer + 1, p_idx(s) // 2, 0))]
        out_specs += [pl.BlockSpec((up_rows, 2 * D_FF), lambda s: (p_idx(s), 0)),
                      pl.BlockSpec((dn_rows, D_MODEL), lambda s: (p_idx(s) // 2, 0))]
        out_shapes += [jax.ShapeDtypeStruct((D_MODEL, 2 * D_FF), BF16), jax.ShapeDtypeStruct((D_FF, D_MODEL), BF16)]
    outs = _call_layer(
        kernel_fn, (prompt_steps + sample_steps,), inputs, in_specs, out_specs + [tail_p, st],
        out_shapes + [jax.ShapeDtypeStruct(conv_tails.shape, F32), jax.ShapeDtypeStruct(new_state.shape, F32)],
        [conv_tails, new_state], [pltpu.VMEM((F32_SUBLANES, 2 * D_FF), F32)], "ffn")
    next_w = tuple(outs[2:-2])
    return outs[0].reshape(xp.shape), outs[1], outs[-2], outs[-1], next_w


def _heads_group_major(w, axis):
    shape = w.shape
    split = shape[:axis] + (N_KV_HEADS, GQA_GROUP, HEAD_DIM) + shape[axis + 1:]
    return jnp.swapaxes(w.reshape(split), axis, axis + 1).reshape(shape)


def kernel(x_prompt, x_sample, cache_win_k, cache_win_v, state_conv, norm_mix, w_in, sinks, gmlp_ln_g, gmlp_ln_b, gmlp_ws, gmlp_bs, w_branch_attn, w_branch_gmlp, w_out, norm_ffn, w_up, conv_w, conv_b, w_down, norm_final):
    batch = x_prompt.shape[0]
    dec_batch, dec_seq, _ = x_sample.shape
    w_q_b = _heads_group_major(w_in[:, :, :ATTN_W], 2).astype(BF16)
    w_pa_b = _heads_group_major(w_branch_attn, 1).astype(BF16)
    mixer_cast = (w_in[0].astype(BF16), w_branch_gmlp[0].astype(BF16), w_out[0].astype(BF16))
    norm_mix3 = norm_mix.reshape(DEPTH, 1, D_MODEL)
    mixer_ln = (gmlp_ln_g.reshape(DEPTH, 1, GMLP_W), gmlp_ln_b.reshape(DEPTH, 1, GMLP_W))
    gmlp_w = (gmlp_ws, gmlp_bs.reshape(DEPTH, GMLP_GROUPS, CHUNK, 1))
    ffn_up_down = (w_up[0].astype(BF16), w_down[0].astype(BF16))
    cache_k = jnp.transpose(cache_win_k, (0, 1, 3, 4, 2)).reshape(DEPTH, dec_batch, KV_W, WINDOW)
    cache_v = jnp.transpose(cache_win_v, (0, 1, 3, 4, 2)).reshape(DEPTH, dec_batch, KV_W, WINDOW)

    xp = x_prompt
    xs = x_sample.reshape(dec_batch * dec_seq, D_MODEL)
    kp = jnp.zeros((DEPTH, batch, WINDOW, KV_W), F32)
    vp = jnp.zeros((DEPTH, batch, WINDOW, KV_W), F32)
    cp = jnp.zeros((DEPTH, batch, CONV_W - 1, 2 * D_FF), F32)
    gv = jnp.zeros((DEPTH, dec_batch * dec_seq, GMLP_W), F32)
    ks = jnp.zeros(cache_k.shape, F32)
    vs = jnp.zeros(cache_v.shape, F32)
    cs = jnp.zeros(state_conv.shape, F32)
    for l in range(DEPTH):
        sink_l = sinks[l]
        mixer_w = ((norm_mix3, w_q_b, mixer_cast[0], sink_l) + mixer_ln + gmlp_w
                   + (w_pa_b, mixer_cast[1], mixer_cast[2]))
        xp, kp, vp, mixer_cast = _mixer_prompt(l, xp, mixer_w, (w_in, w_branch_gmlp, w_out), kp, vp)
        xs, ks, vs, gv = _mixer_sample(l, xs, mixer_w, cache_k, cache_v, ks, vs, gv)
        ffn_w = (norm_ffn.reshape(DEPTH, 1, D_MODEL), ffn_up_down[0], conv_w, conv_b.reshape(DEPTH, 1, 2 * D_FF),
                 ffn_up_down[1], norm_final.reshape(1, D_MODEL))
        xp, xs, cp, cs, ffn_up_down = _ffn(l, xp, xs, ffn_w, w_up, w_down, state_conv, cp, cs)

    kv_prompt = (DEPTH, batch, WINDOW, N_KV_HEADS, HEAD_DIM)
    kv_sample_t = (DEPTH, dec_batch, N_KV_HEADS, HEAD_DIM, WINDOW)
    return (xp, xs.reshape(x_sample.shape),
            kp.reshape(kv_prompt), vp.reshape(kv_prompt), cp,
            jnp.transpose(ks.reshape(kv_sample_t), (0, 1, 4, 2, 3)),
            jnp.transpose(vs.reshape(kv_sample_t), (0, 1, 4, 2, 3)), cs,
            gv.reshape(DEPTH, dec_batch, dec_seq, GMLP_W))
```

```python
import functools

import jax
import jax.numpy as jnp
import numpy as np
from jax import lax
from jax.experimental import pallas as pl
from jax.experimental.pallas import tpu as pltpu

D_MODEL = 1024
DEPTH = 4
HEAD_DIM = 64
N_HEADS = 16
N_KV_HEADS = 4
GQA_GROUP = N_HEADS // N_KV_HEADS
ATTN_W = N_HEADS * HEAD_DIM
KV_W = N_KV_HEADS * HEAD_DIM
WINDOW = 128
BLOCK = 128
CHUNK = 128
GMLP_CH = 128
GMLP_GROUPS = 6
GMLP_W = GMLP_GROUPS * GMLP_CH
D_FF = 2816
CONV_W = 3
EPS = 1e-5
NEG = -1e30

K0, V0, GU0, GV0, GA0, GB0, REST_COLS = 1024, 1280, 1536, 2304, 3072, 4096, 5120

V7X_VMEM_BYTES = 64 * 1024 * 1024
VMEM_LIMIT_BYTES = V7X_VMEM_BYTES - 8 * 1024 * 1024
F32_SUBLANES = 8

TM_PROMPT = 512
TM_PROMPT_FFN = 512
FFN_SUBTILE = 256
SEQ_PER_STEP_MIXER = 16
SEQ_PER_STEP_FFN = 32
FFN_CHUNK = 128
FFN_DOWN_K = 256
FFN_LOOKAHEAD = 8
ATTN_LOOKAHEAD = 2
PROJ_TILE = 256

F32 = jnp.float32
BF16 = jnp.bfloat16


def _rmsnorm(x, g):
    return x * lax.rsqrt(jnp.mean(x * x, axis=-1, keepdims=True) + EPS) * g


def _layernorm(x, g, b):
    mu = jnp.mean(x, axis=-1, keepdims=True)
    xc = x - mu
    var = jnp.mean(xc * xc, axis=-1, keepdims=True)
    return xc * lax.rsqrt(var + EPS) * g + b


def _gelu(x):
    c = np.sqrt(2.0 / np.pi).astype(np.float32)
    return x * (0.5 * (1.0 + jnp.tanh(c * (x + 0.044715 * (x * x * x)))))


def _sigmoid(x):
    return 1.0 / (1.0 + jnp.exp(-x))


def _dot(a, b):
    return jnp.dot(a, b, preferred_element_type=F32)


def _dot_nt(a, b):
    return lax.dot_general(a, b, (((1,), (1,)), ((), ())), preferred_element_type=F32)


def _kv_lane_masks(rows):
    lane = lax.broadcasted_iota(jnp.int32, (rows, KV_W), 1)
    return [(lane >= h * HEAD_DIM) & (lane < (h + 1) * HEAD_DIM) for h in range(N_KV_HEADS)]


def _select_kv_lanes(masks, parts):
    out = parts[N_KV_HEADS - 1]
    for h in range(N_KV_HEADS - 2, -1, -1):
        out = jnp.where(masks[h], parts[h], out)
    return out


def _softmax_pv(s, sink, vv):
    m = jnp.maximum(jnp.max(s, axis=-1, keepdims=True), sink)
    p = jnp.exp(s - m)
    denom = jnp.sum(p, axis=-1, keepdims=True) + jnp.exp(sink - m)
    return _dot(p.astype(BF16), vv) / denom


def _gmlp_mix(vn, ws_ref, bs_ref, period):
    vnb = vn.astype(BF16)
    row = lax.broadcasted_iota(jnp.int32, (CHUNK, CHUNK), 0)
    col = lax.broadcasted_iota(jnp.int32, (CHUNK, CHUNK), 1)
    keep = (col <= row) & (col >= row - (row & (period - 1)))
    reps = CHUNK // period
    w, bias = [], []
    for g in range(GMLP_GROUPS):
        if reps == 1:
            wg, bg = ws_ref[g], bs_ref[g]
        else:
            top = ws_ref[g, 0:period, :]
            wg = jnp.concatenate([pltpu.roll(top, i * period, 1) if i else top for i in range(reps)], axis=0)
            bg = jnp.concatenate([bs_ref[g, 0:period, :]] * reps, axis=0)
        w.append(jnp.where(keep, wg, 0.0).astype(BF16))
        bias.append(bg)
    rows = []
    for c in range(vn.shape[0] // CHUNK):
        cols = []
        for g in range(GMLP_GROUPS):
            blk = vnb[c * CHUNK:(c + 1) * CHUNK, g * GMLP_CH:(g + 1) * GMLP_CH]
            cols.append(_dot(w[g], blk) + bias[g])
        rows.append(jnp.concatenate(cols, axis=1))
    return rows[0] if len(rows) == 1 else jnp.concatenate(rows, axis=0)


class _GateProjection:
    def __init__(self, hb, wr_ref):
        self.hb, self.wr_ref, self.tiles = hb, wr_ref, []

    def issue(self, n_tiles):
        for _ in range(n_tiles):
            c0 = GU0 + len(self.tiles) * PROJ_TILE
            if c0 < REST_COLS:
                self.tiles.append(_dot(self.hb, self.wr_ref[:, c0:c0 + PROJ_TILE]))

    def columns(self, c0, c1):
        self.issue((REST_COLS - GU0) // PROJ_TILE)
        return jnp.concatenate(self.tiles[(c0 - GU0) // PROJ_TILE:(c1 - GU0) // PROJ_TILE], axis=1)


def _gate_merge_out(x, o, proj, lng_ref, lnb_ref, ws_ref, bs_ref, wpa_ref, wpb_ref, wout_ref, period):
    u = _gelu(proj.columns(GU0, GV0))
    vn = _layernorm(_gelu(proj.columns(GV0, GA0)), lng_ref[...], lnb_ref[...])
    ga = proj.columns(GA0, GB0)
    gb = proj.columns(GB0, REST_COLS)
    oa = _dot(o.astype(BF16), wpa_ref[...])
    s_gate = u * _gmlp_mix(vn, ws_ref, bs_ref, period)
    merged = _sigmoid(ga) * oa + _sigmoid(gb) * _dot(s_gate.astype(BF16), wpb_ref[...])
    return x + _dot(merged.astype(BF16), wout_ref[...]), vn


def _mixer_prompt_kernel(x_ref, xprev_ref, nm_ref, wq_ref, wr_ref, sink_ref, lng_ref, lnb_ref, ws_ref, bs_ref,
                         wpa_ref, wpb_ref, wout_ref, *rest, cast_next):
    if cast_next:
        win_next_ref, wpb_next_ref, wout_next_ref, xo_ref, win_cast_ref, wpb_cast_ref, wout_cast_ref, kt_ref, vt_ref = rest
        win_cast_ref[...] = win_next_ref[...].astype(BF16)
        wpb_cast_ref[...] = wpb_next_ref[...].astype(BF16)
        wout_cast_ref[...] = wout_next_ref[...].astype(BF16)
    else:
        xo_ref, kt_ref, vt_ref = rest
    i = pl.program_id(1)
    tm = x_ref.shape[0]
    x = x_ref[...]
    hb = _rmsnorm(x, nm_ref[...]).astype(BF16)

    hb_prev = _rmsnorm(xprev_ref[...], nm_ref[...]).astype(BF16)
    kv_prev = _dot(hb_prev, wr_ref[:, K0:GU0])
    k = _dot(hb, wr_ref[:, K0:V0])
    v = _dot(hb, wr_ref[:, V0:GU0])
    q = _dot(hb, wq_ref[...]) * (HEAD_DIM ** -0.5)
    proj = _GateProjection(hb, wr_ref)

    kk_all = jnp.concatenate([kv_prev[:, 0:KV_W], k], axis=0).astype(BF16)
    vv_all = jnp.concatenate([kv_prev[:, KV_W:2 * KV_W], v], axis=0).astype(BF16)
    kt_ref[...] = k[tm - WINDOW:tm]
    vt_ref[...] = v[tm - WINDOW:tm]

    rows = N_KV_HEADS * BLOCK
    t = lax.broadcasted_iota(jnp.int32, (rows, 2 * BLOCK), 0) & (BLOCK - 1)
    c = lax.broadcasted_iota(jnp.int32, (rows, 2 * BLOCK), 1)
    band = (c > t) & (c <= t + WINDOW)
    bias = jnp.where(band, 0.0, NEG).astype(F32)
    first_key = jnp.where(i > 0, 0, BLOCK)
    bias_first = jnp.where(band & (c >= first_key), 0.0, NEG).astype(F32)
    head_of_row = lax.broadcasted_iota(jnp.int32, (rows, 1), 0) >> (BLOCK.bit_length() - 1)
    masks = _kv_lane_masks(BLOCK)

    items = [(j, g) for j in range(tm // BLOCK) for g in range(GQA_GROUP)]

    def scores(item):
        j, g = item
        qg = q[j * BLOCK:(j + 1) * BLOCK, g * KV_W:(g + 1) * KV_W]
        lhs = jnp.concatenate([jnp.where(masks[h], qg, 0.0) for h in range(N_KV_HEADS)], axis=0).astype(BF16)
        return _dot_nt(lhs, kk_all[j * BLOCK:(j + 2) * BLOCK]) + (bias_first if j == 0 else bias)

    fill = -(-((REST_COLS - GU0) // PROJ_TILE) // len(items))
    o_groups = {}
    ahead = [scores(item) for item in items[:ATTN_LOOKAHEAD]]
    for n, (j, g) in enumerate(items):
        s = ahead.pop(0)
        if n + ATTN_LOOKAHEAD < len(items):
            ahead.append(scores(items[n + ATTN_LOOKAHEAD]))
        proj.issue(fill)
        sink = jnp.full((rows, 1), sink_ref[(N_KV_HEADS - 1) * GQA_GROUP + g], F32)
        for h in range(N_KV_HEADS - 2, -1, -1):
            sink = jnp.where(head_of_row == h, sink_ref[h * GQA_GROUP + g], sink)
        r = _softmax_pv(s, sink, vv_all[j * BLOCK:(j + 2) * BLOCK])
        o_groups[(j, g)] = _select_kv_lanes(masks, [r[h * BLOCK:(h + 1) * BLOCK] for h in range(N_KV_HEADS)])
    o_blocks = [jnp.concatenate([o_groups[(j, g)] for g in range(GQA_GROUP)], axis=1) for j in range(tm // BLOCK)]
    o = o_blocks[0] if len(o_blocks) == 1 else jnp.concatenate(o_blocks, axis=0)

    xo_ref[...], _ = _gate_merge_out(x, o, proj, lng_ref, lnb_ref, ws_ref, bs_ref, wpa_ref, wpb_ref, wout_ref, CHUNK)


def _mixer_sample_kernel(x_ref, nm_ref, wq_ref, wr_ref, sink_ref, lng_ref, lnb_ref, ws_ref, bs_ref,
                         wpa_ref, wpb_ref, wout_ref, ckt_ref, cvt_ref,
                         xo_ref, wkt_ref, wvt_ref, vn_ref):
    n_seq, _, past = ckt_ref.shape
    m_rows = x_ref.shape[0]
    t_new = m_rows // n_seq
    assert m_rows == past and past == BLOCK
    x = x_ref[...]
    hb = _rmsnorm(x, nm_ref[...]).astype(BF16)
    q = _dot(hb, wq_ref[...]) * (HEAD_DIM ** -0.5)
    k = _dot(hb, wr_ref[:, K0:V0])
    v = _dot(hb, wr_ref[:, V0:GU0])
    kt = jnp.transpose(k)
    vt = jnp.transpose(v)
    proj = _GateProjection(hb, wr_ref)

    n_keys = 2 * BLOCK
    rows = N_HEADS * t_new
    t = lax.broadcasted_iota(jnp.int32, (rows, n_keys), 0) & (t_new - 1)
    c = lax.broadcasted_iota(jnp.int32, (rows, n_keys), 1)
    diff = t + past - c
    bias = jnp.where((diff >= 0) & (diff < WINDOW) & (c < past + t_new), 0.0, NEG).astype(F32)
    head_slot = lax.broadcasted_iota(jnp.int32, (rows, 1), 0) >> (t_new.bit_length() - 1)
    sink = jnp.zeros((rows, 1), F32)
    for g in range(GQA_GROUP):
        for h in range(N_KV_HEADS):
            sink = jnp.where(head_slot == g * N_KV_HEADS + h, sink_ref[h * GQA_GROUP + g], sink)
    masks = _kv_lane_masks(t_new)
    zero_rows = jnp.zeros((BLOCK - t_new, KV_W), F32)
    keep_old = lax.broadcasted_iota(jnp.int32, (KV_W, past), 1) < past - t_new

    def new_window(old_t, new_t, b):
        shift = (past - t_new - b * t_new) % past
        placed = pltpu.roll(new_t, shift, 1) if shift else new_t
        return jnp.where(keep_old, pltpu.roll(old_t, past - t_new, 1), placed)

    s_list, v_list = [], []
    for b in range(n_seq):
        kct = ckt_ref[b]
        vct = cvt_ref[b]
        wkt_ref[b] = new_window(kct, kt, b)
        wvt_ref[b] = new_window(vct, vt, b)
        rows_b = slice(b * t_new, (b + 1) * t_new)
        k_pad = jnp.concatenate([k[rows_b], zero_rows], axis=0).astype(BF16)
        v_pad = jnp.concatenate([v[rows_b], zero_rows], axis=0).astype(BF16)
        qb = q[rows_b]
        lhs = jnp.concatenate(
            [jnp.where(masks[h], qb[:, g * KV_W:(g + 1) * KV_W], 0.0)
             for g in range(GQA_GROUP) for h in range(N_KV_HEADS)], axis=0).astype(BF16)
        s_list.append(jnp.concatenate([_dot(lhs, kct.astype(BF16)), _dot_nt(lhs, k_pad)], axis=1) + bias)
        v_list.append((vct.astype(BF16), v_pad))
    proj.issue((REST_COLS - GU0) // PROJ_TILE)
    o_rows = []
    for b in range(n_seq):
        s = s_list[b]
        vct_b, v_pad = v_list[b]
        m = jnp.maximum(jnp.max(s, axis=-1, keepdims=True), sink)
        p = jnp.exp(s - m)
        denom = jnp.sum(p, axis=-1, keepdims=True) + jnp.exp(sink - m)
        pb = p.astype(BF16)
        r = (_dot_nt(pb[:, 0:past], vct_b) + _dot(pb[:, past:n_keys], v_pad)) / denom
        o_groups = []
        for g in range(GQA_GROUP):
            base = g * N_KV_HEADS * t_new
            o_groups.append(_select_kv_lanes(
                masks, [r[base + h * t_new:base + (h + 1) * t_new] for h in range(N_KV_HEADS)]))
        o_rows.append(jnp.concatenate(o_groups, axis=1))
    o = jnp.concatenate(o_rows, axis=0)

    xo_ref[...], vn_ref[...] = _gate_merge_out(x, o, proj, lng_ref, lnb_ref, ws_ref, bs_ref,
                                               wpa_ref, wpb_ref, wout_ref, t_new)


def _conv_gate(za, zb, sa, sb, cw_ref, cb_ref, a0, b0, width):
    def conv(z, shifted, c0):
        z1, z2 = shifted
        cols = slice(c0, c0 + width)
        return cb_ref[:, cols] + ((z2 * cw_ref[0:1, cols] + z1 * cw_ref[1:2, cols]) + z * cw_ref[2:3, cols])

    a = conv(za, sa, a0)
    b = conv(zb, sb, b0)
    c0 = np.sqrt(2.0 / np.pi).astype(np.float32)
    c1 = np.float32(c0 * np.float32(0.044715))
    th = jnp.tanh(a * (c1 * (a * a) + c0))
    half_ab = (0.5 * a) * b
    return half_ab + half_ab * th


def _ffn_body(xs, nf_ref, wup_ref, cw_ref, cb_ref, wdn_ref, nfin_ref, shift_fn, final):
    hbs = [_rmsnorm(x, nf_ref[...]).astype(BF16) for x in xs]
    n_chunks = D_FF // FFN_CHUNK
    per_down = FFN_DOWN_K // FFN_CHUNK
    stages = [(n, c) for n in range(len(xs)) for c in range(n_chunks)]

    def up(stage):
        n, c = stage
        a0 = c * FFN_CHUNK
        w = jnp.concatenate([wup_ref[:, a0:a0 + FFN_CHUNK], wup_ref[:, D_FF + a0:D_FF + a0 + FFN_CHUNK]], axis=1)
        return _dot(hbs[n], w)

    accs = list(xs)
    ahead = [up(stage) for stage in stages[:FFN_LOOKAHEAD]]
    pending = []
    for i, (n, c) in enumerate(stages):
        z = ahead.pop(0)
        if i + FFN_LOOKAHEAD < len(stages):
            ahead.append(up(stages[i + FFN_LOOKAHEAD]))
        a0 = c * FFN_CHUNK
        b0 = D_FF + a0
        za, zb = z[:, 0:FFN_CHUNK], z[:, FFN_CHUNK:2 * FFN_CHUNK]
        pending.append(_conv_gate(za, zb, shift_fn(za, a0), shift_fn(zb, b0), cw_ref, cb_ref, a0, b0, FFN_CHUNK))
        if len(pending) == per_down:
            gated = pending[0] if per_down == 1 else jnp.concatenate(pending, axis=1)
            r0 = a0 + FFN_CHUNK - FFN_DOWN_K
            accs[n] = accs[n] + _dot(gated.astype(BF16), wdn_ref[r0:r0 + FFN_DOWN_K, :])
            pending = []
    if final:
        accs = [_rmsnorm(acc, nfin_ref[...]) for acc in accs]
    return accs


def _ffn_kernel(xp_ref, xs_ref, nf_ref, wup_ref, cw_ref, cb_ref, wdn_ref, nfin_ref, st_ref, *rest,
                final, prompt_steps, steps_per_seq):
    if final:
        xpo_ref, xso_ref, ctp_ref, cts_ref, carry_ref = rest
    else:
        wup_next_ref, wdn_next_ref, xpo_ref, xso_ref, wup_cast_ref, wdn_cast_ref, ctp_ref, cts_ref, carry_ref = rest
        wup_cast_ref[...] = wup_next_ref[...].astype(BF16)
        wdn_cast_ref[...] = wdn_next_ref[...].astype(BF16)
    step = pl.program_id(0)
    weights = (nf_ref, wup_ref, cw_ref, cb_ref, wdn_ref, nfin_ref)

    @pl.when(step < prompt_steps)
    def _():
        sub = carry_ref.shape[0]

        @pl.when(step % steps_per_seq == 0)
        def _():
            carry_ref[...] = jnp.zeros_like(carry_ref)

        def shift_fn(z, c0):
            rows = z.shape[0]
            cols = slice(c0, c0 + z.shape[1])
            ext = jnp.concatenate([carry_ref[:, cols], z], axis=0)
            last = z[rows - sub:rows]
            carry_ref[:, cols] = last
            ctp_ref[:, cols] = pltpu.roll(last, CONV_W - 1, 0)[0:CONV_W - 1]
            return pltpu.roll(ext, 1, 0)[sub:], pltpu.roll(ext, 2, 0)[sub:]

        tiles = [slice(r0, r0 + FFN_SUBTILE) for r0 in range(0, xp_ref.shape[0], FFN_SUBTILE)]
        outs = _ffn_body([xp_ref[rows, :] for rows in tiles], *weights, shift_fn, final)
        for rows, out in zip(tiles, outs):
            xpo_ref[rows, :] = out

    @pl.when(step >= prompt_steps)
    def _():
        n_seq = st_ref.shape[0]
        m_rows = xs_ref.shape[0]
        t_new = m_rows // n_seq

        def shift_fn(z, c0):
            width = z.shape[1]
            cols = slice(c0, c0 + width)
            z3 = z.reshape(n_seq, t_new, width)
            st = st_ref[:, :, cols]
            p0 = jnp.broadcast_to(st[:, 0:1, :], z3.shape)
            p1 = jnp.broadcast_to(st[:, 1:2, :], z3.shape)
            t = lax.broadcasted_iota(jnp.int32, z3.shape, 1)
            r1 = pltpu.roll(z3, 1, 1)
            r2 = pltpu.roll(z3, 2, 1)
            cts_ref[:, :, cols] = r2[:, 0:CONV_W - 1, :]
            z1 = jnp.where(t == 0, p1, r1)
            z2 = jnp.where(t == 0, p0, jnp.where(t == 1, p1, r2))
            return z1.reshape(m_rows, width), z2.reshape(m_rows, width)

        (xso_ref[...],) = _ffn_body([xs_ref[...]], *weights, shift_fn, final)


def _drop_aliased(kernel_fn, n_in, n_aliased):
    def wrapped(*refs):
        return kernel_fn(*refs[:n_in], *refs[n_in + n_aliased:])
    return wrapped


def _layer_spec(layer, shape):
    nd = len(shape)
    return pl.BlockSpec((None,) + tuple(shape), lambda *_: (layer,) + (0,) * nd, pipeline_mode=pl.Buffered(1))


def _resident_spec(shape):
    nd = len(shape)
    return pl.BlockSpec(tuple(shape), lambda *_: (0,) * nd, pipeline_mode=pl.Buffered(1))


_SMEM_SPEC = pl.BlockSpec(memory_space=pltpu.SMEM)
_ANY_SPEC = pl.BlockSpec(memory_space=pl.ANY)


def _compiler_params(n_axes):
    return pltpu.CompilerParams(dimension_semantics=("arbitrary",) * n_axes,
                                vmem_limit_bytes=VMEM_LIMIT_BYTES)


def _mixer_weight_specs(layer):
    return [
        _layer_spec(layer, (1, D_MODEL)),
        _layer_spec(layer, (D_MODEL, ATTN_W)),
        _resident_spec((D_MODEL, REST_COLS)),
        _SMEM_SPEC,
        _layer_spec(layer, (1, GMLP_W)),
        _layer_spec(layer, (1, GMLP_W)),
        _layer_spec(layer, (GMLP_GROUPS, CHUNK, CHUNK)),
        _layer_spec(layer, (GMLP_GROUPS, CHUNK, 1)),
        _layer_spec(layer, (ATTN_W, D_MODEL)),
        _resident_spec((GMLP_W, D_MODEL)),
        _resident_spec((D_MODEL, D_MODEL)),
    ]


def _ffn_weight_specs(layer):
    return [
        _layer_spec(layer, (1, D_MODEL)),
        _resident_spec((D_MODEL, 2 * D_FF)),
        _layer_spec(layer, (CONV_W, 2 * D_FF)),
        _layer_spec(layer, (1, 2 * D_FF)),
        _resident_spec((D_FF, D_MODEL)),
        _resident_spec((1, D_MODEL)),
    ]


def _call_layer(kernel_fn, grid, inputs, in_specs, out_specs, out_shapes, carried, scratch, name):
    n_in = len(inputs)
    first_carried = len(out_shapes) - len(carried)
    aliases = {n_in + n: first_carried + n for n in range(len(carried))}
    return pl.pallas_call(
        _drop_aliased(kernel_fn, n_in, len(carried)), grid=grid,
        in_specs=list(in_specs) + [_ANY_SPEC] * len(carried), out_specs=out_specs, out_shape=out_shapes,
        scratch_shapes=scratch, input_output_aliases=aliases,
        compiler_params=_compiler_params(len(grid)), name=name,
    )(*inputs, *carried)


def _mixer_prompt(layer, x, mixer_w, next_f32, k_tails, v_tails):
    batch, seq, _ = x.shape
    tm = TM_PROMPT
    tile = pl.BlockSpec((None, tm, D_MODEL), lambda b, i: (b, i, 0))
    prev_block = pl.BlockSpec((None, BLOCK, D_MODEL), lambda b, i: (b, jnp.maximum(i * (tm // BLOCK) - 1, 0), 0))
    tail = pl.BlockSpec((None, None, WINDOW, KV_W), lambda b, i: (layer, b, 0, 0))
    tail_shape = jax.ShapeDtypeStruct(k_tails.shape, F32)
    cast_next = layer < DEPTH - 1
    inputs = (x, x) + mixer_w
    in_specs = [tile, prev_block] + _mixer_weight_specs(layer)
    out_specs = [tile]
    out_shapes = [jax.ShapeDtypeStruct(x.shape, F32)]
    if cast_next:
        steps_per_seq = seq // tm
        n_steps = batch * steps_per_seq
        rows, pb_rows = D_MODEL // n_steps, GMLP_W // (n_steps // 2)

        def flat(b, i):
            return b * steps_per_seq + i

        inputs += next_f32
        in_specs += [pl.BlockSpec((None, rows, REST_COLS), lambda b, i: (layer + 1, flat(b, i), 0)),
                     pl.BlockSpec((None, pb_rows, D_MODEL), lambda b, i: (layer + 1, flat(b, i) // 2, 0)),
                     pl.BlockSpec((None, rows, D_MODEL), lambda b, i: (layer + 1, flat(b, i), 0))]
        out_specs += [pl.BlockSpec((rows, REST_COLS), lambda b, i: (flat(b, i), 0)),
                      pl.BlockSpec((pb_rows, D_MODEL), lambda b, i: (flat(b, i) // 2, 0)),
                      pl.BlockSpec((rows, D_MODEL), lambda b, i: (flat(b, i), 0))]
        out_shapes += [jax.ShapeDtypeStruct((D_MODEL, REST_COLS), BF16), jax.ShapeDtypeStruct((GMLP_W, D_MODEL), BF16),
                       jax.ShapeDtypeStruct((D_MODEL, D_MODEL), BF16)]
    outs = _call_layer(
        functools.partial(_mixer_prompt_kernel, cast_next=cast_next), (batch, seq // tm), inputs, in_specs,
        out_specs + [tail, tail], out_shapes + [tail_shape, tail_shape], [k_tails, v_tails], [], "mixer_prompt")
    return outs[0], outs[-2], outs[-1], tuple(outs[1:-2])


def _mixer_sample(layer, x, mixer_w, cache_k, cache_v, win_k, win_v, vn_all):
    n_rows = x.shape[0]
    _, n_seq_total, _, past = cache_k.shape
    t_new = n_rows // n_seq_total
    ns = SEQ_PER_STEP_MIXER
    m = ns * t_new
    tile = pl.BlockSpec((m, D_MODEL), lambda i: (i, 0))
    win = pl.BlockSpec((None, ns, KV_W, past), lambda i: (layer, i, 0, 0))
    vn_spec = pl.BlockSpec((None, m, GMLP_W), lambda i: (layer, i, 0))
    return _call_layer(
        _mixer_sample_kernel, (n_seq_total // ns,), (x,) + mixer_w + (cache_k, cache_v),
        [tile] + _mixer_weight_specs(layer) + [win, win], [tile, win, win, vn_spec],
        [jax.ShapeDtypeStruct(x.shape, F32), jax.ShapeDtypeStruct(win_k.shape, F32),
         jax.ShapeDtypeStruct(win_v.shape, F32), jax.ShapeDtypeStruct(vn_all.shape, F32)],
        [win_k, win_v, vn_all], [], "mixer_sample")


def _ffn(layer, xp, xs, ffn_w, w_up_f32, w_down_f32, state, conv_tails, new_state):
    batch, seq, _ = xp.shape
    n_rows_s = xs.shape[0]
    n_seq_total = state.shape[1]
    t_new = n_rows_s // n_seq_total
    tm = TM_PROMPT_FFN
    ns = SEQ_PER_STEP_FFN
    m = ns * t_new
    steps_per_seq = seq // tm
    prompt_steps = batch * steps_per_seq
    sample_steps = n_seq_total // ns

    def p_idx(s):
        return jnp.minimum(s, prompt_steps - 1)

    def s_idx(s):
        return jnp.maximum(s - prompt_steps, 0)

    tile_p = pl.BlockSpec((tm, D_MODEL), lambda s: (p_idx(s), 0))
    tile_s = pl.BlockSpec((m, D_MODEL), lambda s: (s_idx(s), 0))
    st = pl.BlockSpec((None, ns, CONV_W - 1, 2 * D_FF), lambda s: (layer, s_idx(s), 0, 0))
    tail_p = pl.BlockSpec((None, None, CONV_W - 1, 2 * D_FF), lambda s: (layer, p_idx(s) // steps_per_seq, 0, 0))
    xp2 = xp.reshape(batch * seq, D_MODEL)
    final = layer == DEPTH - 1
    kernel_fn = functools.partial(_ffn_kernel, final=final, prompt_steps=prompt_steps, steps_per_seq=steps_per_seq)
    inputs = (xp2, xs) + ffn_w + (state,)
    in_specs = [tile_p, tile_s] + _ffn_weight_specs(layer) + [st]
    out_specs = [tile_p, tile_s]
    out_shapes = [jax.ShapeDtypeStruct(xp2.shape, F32), jax.ShapeDtypeStruct(xs.shape, F32)]
    if not final:
        up_rows = D_MODEL // prompt_steps
        dn_rows = D_FF // (prompt_steps // 2)
        inputs += (w_up_f32, w_down_f32)
        in_specs += [pl.BlockSpec((None, up_rows, 2 * D_FF), lambda s: (layer + 1, p_idx(s), 0)),
                     pl.BlockSpec((None, dn_rows, D_MODEL), lambda s: (layer + 1, p_idx(s) // 2, 0))]
        out_specs += [pl.BlockSpec((up_rows, 2 * D_FF), lambda s: (p_idx(s), 0)),
                      pl.BlockSpec((dn_rows, D_MODEL), lambda s: (p_idx(s) // 2, 0))]
        out_shapes += [jax.ShapeDtypeStruct((D_MODEL, 2 * D_FF), BF16), jax.ShapeDtypeStruct((D_FF, D_MODEL), BF16)]
    outs = _call_layer(
        kernel_fn, (prompt_steps + sample_steps,), inputs, in_specs, out_specs + [tail_p, st],
        out_shapes + [jax.ShapeDtypeStruct(conv_tails.shape, F32), jax.ShapeDtypeStruct(new_state.shape, F32)],
        [conv_tails, new_state], [pltpu.VMEM((F32_SUBLANES, 2 * D_FF), F32)], "ffn")
    next_w = tuple(outs[2:-2])
    return outs[0].reshape(xp.shape), outs[1], outs[-2], outs[-1], next_w


def _heads_group_major(w, axis):
    shape = w.shape
    split = shape[:axis] + (N_KV_HEADS, GQA_GROUP, HEAD_DIM) + shape[axis + 1:]
    return jnp.swapaxes(w.reshape(split), axis, axis + 1).reshape(shape)


def kernel(x_prompt, x_sample, cache_win_k, cache_win_v, state_conv, norm_mix, w_in, sinks, gmlp_ln_g, gmlp_ln_b, gmlp_ws, gmlp_bs, w_branch_attn, w_branch_gmlp, w_out, norm_ffn, w_up, conv_w, conv_b, w_down, norm_final):
    batch = x_prompt.shape[0]
    dec_batch, dec_seq, _ = x_sample.shape
    w_q_b = _heads_group_major(lax.optimization_barrier(w_in[:, :, :ATTN_W]), 2).astype(BF16)
    w_pa_b = _heads_group_major(w_branch_attn, 1).astype(BF16)
    mixer_cast = (w_in[0].astype(BF16), w_branch_gmlp[0].astype(BF16), w_out[0].astype(BF16))
    norm_mix3 = norm_mix.reshape(DEPTH, 1, D_MODEL)
    mixer_ln = (gmlp_ln_g.reshape(DEPTH, 1, GMLP_W), gmlp_ln_b.reshape(DEPTH, 1, GMLP_W))
    gmlp_w = (gmlp_ws, gmlp_bs.reshape(DEPTH, GMLP_GROUPS, CHUNK, 1))
    ffn_up_down = (w_up[0].astype(BF16), w_down[0].astype(BF16))
    cache_k = jnp.transpose(cache_win_k, (0, 1, 3, 4, 2)).reshape(DEPTH, dec_batch, KV_W, WINDOW)
    cache_v = jnp.transpose(cache_win_v, (0, 1, 3, 4, 2)).reshape(DEPTH, dec_batch, KV_W, WINDOW)

    xp = x_prompt
    xs = x_sample.reshape(dec_batch * dec_seq, D_MODEL)
    kp = jnp.zeros((DEPTH, batch, WINDOW, KV_W), F32)
    vp = jnp.zeros((DEPTH, batch, WINDOW, KV_W), F32)
    cp = jnp.zeros((DEPTH, batch, CONV_W - 1, 2 * D_FF), F32)
    gv = jnp.zeros((DEPTH, dec_batch * dec_seq, GMLP_W), F32)
    ks = jnp.zeros(cache_k.shape, F32)
    vs = jnp.zeros(cache_v.shape, F32)
    cs = jnp.zeros(state_conv.shape, F32)
    for l in range(DEPTH):
        sink_l = sinks[l]
        mixer_w = ((norm_mix3, w_q_b, mixer_cast[0], sink_l) + mixer_ln + gmlp_w
                   + (w_pa_b, mixer_cast[1], mixer_cast[2]))
        xp, kp, vp, mixer_cast = _mixer_prompt(l, xp, mixer_w, (w_in, w_branch_gmlp, w_out), kp, vp)
        xs, ks, vs, gv = _mixer_sample(l, xs, mixer_w, cache_k, cache_v, ks, vs, gv)
        ffn_w = (norm_ffn.reshape(DEPTH, 1, D_MODEL), ffn_up_down[0], conv_w, conv_b.reshape(DEPTH, 1, 2 * D_FF),
                 ffn_up_down[1], norm_final.reshape(1, D_MODEL))
        xp, xs, cp, cs, ffn_up_down = _ffn(l, xp, xs, ffn_w, w_up, w_down, state_conv, cp, cs)

    kv_prompt = (DEPTH, batch, WINDOW, N_KV_HEADS, HEAD_DIM)
    kv_sample_t = (DEPTH, dec_batch, N_KV_HEADS, HEAD_DIM, WINDOW)
    return (xp, xs.reshape(x_sample.shape),
            kp.reshape(kv_prompt), vp.reshape(kv_prompt), cp,
            jnp.transpose(ks.reshape(kv_sample_t), (0, 1, 4, 2, 3)),
            jnp.transpose(vs.reshape(kv_sample_t), (0, 1, 4, 2, 3)), cs,
            gv.reshape(DEPTH, dec_batch, dec_seq, GMLP_W))
```

```python
import functools

import jax
import jax.numpy as jnp
import numpy as np
from jax import lax
from jax.experimental import pallas as pl
from jax.experimental.pallas import tpu as pltpu

D_MODEL = 1024
DEPTH = 4
HEAD_DIM = 64
N_HEADS = 16
N_KV_HEADS = 4
GQA_GROUP = N_HEADS // N_KV_HEADS
ATTN_W = N_HEADS * HEAD_DIM
KV_W = N_KV_HEADS * HEAD_DIM
WINDOW = 128
BLOCK = 128
CHUNK = 128
GMLP_CH = 128
GMLP_GROUPS = 6
GMLP_W = GMLP_GROUPS * GMLP_CH
D_FF = 2816
CONV_W = 3
EPS = 1e-5
NEG = -1e30

K0, V0, GU0, GV0, GA0, GB0, REST_COLS = 1024, 1280, 1536, 2304, 3072, 4096, 5120

V7X_VMEM_BYTES = 64 * 1024 * 1024
VMEM_LIMIT_BYTES = V7X_VMEM_BYTES - 8 * 1024 * 1024
F32_SUBLANES = 8

TM_PROMPT = 512
TM_PROMPT_FFN = 512
FFN_SUBTILE = 256
SEQ_PER_STEP_MIXER = 16
SEQ_PER_STEP_FFN = 32
FFN_CHUNK = 128
FFN_DOWN_K = 256
FFN_LOOKAHEAD = 8
ATTN_LOOKAHEAD = 2
PROJ_TILE = 256

F32 = jnp.float32
BF16 = jnp.bfloat16


def _rmsnorm(x, g):
    return x * lax.rsqrt(jnp.mean(x * x, axis=-1, keepdims=True) + EPS) * g


def _layernorm(x, g, b):
    mu = jnp.mean(x, axis=-1, keepdims=True)
    xc = x - mu
    var = jnp.mean(xc * xc, axis=-1, keepdims=True)
    return xc * lax.rsqrt(var + EPS) * g + b


def _gelu(x):
    c = np.sqrt(2.0 / np.pi).astype(np.float32)
    return x * (0.5 * (1.0 + jnp.tanh(c * (x + 0.044715 * (x * x * x)))))


def _sigmoid(x):
    return 1.0 / (1.0 + jnp.exp(-x))


def _dot(a, b):
    return jnp.dot(a, b, preferred_element_type=F32)


def _dot_nt(a, b):
    return lax.dot_general(a, b, (((1,), (1,)), ((), ())), preferred_element_type=F32)


def _kv_lane_masks(rows):
    lane = lax.broadcasted_iota(jnp.int32, (rows, KV_W), 1)
    return [(lane >= h * HEAD_DIM) & (lane < (h + 1) * HEAD_DIM) for h in range(N_KV_HEADS)]


def _select_kv_lanes(masks, parts):
    out = parts[N_KV_HEADS - 1]
    for h in range(N_KV_HEADS - 2, -1, -1):
        out = jnp.where(masks[h], parts[h], out)
    return out


def _softmax_pv(s, sink, vv):
    m = jnp.maximum(jnp.max(s, axis=-1, keepdims=True), sink)
    p = jnp.exp(s - m)
    denom = jnp.sum(p, axis=-1, keepdims=True) + jnp.exp(sink - m)
    return _dot(p.astype(BF16), vv) / denom


def _gmlp_mix(vn, ws_ref, bs_ref, period):
    vnb = vn.astype(BF16)
    row = lax.broadcasted_iota(jnp.int32, (CHUNK, CHUNK), 0)
    col = lax.broadcasted_iota(jnp.int32, (CHUNK, CHUNK), 1)
    keep = (col <= row) & (col >= row - (row & (period - 1)))
    reps = CHUNK // period
    w, bias = [], []
    for g in range(GMLP_GROUPS):
        if reps == 1:
            wg, bg = ws_ref[g], bs_ref[g]
        else:
            top = ws_ref[g, 0:period, :]
            wg = jnp.concatenate([pltpu.roll(top, i * period, 1) if i else top for i in range(reps)], axis=0)
            bg = jnp.concatenate([bs_ref[g, 0:period, :]] * reps, axis=0)
        w.append(jnp.where(keep, wg, 0.0).astype(BF16))
        bias.append(bg)
    rows = []
    for c in range(vn.shape[0] // CHUNK):
        cols = []
        for g in range(GMLP_GROUPS):
            blk = vnb[c * CHUNK:(c + 1) * CHUNK, g * GMLP_CH:(g + 1) * GMLP_CH]
            cols.append(_dot(w[g], blk) + bias[g])
        rows.append(jnp.concatenate(cols, axis=1))
    return rows[0] if len(rows) == 1 else jnp.concatenate(rows, axis=0)


class _GateProjection:
    def __init__(self, hb, wr_ref):
        self.hb, self.wr_ref, self.tiles = hb, wr_ref, []

    def issue(self, n_tiles):
        for _ in range(n_tiles):
            c0 = GU0 + len(self.tiles) * PROJ_TILE
            if c0 < REST_COLS:
                self.tiles.append(_dot(self.hb, self.wr_ref[:, c0:c0 + PROJ_TILE]))

    def columns(self, c0, c1):
        self.issue((REST_COLS - GU0) // PROJ_TILE)
        return jnp.concatenate(self.tiles[(c0 - GU0) // PROJ_TILE:(c1 - GU0) // PROJ_TILE], axis=1)


def _gate_merge_out(x, o, proj, lng_ref, lnb_ref, ws_ref, bs_ref, wpa_ref, wpb_ref, wout_ref, period):
    u = _gelu(proj.columns(GU0, GV0))
    vn = _layernorm(_gelu(proj.columns(GV0, GA0)), lng_ref[...], lnb_ref[...])
    ga = proj.columns(GA0, GB0)
    gb = proj.columns(GB0, REST_COLS)
    oa = _dot(o.astype(BF16), wpa_ref[...])
    s_gate = u * _gmlp_mix(vn, ws_ref, bs_ref, period)
    merged = _sigmoid(ga) * oa + _sigmoid(gb) * _dot(s_gate.astype(BF16), wpb_ref[...])
    return x + _dot(merged.astype(BF16), wout_ref[...]), vn


def _mixer_prompt_kernel(x_ref, xprev_ref, nm_ref, wq_ref, wr_ref, sink_ref, lng_ref, lnb_ref, ws_ref, bs_ref,
                         wpa_ref, wpb_ref, wout_ref, *rest, cast_next):
    if cast_next:
        win_next_ref, wpb_next_ref, wout_next_ref, xo_ref, win_cast_ref, wpb_cast_ref, wout_cast_ref, kt_ref, vt_ref = rest
        win_cast_ref[...] = win_next_ref[...].astype(BF16)
        wpb_cast_ref[...] = wpb_next_ref[...].astype(BF16)
        wout_cast_ref[...] = wout_next_ref[...].astype(BF16)
    else:
        xo_ref, kt_ref, vt_ref = rest
    i = pl.program_id(1)
    tm = x_ref.shape[0]
    x = x_ref[...]
    hb = _rmsnorm(x, nm_ref[...]).astype(BF16)

    hb_prev = _rmsnorm(xprev_ref[...], nm_ref[...]).astype(BF16)
    kv_prev = _dot(hb_prev, wr_ref[:, K0:GU0])
    k = _dot(hb, wr_ref[:, K0:V0])
    v = _dot(hb, wr_ref[:, V0:GU0])
    q = _dot(hb, wq_ref[...]) * (HEAD_DIM ** -0.5)
    proj = _GateProjection(hb, wr_ref)

    kk_all = jnp.concatenate([kv_prev[:, 0:KV_W], k], axis=0).astype(BF16)
    vv_all = jnp.concatenate([kv_prev[:, KV_W:2 * KV_W], v], axis=0).astype(BF16)
    kt_ref[...] = k[tm - WINDOW:tm]
    vt_ref[...] = v[tm - WINDOW:tm]

    rows = N_KV_HEADS * BLOCK
    t = lax.broadcasted_iota(jnp.int32, (rows, 2 * BLOCK), 0) & (BLOCK - 1)
    c = lax.broadcasted_iota(jnp.int32, (rows, 2 * BLOCK), 1)
    band = (c > t) & (c <= t + WINDOW)
    bias = jnp.where(band, 0.0, NEG).astype(F32)
    first_key = jnp.where(i > 0, 0, BLOCK)
    bias_first = jnp.where(band & (c >= first_key), 0.0, NEG).astype(F32)
    head_of_row = lax.broadcasted_iota(jnp.int32, (rows, 1), 0) >> (BLOCK.bit_length() - 1)
    masks = _kv_lane_masks(BLOCK)

    items = [(j, g) for j in range(tm // BLOCK) for g in range(GQA_GROUP)]

    def scores(item):
        j, g = item
        qg = q[j * BLOCK:(j + 1) * BLOCK, g * KV_W:(g + 1) * KV_W]
        lhs = jnp.concatenate([jnp.where(masks[h], qg, 0.0) for h in range(N_KV_HEADS)], axis=0).astype(BF16)
        return _dot_nt(lhs, kk_all[j * BLOCK:(j + 2) * BLOCK]) + (bias_first if j == 0 else bias)

    fill = -(-((REST_COLS - GU0) // PROJ_TILE) // len(items))
    o_groups = {}
    ahead = [scores(item) for item in items[:ATTN_LOOKAHEAD]]
    for n, (j, g) in enumerate(items):
        s = ahead.pop(0)
        if n + ATTN_LOOKAHEAD < len(items):
            ahead.append(scores(items[n + ATTN_LOOKAHEAD]))
        proj.issue(fill)
        sink = jnp.full((rows, 1), sink_ref[(N_KV_HEADS - 1) * GQA_GROUP + g], F32)
        for h in range(N_KV_HEADS - 2, -1, -1):
            sink = jnp.where(head_of_row == h, sink_ref[h * GQA_GROUP + g], sink)
        r = _softmax_pv(s, sink, vv_all[j * BLOCK:(j + 2) * BLOCK])
        o_groups[(j, g)] = _select_kv_lanes(masks, [r[h * BLOCK:(h + 1) * BLOCK] for h in range(N_KV_HEADS)])
    o_blocks = [jnp.concatenate([o_groups[(j, g)] for g in range(GQA_GROUP)], axis=1) for j in range(tm // BLOCK)]
    o = o_blocks[0] if len(o_blocks) == 1 else jnp.concatenate(o_blocks, axis=0)

    xo_ref[...], _ = _gate_merge_out(x, o, proj, lng_ref, lnb_ref, ws_ref, bs_ref, wpa_ref, wpb_ref, wout_ref, CHUNK)


def _mixer_sample_kernel(x_ref, nm_ref, wq_ref, wr_ref, sink_ref, lng_ref, lnb_ref, ws_ref, bs_ref,
                         wpa_ref, wpb_ref, wout_ref, ckt_ref, cvt_ref,
                         xo_ref, wkt_ref, wvt_ref, vn_ref):
    n_seq, _, past = ckt_ref.shape
    m_rows = x_ref.shape[0]
    t_new = m_rows // n_seq
    assert m_rows == past and past == BLOCK
    x = x_ref[...]
    hb = _rmsnorm(x, nm_ref[...]).astype(BF16)
    q = _dot(hb, wq_ref[...]) * (HEAD_DIM ** -0.5)
    k = _dot(hb, wr_ref[:, K0:V0])
    v = _dot(hb, wr_ref[:, V0:GU0])
    kt = jnp.transpose(k)
    vt = jnp.transpose(v)
    proj = _GateProjection(hb, wr_ref)

    n_keys = 2 * BLOCK
    rows = N_HEADS * t_new
    t = lax.broadcasted_iota(jnp.int32, (rows, n_keys), 0) & (t_new - 1)
    c = lax.broadcasted_iota(jnp.int32, (rows, n_keys), 1)
    diff = t + past - c
    bias = jnp.where((diff >= 0) & (diff < WINDOW) & (c < past + t_new), 0.0, NEG).astype(F32)
    head_slot = lax.broadcasted_iota(jnp.int32, (rows, 1), 0) >> (t_new.bit_length() - 1)
    sink = jnp.zeros((rows, 1), F32)
    for g in range(GQA_GROUP):
        for h in range(N_KV_HEADS):
            sink = jnp.where(head_slot == g * N_KV_HEADS + h, sink_ref[h * GQA_GROUP + g], sink)
    masks = _kv_lane_masks(t_new)
    zero_rows = jnp.zeros((BLOCK - t_new, KV_W), F32)
    keep_old = lax.broadcasted_iota(jnp.int32, (KV_W, past), 1) < past - t_new

    def new_window(old_t, new_t, b):
        shift = (past - t_new - b * t_new) % past
        placed = pltpu.roll(new_t, shift, 1) if shift else new_t
        return jnp.where(keep_old, pltpu.roll(old_t, past - t_new, 1), placed)

    s_list, v_list = [], []
    for b in range(n_seq):
        kct = ckt_ref[b]
        vct = cvt_ref[b]
        wkt_ref[b] = new_window(kct, kt, b)
        wvt_ref[b] = new_window(vct, vt, b)
        rows_b = slice(b * t_new, (b + 1) * t_new)
        k_pad = jnp.concatenate([k[rows_b], zero_rows], axis=0).astype(BF16)
        v_pad = jnp.concatenate([v[rows_b], zero_rows], axis=0).astype(BF16)
        qb = q[rows_b]
        lhs = jnp.concatenate(
            [jnp.where(masks[h], qb[:, g * KV_W:(g + 1) * KV_W], 0.0)
             for g in range(GQA_GROUP) for h in range(N_KV_HEADS)], axis=0).astype(BF16)
        s_list.append(jnp.concatenate([_dot(lhs, kct.astype(BF16)), _dot_nt(lhs, k_pad)], axis=1) + bias)
        v_list.append((vct.astype(BF16), v_pad))
    proj.issue((REST_COLS - GU0) // PROJ_TILE)
    o_rows = []
    for b in range(n_seq):
        s = s_list[b]
        vct_b, v_pad = v_list[b]
        m = jnp.maximum(jnp.max(s, axis=-1, keepdims=True), sink)
        p = jnp.exp(s - m)
        denom = jnp.sum(p, axis=-1, keepdims=True) + jnp.exp(sink - m)
        pb = p.astype(BF16)
        r = (_dot_nt(pb[:, 0:past], vct_b) + _dot(pb[:, past:n_keys], v_pad)) / denom
        o_groups = []
        for g in range(GQA_GROUP):
            base = g * N_KV_HEADS * t_new
            o_groups.append(_select_kv_lanes(
                masks, [r[base + h * t_new:base + (h + 1) * t_new] for h in range(N_KV_HEADS)]))
        o_rows.append(jnp.concatenate(o_groups, axis=1))
    o = jnp.concatenate(o_rows, axis=0)

    xo_ref[...], vn_ref[...] = _gate_merge_out(x, o, proj, lng_ref, lnb_ref, ws_ref, bs_ref,
                                               wpa_ref, wpb_ref, wout_ref, t_new)


def _conv_gate(za, zb, sa, sb, cw_ref, cb_ref, a0, b0, width):
    def conv(z, shifted, c0):
        z1, z2 = shifted
        cols = slice(c0, c0 + width)
        return cb_ref[:, cols] + ((z2 * cw_ref[0:1, cols] + z1 * cw_ref[1:2, cols]) + z * cw_ref[2:3, cols])

    a = conv(za, sa, a0)
    b = conv(zb, sb, b0)
    c0 = np.sqrt(2.0 / np.pi).astype(np.float32)
    c1 = np.float32(c0 * np.float32(0.044715))
    th = jnp.tanh(a * (c1 * (a * a) + c0))
    half_ab = (0.5 * a) * b
    return half_ab + half_ab * th


def _ffn_body(xs, nf_ref, wup_ref, cw_ref, cb_ref, wdn_ref, nfin_ref, shift_fn, final):
    hbs = [_rmsnorm(x, nf_ref[...]).astype(BF16) for x in xs]
    n_chunks = D_FF // FFN_CHUNK
    per_down = FFN_DOWN_K // FFN_CHUNK
    stages = [(n, c) for n in range(len(xs)) for c in range(n_chunks)]

    def up(stage):
        n, c = stage
        a0 = c * FFN_CHUNK
        w = jnp.concatenate([wup_ref[:, a0:a0 + FFN_CHUNK], wup_ref[:, D_FF + a0:D_FF + a0 + FFN_CHUNK]], axis=1)
        return _dot(hbs[n], w)

    accs = list(xs)
    ahead = [up(stage) for stage in stages[:FFN_LOOKAHEAD]]
    pending = []
    for i, (n, c) in enumerate(stages):
        z = ahead.pop(0)
        if i + FFN_LOOKAHEAD < len(stages):
            ahead.append(up(stages[i + FFN_LOOKAHEAD]))
        a0 = c * FFN_CHUNK
        b0 = D_FF + a0
        za, zb = z[:, 0:FFN_CHUNK], z[:, FFN_CHUNK:2 * FFN_CHUNK]
        pending.append(_conv_gate(za, zb, shift_fn(za, a0), shift_fn(zb, b0), cw_ref, cb_ref, a0, b0, FFN_CHUNK))
        if len(pending) == per_down:
            gated = pending[0] if per_down == 1 else jnp.concatenate(pending, axis=1)
            r0 = a0 + FFN_CHUNK - FFN_DOWN_K
            accs[n] = accs[n] + _dot(gated.astype(BF16), wdn_ref[r0:r0 + FFN_DOWN_K, :])
            pending = []
    if final:
        accs = [_rmsnorm(acc, nfin_ref[...]) for acc in accs]
    return accs


def _ffn_kernel(xp_ref, xs_ref, nf_ref, wup_ref, cw_ref, cb_ref, wdn_ref, nfin_ref, st_ref, *rest,
                final, prompt_steps, steps_per_seq):
    if final:
        xpo_ref, xso_ref, ctp_ref, cts_ref, carry_ref = rest
    else:
        wup_next_ref, wdn_next_ref, xpo_ref, xso_ref, wup_cast_ref, wdn_cast_ref, ctp_ref, cts_ref, carry_ref = rest
        wup_cast_ref[...] = wup_next_ref[...].astype(BF16)
        wdn_cast_ref[...] = wdn_next_ref[...].astype(BF16)
    step = pl.program_id(0)
    weights = (nf_ref, wup_ref, cw_ref, cb_ref, wdn_ref, nfin_ref)

    @pl.when(step < prompt_steps)
    def _():
        sub = carry_ref.shape[0]

        @pl.when(step % steps_per_seq == 0)
        def _():
            carry_ref[...] = jnp.zeros_like(carry_ref)

        def shift_fn(z, c0):
            rows = z.shape[0]
            cols = slice(c0, c0 + z.shape[1])
            ext = jnp.concatenate([carry_ref[:, cols], z], axis=0)
            last = z[rows - sub:rows]
            carry_ref[:, cols] = last
            ctp_ref[:, cols] = pltpu.roll(last, CONV_W - 1, 0)[0:CONV_W - 1]
            return pltpu.roll(ext, 1, 0)[sub:], pltpu.roll(ext, 2, 0)[sub:]

        tiles = [slice(r0, r0 + FFN_SUBTILE) for r0 in range(0, xp_ref.shape[0], FFN_SUBTILE)]
        outs = _ffn_body([xp_ref[rows, :] for rows in tiles], *weights, shift_fn, final)
        for rows, out in zip(tiles, outs):
            xpo_ref[rows, :] = out

    @pl.when(step >= prompt_steps)
    def _():
        n_seq = st_ref.shape[0]
        m_rows = xs_ref.shape[0]
        t_new = m_rows // n_seq

        def shift_fn(z, c0):
            width = z.shape[1]
            cols = slice(c0, c0 + width)
            z3 = z.reshape(n_seq, t_new, width)
            st = st_ref[:, :, cols]
            p0 = jnp.broadcast_to(st[:, 0:1, :], z3.shape)
            p1 = jnp.broadcast_to(st[:, 1:2, :], z3.shape)
            t = lax.broadcasted_iota(jnp.int32, z3.shape, 1)
            r1 = pltpu.roll(z3, 1, 1)
            r2 = pltpu.roll(z3, 2, 1)
            cts_ref[:, :, cols] = r2[:, 0:CONV_W - 1, :]
            z1 = jnp.where(t == 0, p1, r1)
            z2 = jnp.where(t == 0, p0, jnp.where(t == 1, p1, r2))
            return z1.reshape(m_rows, width), z2.reshape(m_rows, width)

        (xso_ref[...],) = _ffn_body([xs_ref[...]], *weights, shift_fn, final)


def _drop_aliased(kernel_fn, n_in, n_aliased):
    def wrapped(*refs):
        return kernel_fn(*refs[:n_in], *refs[n_in + n_aliased:])
    return wrapped


def _layer_spec(layer, shape):
    nd = len(shape)
    return pl.BlockSpec((None,) + tuple(shape), lambda *_: (layer,) + (0,) * nd, pipeline_mode=pl.Buffered(1))


def _resident_spec(shape):
    nd = len(shape)
    return pl.BlockSpec(tuple(shape), lambda *_: (0,) * nd, pipeline_mode=pl.Buffered(1))


_SMEM_SPEC = pl.BlockSpec(memory_space=pltpu.SMEM)
_ANY_SPEC = pl.BlockSpec(memory_space=pl.ANY)


def _compiler_params(n_axes):
    return pltpu.CompilerParams(dimension_semantics=("arbitrary",) * n_axes,
                                vmem_limit_bytes=VMEM_LIMIT_BYTES)


def _mixer_weight_specs(layer):
    return [
        _layer_spec(layer, (1, D_MODEL)),
        _layer_spec(layer, (D_MODEL, ATTN_W)),
        _resident_spec((D_MODEL, REST_COLS)),
        _SMEM_SPEC,
        _layer_spec(layer, (1, GMLP_W)),
        _layer_spec(layer, (1, GMLP_W)),
        _layer_spec(layer, (GMLP_GROUPS, CHUNK, CHUNK)),
        _layer_spec(layer, (GMLP_GROUPS, CHUNK, 1)),
        _layer_spec(layer, (ATTN_W, D_MODEL)),
        _resident_spec((GMLP_W, D_MODEL)),
        _resident_spec((D_MODEL, D_MODEL)),
    ]


def _ffn_weight_specs(layer):
    return [
        _layer_spec(layer, (1, D_MODEL)),
        _resident_spec((D_MODEL, 2 * D_FF)),
        _layer_spec(layer, (CONV_W, 2 * D_FF)),
        _layer_spec(layer, (1, 2 * D_FF)),
        _resident_spec((D_FF, D_MODEL)),
        _resident_spec((1, D_MODEL)),
    ]


def _call_layer(kernel_fn, grid, inputs, in_specs, out_specs, out_shapes, carried, scratch, name):
    n_in = len(inputs)
    first_carried = len(out_shapes) - len(carried)
    aliases = {n_in + n: first_carried + n for n in range(len(carried))}
    return pl.pallas_call(
        _drop_aliased(kernel_fn, n_in, len(carried)), grid=grid,
        in_specs=list(in_specs) + [_ANY_SPEC] * len(carried), out_specs=out_specs, out_shape=out_shapes,
        scratch_shapes=scratch, input_output_aliases=aliases,
        compiler_params=_compiler_params(len(grid)), name=name,
    )(*inputs, *carried)


def _mixer_prompt(layer, x, mixer_w, next_f32, k_tails, v_tails):
    batch, seq, _ = x.shape
    tm = TM_PROMPT
    tile = pl.BlockSpec((None, tm, D_MODEL), lambda b, i: (b, i, 0))
    prev_block = pl.BlockSpec((None, BLOCK, D_MODEL), lambda b, i: (b, jnp.maximum(i * (tm // BLOCK) - 1, 0), 0))
    tail = pl.BlockSpec((None, None, WINDOW, KV_W), lambda b, i: (layer, b, 0, 0))
    tail_shape = jax.ShapeDtypeStruct(k_tails.shape, F32)
    cast_next = layer < DEPTH - 1
    inputs = (x, x) + mixer_w
    in_specs = [tile, prev_block] + _mixer_weight_specs(layer)
    out_specs = [tile]
    out_shapes = [jax.ShapeDtypeStruct(x.shape, F32)]
    if cast_next:
        steps_per_seq = seq // tm
        n_steps = batch * steps_per_seq
        rows, pb_rows = D_MODEL // n_steps, GMLP_W // (n_steps // 2)

        def flat(b, i):
            return b * steps_per_seq + i

        inputs += next_f32
        in_specs += [pl.BlockSpec((None, rows, REST_COLS), lambda b, i: (layer + 1, flat(b, i), 0)),
                     pl.BlockSpec((None, pb_rows, D_MODEL), lambda b, i: (layer + 1, flat(b, i) // 2, 0)),
                     pl.BlockSpec((None, rows, D_MODEL), lambda b, i: (layer + 1, flat(b, i), 0))]
        out_specs += [pl.BlockSpec((rows, REST_COLS), lambda b, i: (flat(b, i), 0)),
                      pl.BlockSpec((pb_rows, D_MODEL), lambda b, i: (flat(b, i) // 2, 0)),
                      pl.BlockSpec((rows, D_MODEL), lambda b, i: (flat(b, i), 0))]
        out_shapes += [jax.ShapeDtypeStruct((D_MODEL, REST_COLS), BF16), jax.ShapeDtypeStruct((GMLP_W, D_MODEL), BF16),
                       jax.ShapeDtypeStruct((D_MODEL, D_MODEL), BF16)]
    outs = _call_layer(
        functools.partial(_mixer_prompt_kernel, cast_next=cast_next), (batch, seq // tm), inputs, in_specs,
        out_specs + [tail, tail], out_shapes + [tail_shape, tail_shape], [k_tails, v_tails], [], "mixer_prompt")
    return outs[0], outs[-2], outs[-1], tuple(outs[1:-2])


def _mixer_sample(layer, x, mixer_w, cache_k, cache_v, win_k, win_v, vn_all):
    n_rows = x.shape[0]
    _, n_seq_total, _, past = cache_k.shape
    t_new = n_rows // n_seq_total
    ns = SEQ_PER_STEP_MIXER
    m = ns * t_new
    tile = pl.BlockSpec((m, D_MODEL), lambda i: (i, 0))
    win = pl.BlockSpec((None, ns, KV_W, past), lambda i: (layer, i, 0, 0))
    vn_spec = pl.BlockSpec((None, m, GMLP_W), lambda i: (layer, i, 0))
    return _call_layer(
        _mixer_sample_kernel, (n_seq_total // ns,), (x,) + mixer_w + (cache_k, cache_v),
        [tile] + _mixer_weight_specs(layer) + [win, win], [tile, win, win, vn_spec],
        [jax.ShapeDtypeStruct(x.shape, F32), jax.ShapeDtypeStruct(win_k.shape, F32),
         jax.ShapeDtypeStruct(win_v.shape, F32), jax.ShapeDtypeStruct(vn_all.shape, F32)],
        [win_k, win_v, vn_all], [], "mixer_sample")


def _ffn(layer, xp, xs, ffn_w, w_up_f32, w_down_f32, state, conv_tails, new_state):
    batch, seq, _ = xp.shape
    n_rows_s = xs.shape[0]
    n_seq_total = state.shape[1]
    t_new = n_rows_s // n_seq_total
    tm = TM_PROMPT_FFN
    ns = SEQ_PER_STEP_FFN
    m = ns * t_new
    steps_per_seq = seq // tm
    prompt_steps = batch * steps_per_seq
    sample_steps = n_seq_total // ns

    def p_idx(s):
        return jnp.minimum(s, prompt_steps - 1)

    def s_idx(s):
        return jnp.maximum(s - prompt_steps, 0)

    tile_p = pl.BlockSpec((tm, D_MODEL), lambda s: (p_idx(s), 0))
    tile_s = pl.BlockSpec((m, D_MODEL), lambda s: (s_idx(s), 0))
    st = pl.BlockSpec((None, ns, CONV_W - 1, 2 * D_FF), lambda s: (layer, s_idx(s), 0, 0))
    tail_p = pl.BlockSpec((None, None, CONV_W - 1, 2 * D_FF), lambda s: (layer, p_idx(s) // steps_per_seq, 0, 0))
    xp2 = xp.reshape(batch * seq, D_MODEL)
    final = layer == DEPTH - 1
    kernel_fn = functools.partial(_ffn_kernel, final=final, prompt_steps=prompt_steps, steps_per_seq=steps_per_seq)
    inputs = (xp2, xs) + ffn_w + (state,)
    in_specs = [tile_p, tile_s] + _ffn_weight_specs(layer) + [st]
    out_specs = [tile_p, tile_s]
    out_shapes = [jax.ShapeDtypeStruct(xp2.shape, F32), jax.ShapeDtypeStruct(xs.shape, F32)]
    if not final:
        up_rows = D_MODEL // prompt_steps
        dn_rows = D_FF // (prompt_steps // 2)
        inputs += (w_up_f32, w_down_f32)
        in_specs += [pl.BlockSpec((None, up_rows, 2 * D_FF), lambda s: (layer + 1, p_idx(s), 0)),
                     pl.BlockSpec((None, dn_rows, D_MODEL), lambda s: (layer + 1, p_idx(s) // 2, 0))]
        out_specs += [pl.BlockSpec((up_rows, 2 * D_FF), lambda s: (p_idx(s), 0)),
                      pl.BlockSpec((dn_rows, D_MODEL), lambda s: (p_idx(s) // 2, 0))]
        out_shapes += [jax.ShapeDtypeStruct((D_MODEL, 2 * D_FF), BF16), jax.ShapeDtypeStruct((D_FF, D_MODEL), BF16)]
    outs = _call_layer(
        kernel_fn, (prompt_steps + sample_steps,), inputs, in_specs, out_specs + [tail_p, st],
        out_shapes + [jax.ShapeDtypeStruct(conv_tails.shape, F32), jax.ShapeDtypeStruct(new_state.shape, F32)],
        [conv_tails, new_state], [pltpu.VMEM((F32_SUBLANES, 2 * D_FF), F32)], "ffn")
    next_w = tuple(outs[2:-2])
    return outs[0].reshape(xp.shape), outs[1], outs[-2], outs[-1], next_w


def _heads_group_major(w, axis):
    shape = w.shape
    split = shape[:axis] + (N_KV_HEADS, GQA_GROUP, HEAD_DIM) + shape[axis + 1:]
    return jnp.swapaxes(w.reshape(split), axis, axis + 1).reshape(shape)


def kernel(x_prompt, x_sample, cache_win_k, cache_win_v, state_conv, norm_mix, w_in, sinks, gmlp_ln_g, gmlp_ln_b, gmlp_ws, gmlp_bs, w_branch_attn, w_branch_gmlp, w_out, norm_ffn, w_up, conv_w, conv_b, w_down, norm_final):
    batch = x_prompt.shape[0]
    dec_batch, dec_seq, _ = x_sample.shape
    w_q_b = _heads_group_major(lax.optimization_barrier(w_in[:, :, :ATTN_W]).astype(BF16), 2)
    w_pa_b = _heads_group_major(w_branch_attn.astype(BF16), 1)
    mixer_cast = (w_in[0].astype(BF16), w_branch_gmlp[0].astype(BF16), w_out[0].astype(BF16))
    norm_mix3 = norm_mix.reshape(DEPTH, 1, D_MODEL)
    mixer_ln = (gmlp_ln_g.reshape(DEPTH, 1, GMLP_W), gmlp_ln_b.reshape(DEPTH, 1, GMLP_W))
    gmlp_w = (gmlp_ws, gmlp_bs.reshape(DEPTH, GMLP_GROUPS, CHUNK, 1))
    ffn_up_down = (w_up[0].astype(BF16), w_down[0].astype(BF16))
    cache_k = jnp.transpose(cache_win_k, (0, 1, 3, 4, 2)).reshape(DEPTH, dec_batch, KV_W, WINDOW)
    cache_v = jnp.transpose(cache_win_v, (0, 1, 3, 4, 2)).reshape(DEPTH, dec_batch, KV_W, WINDOW)

    xp = x_prompt
    xs = x_sample.reshape(dec_batch * dec_seq, D_MODEL)
    kp = jnp.zeros((DEPTH, batch, WINDOW, KV_W), F32)
    vp = jnp.zeros((DEPTH, batch, WINDOW, KV_W), F32)
    cp = jnp.zeros((DEPTH, batch, CONV_W - 1, 2 * D_FF), F32)
    gv = jnp.zeros((DEPTH, dec_batch * dec_seq, GMLP_W), F32)
    ks = jnp.zeros(cache_k.shape, F32)
    vs = jnp.zeros(cache_v.shape, F32)
    cs = jnp.zeros(state_conv.shape, F32)
    for l in range(DEPTH):
        sink_l = sinks[l]
        mixer_w = ((norm_mix3, w_q_b, mixer_cast[0], sink_l) + mixer_ln + gmlp_w
                   + (w_pa_b, mixer_cast[1], mixer_cast[2]))
        xp, kp, vp, mixer_cast = _mixer_prompt(l, xp, mixer_w, (w_in, w_branch_gmlp, w_out), kp, vp)
        xs, ks, vs, gv = _mixer_sample(l, xs, mixer_w, cache_k, cache_v, ks, vs, gv)
        ffn_w = (norm_ffn.reshape(DEPTH, 1, D_MODEL), ffn_up_down[0], conv_w, conv_b.reshape(DEPTH, 1, 2 * D_FF),
                 ffn_up_down[1], norm_final.reshape(1, D_MODEL))
        xp, xs, cp, cs, ffn_up_down = _ffn(l, xp, xs, ffn_w, w_up, w_down, state_conv, cp, cs)

    kv_prompt = (DEPTH, batch, WINDOW, N_KV_HEADS, HEAD_DIM)
    kv_sample_t = (DEPTH, dec_batch, N_KV_HEADS, HEAD_DIM, WINDOW)
    return (xp, xs.reshape(x_sample.shape),
            kp.reshape(kv_prompt), vp.reshape(kv_prompt), cp,
            jnp.transpose(ks.reshape(kv_sample_t), (0, 1, 4, 2, 3)),
            jnp.transpose(vs.reshape(kv_sample_t), (0, 1, 4, 2, 3)), cs,
            gv.reshape(DEPTH, dec_batch, dec_seq, GMLP_W))
```
